```python
import jax, jax.numpy as jnp
from jax import lax
import numpy as np

D_MODEL = 2048
BATCH = 4
SEQ = 4096
DEPTH = 4

GRID_W = 64
CTX_LEN = 256
N_MIXERS = 2
N_RET_LAYERS = (DEPTH + 1) // 2
N_MLA_LAYERS = DEPTH // 2

RET_HEADS = 8
RET_DK = D_MODEL // RET_HEADS
RET_DV = 2 * D_MODEL // RET_HEADS
RET_CHUNK = 128
RET_ROPE_BASE = 10000.0
GN_EPS = 1e-6

MLA_HEADS = 16
MLA_Q_LORA = 512
MLA_KV_LORA = 512
MLA_D_NOPE = 128
MLA_D_ROPE = 64
MLA_D_V = 128
MLA_Q_BLOCK = 128
MLA_SCALE = (MLA_D_NOPE + MLA_D_ROPE) ** -0.5
AXIAL_ROPE_BASE = 10000.0
RMS_EPS = 1e-6

FFN_HIDDEN = (((8 * D_MODEL + 2) // 3) + 255) // 256 * 256

DEEPNORM_ALPHA = (2 * DEPTH) ** 0.25
DEEPNORM_BETA = (8 * DEPTH) ** -0.25
LN_EPS = 1e-5

kernel_name = 'hybrid_retention_mla_dit_trunk'


def layer_norm(x, g, b):
    xf = x.astype(jnp.float32)
    mu = jnp.mean(xf, axis=-1, keepdims=True)
    var = jnp.mean(jnp.square(xf - mu), axis=-1, keepdims=True)
    return ((xf - mu) * lax.rsqrt(var + LN_EPS) * g + b).astype(x.dtype)


def rms_norm(x, g):
    xf = x.astype(jnp.float32)
    return (xf * lax.rsqrt(jnp.mean(jnp.square(xf), axis=-1, keepdims=True) + RMS_EPS) * g).astype(x.dtype)


def rope_tables(pos, inv_freq):
    ang = pos.astype(jnp.float32)[:, None] * inv_freq[None, :]
    return jnp.cos(ang), jnp.sin(ang)


def apply_rope(x, cos, sin):
    x1, x2 = jnp.split(x, 2, axis=-1)
    c = cos[:, None, :]
    s = sin[:, None, :]
    return jnp.concatenate([x1 * c - x2 * s, x1 * s + x2 * c], axis=-1).astype(x.dtype)


def apply_rope_2d(x, row_cos, row_sin, col_cos, col_sin):
    xr, xc = jnp.split(x, 2, axis=-1)
    return jnp.concatenate([apply_rope(xr, row_cos, row_sin), apply_rope(xc, col_cos, col_sin)], axis=-1)


def modulation(cond, w, b):
    return jnp.split(jax.nn.silu(cond) @ w + b, 6, axis=-1)


def retention_chunkwise(q, k, v, log_gamma, s0):
    B, H, N, dk = q.shape
    dv = v.shape[-1]
    C = RET_CHUNK
    nc = N // C

    def to_chunks(t):
        return t.reshape(B, H, nc, C, t.shape[-1]).transpose(2, 0, 1, 3, 4)

    idx = jnp.arange(C, dtype=jnp.float32)
    diff = idx[:, None] - idx[None, :]
    intra = jnp.where(diff >= 0, jnp.exp(log_gamma[:, None, None] * jnp.maximum(diff, 0.0)), 0.0)
    q_dec = jnp.exp(log_gamma[:, None] * (idx + 1.0))[:, :, None]
    k_dec = jnp.exp(log_gamma[:, None] * (C - 1.0 - idx))[:, :, None]
    c_dec = jnp.exp(log_gamma * C)[:, None, None]

    def step(S, qkv):
        qc, kc, vc = qkv
        scores = jnp.einsum('bhid,bhjd->bhij', qc, kc) * intra
        o = jnp.einsum('bhij,bhjv->bhiv', scores, vc) + jnp.einsum('bhid,bhdv->bhiv', qc * q_dec, S)
        S = S * c_dec + jnp.einsum('bhjd,bhjv->bhdv', kc * k_dec, vc)
        return S, o

    S, o = lax.scan(step, s0, (to_chunks(q), to_chunks(k), to_chunks(v)))
    return o.transpose(1, 2, 0, 3, 4).reshape(B, H, N, dv), S


def head_group_norm(o):
    mu = jnp.mean(o, axis=-1, keepdims=True)
    var = jnp.mean(jnp.square(o - mu), axis=-1, keepdims=True)
    o = (o - mu) * lax.rsqrt(var + GN_EPS)
    B, H, N, dv = o.shape
    return o.transpose(0, 2, 1, 3).reshape(B, N, H * dv)


def retention_mixer(u_x, u_c, ret_cos, ret_sin, w_qkv, w_g, decay_logit, w_o, ctx_out):
    H, dk, dv = RET_HEADS, RET_DK, RET_DV

    def project(u):
        B, N, _ = u.shape
        q, k, v = jnp.split(u @ w_qkv, [H * dk, 2 * H * dk], axis=-1)
        return q.reshape(B, N, H, dk), k.reshape(B, N, H, dk) * (dk ** -0.5), v.reshape(B, N, H, dv)

    def heads_first(t):
        return t.astype(jnp.float32).transpose(0, 2, 1, 3)

    q_x, k_x, v_x = project(u_x)
    q_x = apply_rope(q_x, ret_cos, ret_sin)
    k_x = apply_rope(k_x, ret_cos, ret_sin)
    q_x, k_x, v_x = heads_first(q_x), heads_first(k_x), heads_first(v_x)
    q_c, k_c, v_c = project(u_c)
    q_c, k_c, v_c = heads_first(q_c), heads_first(k_c), heads_first(v_c)

    log_gamma = jax.nn.log_sigmoid(decay_logit.astype(jnp.float32))
    s0 = jnp.zeros((q_x.shape[0], H, dk, dv), jnp.float32)
    flip = lambda t: jnp.flip(t, axis=2)
    oc_f, sc_f = retention_chunkwise(q_c, k_c, v_c, log_gamma[0], s0)
    ox_f, _ = retention_chunkwise(q_x, k_x, v_x, log_gamma[0], sc_f)
    oc_b, sc_b = retention_chunkwise(flip(q_c), flip(k_c), flip(v_c), log_gamma[1], s0)
    ox_b, _ = retention_chunkwise(flip(q_x), flip(k_x), flip(v_x), log_gamma[1], sc_b)

    def combine(u, o_f, o_b):
        g_f, g_b = jnp.split(u @ w_g, 2, axis=-1)
        y = (jax.nn.silu(g_f) * head_group_norm(o_f).astype(u.dtype)
             + jax.nn.silu(g_b) * head_group_norm(o_b).astype(u.dtype))
        return y @ w_o

    y_x = combine(u_x, ox_f, flip(ox_b))
    y_c = combine(u_c, oc_f, flip(oc_b)) if ctx_out else None
    return y_x, y_c


def mla_attend(qn, qr, kn, kr, v):
    s = jnp.einsum('bqhd,bkhd->bhqk', qn, kn) + jnp.einsum('bqhd,bkd->bhqk', qr, kr)
    p = jax.nn.softmax(s.astype(jnp.float32) * MLA_SCALE, axis=-1)
    return jnp.einsum('bhqk,bkhd->bqhd', p.astype(v.dtype), v)


def mla_block_attention(qn, qr, kn, kr, v):
    B, N, H, _ = qn.shape
    nb = N // MLA_Q_BLOCK

    def blocks(t):
        return t.reshape(B, nb, MLA_Q_BLOCK, *t.shape[2:]).swapaxes(0, 1)

    o = lax.map(lambda qs: mla_attend(qs[0], qs[1], kn, kr, v), (blocks(qn), blocks(qr)))
    return o.swapaxes(0, 1).reshape(B, N, H, MLA_D_V)


def mla_mixer(u_x, u_c, row_cos, row_sin, col_cos, col_sin,
              w_dq, g_q, w_uq, w_dkv, g_kv, w_ukv, w_o, ctx_out):
    H, dn, dr, dv = MLA_HEADS, MLA_D_NOPE, MLA_D_ROPE, MLA_D_V

    def queries(u):
        B, N, _ = u.shape
        q = (rms_norm(u @ w_dq, g_q) @ w_uq).reshape(B, N, H, dn + dr)
        return q[..., :dn], q[..., dn:]

    def keys_values(u):
        B, N, _ = u.shape
        ckv = u @ w_dkv
        c_kv = rms_norm(ckv[..., :MLA_KV_LORA], g_kv)
        kr = ckv[..., MLA_KV_LORA:]
        kv = (c_kv @ w_ukv).reshape(B, N, H, dn + dv)
        return kv[..., :dn], kr, kv[..., dn:]

    qn_x, qr_x = queries(u_x)
    qr_x = apply_rope_2d(qr_x, row_cos, row_sin, col_cos, col_sin)
    kn_x, kr_x, v_x = keys_values(u_x)
    kr_x = apply_rope_2d(kr_x[:, :, None, :], row_cos, row_sin, col_cos, col_sin)[:, :, 0, :]
    kn_c, kr_c, v_c = keys_values(u_c)

    kn = jnp.concatenate([kn_x, kn_c], axis=1)
    kr = jnp.concatenate([kr_x, kr_c], axis=1)
    v = jnp.concatenate([v_x, v_c], axis=1)
    B, N, _ = u_x.shape
    o_x = mla_block_attention(qn_x, qr_x, kn, kr, v)
    y_x = o_x.reshape(B, N, H * dv) @ w_o
    y_c = None
    if ctx_out:
        qn_c, qr_c = queries(u_c)
        o_c = mla_attend(qn_c, qr_c, kn_c, kr_c, v_c)
        y_c = o_c.reshape(B, u_c.shape[1], H * dv) @ w_o
    return y_x, y_c


def swiglu(u, w_in, w_out):
    a, b = jnp.split(u @ w_in, 2, axis=-1)
    return (jax.nn.silu(a) * b) @ w_out


def setup_inputs(seed: int = 0) -> dict:
    key = jax.random.key(seed)
    ks = jax.random.split(key, 21)
    f32 = jnp.float32
    D, F = D_MODEL, FFN_HIDDEN

    def nrm(k, shape, scale):
        return jax.random.normal(k, shape, f32) * scale

    ret_qkv_w = 2 * RET_HEADS * RET_DK + RET_HEADS * RET_DV
    base_logit = jnp.log(2.0 ** (5.0 + jnp.arange(RET_HEADS, dtype=f32)) - 1.0)
    return {
        'x': nrm(ks[0], (BATCH, SEQ, D), 1.0),
        'c': nrm(ks[1], (BATCH, D), 1.0),
        'ctx': nrm(ks[2], (BATCH, CTX_LEN, D), 1.0),
        'c_ctx': nrm(ks[3], (D,), 1.0),
        'ada_w': nrm(ks[4], (DEPTH, D, 6 * D), 0.5 * D ** -0.5),
        'ada_b': nrm(ks[5], (DEPTH, 6 * D), 0.02),
        'ln_g': 1.0 + nrm(ks[6], (DEPTH, 2, D), 0.02),
        'ln_b': nrm(ks[7], (DEPTH, 2, D), 0.02),
        'ret_w_qkv': nrm(ks[8], (N_RET_LAYERS, D, ret_qkv_w), D ** -0.5),
        'ret_w_g': nrm(ks[9], (N_RET_LAYERS, D, 2 * RET_HEADS * RET_DV), D ** -0.5),
        'ret_decay_logit': base_logit + nrm(ks[10], (N_RET_LAYERS, 2, RET_HEADS), 0.1),
        'ret_w_o': nrm(ks[11], (N_RET_LAYERS, RET_HEADS * RET_DV, D), DEEPNORM_BETA * (RET_HEADS * RET_DV) ** -0.5),
        'mla_w_dq': nrm(ks[12], (N_MLA_LAYERS, D, MLA_Q_LORA), D ** -0.5),
        'mla_g_q': 1.0 + nrm(ks[13], (N_MLA_LAYERS, MLA_Q_LORA), 0.02),
        'mla_w_uq': nrm(ks[14], (N_MLA_LAYERS, MLA_Q_LORA, MLA_HEADS * (MLA_D_NOPE + MLA_D_ROPE)), MLA_Q_LORA ** -0.5),
        'mla_w_dkv': nrm(ks[15], (N_MLA_LAYERS, D, MLA_KV_LORA + MLA_D_ROPE), D ** -0.5),
        'mla_g_kv': 1.0 + nrm(ks[16], (N_MLA_LAYERS, MLA_KV_LORA), 0.02),
        'mla_w_ukv': nrm(ks[17], (N_MLA_LAYERS, MLA_KV_LORA, MLA_HEADS * (MLA_D_NOPE + MLA_D_V)), MLA_KV_LORA ** -0.5),
        'mla_w_o': nrm(ks[18], (N_MLA_LAYERS, MLA_HEADS * MLA_D_V, D), DEEPNORM_BETA * (MLA_HEADS * MLA_D_V) ** -0.5),
        'ffn_w_in': nrm(ks[19], (DEPTH, D, 2 * F), D ** -0.5),
        'ffn_w_out': nrm(ks[20], (DEPTH, F, D), DEEPNORM_BETA * F ** -0.5),
    }


def reference(x, c, ctx, c_ctx, ada_w, ada_b, ln_g, ln_b,
              ret_w_qkv, ret_w_g, ret_decay_logit, ret_w_o,
              mla_w_dq, mla_g_q, mla_w_uq, mla_w_dkv, mla_g_kv, mla_w_ukv, mla_w_o,
              ffn_w_in, ffn_w_out):
    N = x.shape[1]
    ROWS = N // GRID_W
    rows = jnp.repeat(jnp.arange(ROWS), GRID_W)
    cols = jnp.tile(jnp.arange(GRID_W), ROWS)
    t = jnp.arange(N)

    axial_dim = MLA_D_ROPE // 2
    axial_inv = AXIAL_ROPE_BASE ** (-jnp.arange(axial_dim // 2, dtype=jnp.float32) * 2.0 / axial_dim)
    row_cos, row_sin = rope_tables(rows, axial_inv)
    col_cos, col_sin = rope_tables(cols, axial_inv)
    ret_inv = RET_ROPE_BASE ** (-jnp.linspace(0.0, 1.0, RET_DK // 2, dtype=jnp.float32))
    ret_cos, ret_sin = rope_tables(t, ret_inv)

    h_x, h_c = x, ctx
    for i in range(DEPTH):
        ctx_out = i < DEPTH - 1
        sh_a, sc_a, g_a, sh_f, sc_f, g_f = [m[:, None, :] for m in modulation(c, ada_w[i], ada_b[i])]
        csh_a, csc_a, cg_a, csh_f, csc_f, cg_f = modulation(c_ctx, ada_w[i], ada_b[i])

        u_x = h_x * (1.0 + sc_a) + sh_a
        u_c = h_c * (1.0 + csc_a) + csh_a
        j = i // N_MIXERS
        if i % N_MIXERS == 0:
            y_x, y_c = retention_mixer(u_x, u_c, ret_cos, ret_sin, ret_w_qkv[j], ret_w_g[j],
                                       ret_decay_logit[j], ret_w_o[j], ctx_out)
        else:
            y_x, y_c = mla_mixer(u_x, u_c, row_cos, row_sin, col_cos, col_sin,
                                 mla_w_dq[j], mla_g_q[j], mla_w_uq[j], mla_w_dkv[j], mla_g_kv[j],
                                 mla_w_ukv[j], mla_w_o[j], ctx_out)
        h_x = layer_norm(DEEPNORM_ALPHA * h_x + g_a * y_x, ln_g[i, 0], ln_b[i, 0])
        f_x = swiglu(h_x * (1.0 + sc_f) + sh_f, ffn_w_in[i], ffn_w_out[i])
        h_x = layer_norm(DEEPNORM_ALPHA * h_x + g_f * f_x, ln_g[i, 1], ln_b[i, 1])
        if ctx_out:
            h_c = layer_norm(DEEPNORM_ALPHA * h_c + cg_a * y_c, ln_g[i, 0], ln_b[i, 0])
            f_c = swiglu(h_c * (1.0 + csc_f) + csh_f, ffn_w_in[i], ffn_w_out[i])
            h_c = layer_norm(DEEPNORM_ALPHA * h_c + cg_f * f_c, ln_g[i, 1], ln_b[i, 1])
    return h_x
```

```python
import functools

import jax
import jax.numpy as jnp
from jax import lax
from jax.experimental import pallas as pl
from jax.experimental.pallas import tpu as pltpu

F32 = jnp.float32
BF16 = jnp.bfloat16

GRID_W = 64
RET_ROPE_BASE = 10000.0
GN_EPS = 1e-6
MLA_HEADS = 16
MLA_D_NOPE = 128
MLA_D_ROPE = 64
MLA_D_V = 128
MLA_SCALE = (MLA_D_NOPE + MLA_D_ROPE) ** -0.5
AXIAL_ROPE_BASE = 10000.0
RMS_EPS = 1e-6
LN_EPS = 1e-5

LANES = 128
MXU_DIM = 256
VMEM_LIMIT_BYTES = 56 * 1024 * 1024

RET_CHUNK = 256
MLA_HEAD_PAD = 2 * LANES


def _params(*sem):
    return pltpu.CompilerParams(dimension_semantics=sem, vmem_limit_bytes=VMEM_LIMIT_BYTES)


def _silu(x):
    return x / (1.0 + jnp.exp(-x))


def _row_tile(n_lat, n_ctx):
    for t in (1024, 512, 256):
        if n_lat % t == 0 and n_ctx % t == 0:
            return t
    raise ValueError("token counts must be multiples of 256")


def _mod_kernel(cond_ref, w_ref, b_ref, o_ref):
    s = _silu(cond_ref[...]).astype(BF16)
    acc = jnp.dot(s, w_ref[0].astype(BF16), preferred_element_type=F32)
    o_ref[0] = acc + b_ref[0]


def _modulation(cond8, ada_w, ada_b):
    depth, d, n = ada_w.shape
    tn = 1024
    return pl.pallas_call(
        _mod_kernel,
        grid=(depth, n // tn),
        in_specs=[
            pl.BlockSpec((8, d), lambda l, j: (0, 0)),
            pl.BlockSpec((1, d, tn), lambda l, j: (l, 0, j)),
            pl.BlockSpec((1, 1, tn), lambda l, j: (l, 0, j)),
        ],
        out_specs=pl.BlockSpec((1, 8, tn), lambda l, j: (l, 0, j)),
        out_shape=jax.ShapeDtypeStruct((depth, 8, n), F32),
        compiler_params=_params("parallel", "parallel"),
    )(cond8, ada_w, ada_b.reshape(depth, 1, n))


def _modulate_kernel(h_ref, sc_ref, sh_ref, u_ref):
    u_ref[...] = (h_ref[...] * (1.0 + sc_ref[0]) + sh_ref[0]).astype(BF16)


def _modulate(h, mod, sc_idx, sh_idx, tm):
    t, d = h.shape
    vec = lambda f: pl.BlockSpec((1, 1, d), lambda i: (f(i), 0, 0))
    return pl.pallas_call(
        _modulate_kernel,
        grid=(t // tm,),
        in_specs=[pl.BlockSpec((tm, d), lambda i: (i, 0)), vec(sc_idx), vec(sh_idx)],
        out_specs=pl.BlockSpec((tm, d), lambda i: (i, 0)),
        out_shape=jax.ShapeDtypeStruct((t, d), BF16),
        compiler_params=_params("parallel"),
    )(h, mod, mod)


def _ret_qkv_kernel(x_ref, w_ref, cos_ref, sin_ref, o_ref, *, n_q, n_k, k_scale):
    j = pl.program_id(1)
    acc = jnp.dot(x_ref[...], w_ref[...], preferred_element_type=F32)
    tn = acc.shape[1]

    @pl.when(j < n_q + n_k)
    def _():
        scale = jnp.where(j < n_q, 1.0, k_scale).astype(F32)
        c = cos_ref[...]
        s = sin_ref[...]
        for g in range(tn // MXU_DIM):
            lo = g * MXU_DIM
            x1 = acc[:, lo:lo + LANES] * scale
            x2 = acc[:, lo + LANES:lo + MXU_DIM] * scale
            o_ref[:, lo:lo + LANES] = (x1 * c - x2 * s).astype(BF16)
            o_ref[:, lo + LANES:lo + MXU_DIM] = (x1 * s + x2 * c).astype(BF16)

    @pl.when(j >= n_q + n_k)
    def _():
        o_ref[...] = acc.astype(BF16)


def _ret_qkv(u, w, cos, sin, n_heads, tm):
    t, d = u.shape
    n = w.shape[1]
    dk = d // n_heads
    tn = 512
    n_q = n_heads * dk // tn
    kern = functools.partial(_ret_qkv_kernel, n_q=n_q, n_k=n_q, k_scale=dk ** -0.5)
    return pl.pallas_call(
        kern,
        grid=(t // tm, n // tn),
        in_specs=[
            pl.BlockSpec((tm, d), lambda i, j: (i, 0)),
            pl.BlockSpec((d, tn), lambda i, j: (0, j)),
            pl.BlockSpec((tm, LANES), lambda i, j: (i, 0)),
            pl.BlockSpec((tm, LANES), lambda i, j: (i, 0)),
        ],
        out_specs=pl.BlockSpec((tm, tn), lambda i, j: (i, j)),
        out_shape=jax.ShapeDtypeStruct((t, n), BF16),
        compiler_params=_params("parallel", "arbitrary"),
    )(u, w, cos, sin)


def _retention_kernel(dec_ref, qf_ref, qb_ref, of_ref, ob_ref, s_ref, intra_ref, *, n_heads, dk, dv):
    c = RET_CHUNK
    step = pl.program_id(1)
    row = lax.broadcasted_iota(jnp.int32, (c, c), 0)
    col = lax.broadcasted_iota(jnp.int32, (c, c), 1)
    pos = lax.broadcasted_iota(jnp.int32, (c, 1), 0).astype(F32)

    def log_gamma(idx):
        x = dec_ref[idx][0:1, 0:1]
        return jnp.minimum(x, 0.0) - jnp.log1p(jnp.exp(-jnp.abs(x)))

    @pl.when(step == 0)
    def _():
        s_ref[...] = jnp.zeros_like(s_ref)
        for d in range(2):
            diff = (row - col) if d == 0 else (col - row)
            dist = jnp.maximum(diff, 0).astype(F32)
            for h in range(n_heads):
                idx = d * n_heads + h
                intra_ref[idx] = jnp.where(diff >= 0, jnp.exp(log_gamma(idx) * dist), 0.0)

    for d, (x_ref, o_ref) in enumerate(((qf_ref, of_ref), (qb_ref, ob_ref))):
        ahead = pos + 1.0 if d == 0 else float(c) - pos
        behind = float(c - 1) - pos if d == 0 else pos
        for h in range(n_heads):
            idx = d * n_heads + h
            lg = log_gamma(idx)
            q = x_ref[:, h * dk:(h + 1) * dk]
            k = x_ref[:, n_heads * dk + h * dk:n_heads * dk + (h + 1) * dk]
            v = x_ref[:, 2 * n_heads * dk + h * dv:2 * n_heads * dk + (h + 1) * dv]
            scores = lax.dot_general(q, k, (((1,), (1,)), ((), ())), preferred_element_type=F32)
            p = (scores * intra_ref[idx]).astype(BF16)
            qd = (q.astype(F32) * jnp.exp(lg * ahead)).astype(BF16)
            state = s_ref[idx]
            o = (jnp.dot(p, v, preferred_element_type=F32)
                 + jnp.dot(qd, state.astype(BF16), preferred_element_type=F32))
            kd = (k.astype(F32) * jnp.exp(lg * behind)).astype(BF16)
            s_ref[idx] = state * jnp.exp(lg * float(c)) + lax.dot_general(
                kd, v, (((0,), (0,)), ((), ())), preferred_element_type=F32)
            mu = jnp.mean(o, axis=-1, keepdims=True)
            ctr = o - mu
            var = jnp.mean(ctr * ctr, axis=-1, keepdims=True)
            o_ref[:, h * dv:(h + 1) * dv] = (ctr * lax.rsqrt(var + GN_EPS)).astype(BF16)


def _retention(qkv, decay_tile, batch, seq, n_heads):
    t, n = qkv.shape
    d = n // 4
    dk, dv = d // n_heads, 2 * d // n_heads
    c = RET_CHUNK
    n_lat = seq // c
    ctx_blk = batch * n_lat

    def fwd_blk(b, s):
        return jnp.where(s == 0, ctx_blk + b, b * n_lat + s - 1)

    def bwd_blk(b, s):
        return jnp.where(s == 0, ctx_blk + b, b * n_lat + n_lat - s)

    kern = functools.partial(_retention_kernel, n_heads=n_heads, dk=dk, dv=dv)
    out = jax.ShapeDtypeStruct((t, n_heads * dv), BF16)
    return pl.pallas_call(
        kern,
        grid=(batch, n_lat + 1),
        in_specs=[
            pl.BlockSpec((2 * n_heads, 8, LANES), lambda b, s: (0, 0, 0)),
            pl.BlockSpec((c, n), lambda b, s: (fwd_blk(b, s), 0)),
            pl.BlockSpec((c, n), lambda b, s: (bwd_blk(b, s), 0)),
        ],
        out_specs=[
            pl.BlockSpec((c, n_heads * dv), lambda b, s: (fwd_blk(b, s), 0)),
            pl.BlockSpec((c, n_heads * dv), lambda b, s: (bwd_blk(b, s), 0)),
        ],
        out_shape=[out, out],
        scratch_shapes=[
            pltpu.VMEM((2 * n_heads, dk, dv), F32),
            pltpu.VMEM((2 * n_heads, c, c), F32),
        ],
        compiler_params=_params("parallel", "arbitrary"),
    )(decay_tile, qkv, qkv)


def _ret_gate_kernel(x_ref, wf_ref, wb_ref, of_ref, ob_ref, y_ref):
    x = x_ref[...]
    gf = jnp.dot(x, wf_ref[...], preferred_element_type=F32)
    gb = jnp.dot(x, wb_ref[...], preferred_element_type=F32)
    y = _silu(gf) * of_ref[...].astype(F32) + _silu(gb) * ob_ref[...].astype(F32)
    y_ref[...] = y.astype(BF16)


def _ret_gate(u, w_g, o_f, o_b, tm):
    t, d = u.shape
    n = w_g.shape[1] // 2
    tn = 512
    nb = n // tn
    return pl.pallas_call(
        _ret_gate_kernel,
        grid=(t // tm, nb),
        in_specs=[
            pl.BlockSpec((tm, d), lambda i, j: (i, 0)),
            pl.BlockSpec((d, tn), lambda i, j: (0, j)),
            pl.BlockSpec((d, tn), lambda i, j: (0, nb + j)),
            pl.BlockSpec((tm, tn), lambda i, j: (i, j)),
            pl.BlockSpec((tm, tn), lambda i, j: (i, j)),
        ],
        out_specs=pl.BlockSpec((tm, tn), lambda i, j: (i, j)),
        out_shape=jax.ShapeDtypeStruct((t, n), BF16),
        compiler_params=_params("parallel", "arbitrary"),
    )(u, w_g, w_g, o_f, o_b)


def _out_ln_kernel(x_ref, w_ref, h_ref, gate_ref, lng_ref, lnb_ref, sc_ref, sh_ref,
                   hout_ref, u_ref, acc_ref, *, alpha):
    k = pl.program_id(1)

    @pl.when(k == 0)
    def _():
        acc_ref[...] = jnp.zeros_like(acc_ref)

    acc_ref[...] += jnp.dot(x_ref[...], w_ref[...], preferred_element_type=F32)

    @pl.when(k == pl.num_programs(1) - 1)
    def _():
        z = alpha * h_ref[...] + gate_ref[0] * acc_ref[...]
        mu = jnp.mean(z, axis=-1, keepdims=True)
        ctr = z - mu
        var = jnp.mean(ctr * ctr, axis=-1, keepdims=True)
        hn = ctr * lax.rsqrt(var + LN_EPS) * lng_ref[0] + lnb_ref[0]
        hout_ref[...] = hn
        u_ref[...] = (hn * (1.0 + sc_ref[0]) + sh_ref[0]).astype(BF16)


def _out_ln(x, w, h, mod, ln_g, ln_b, *, gate_idx, ln_idx, sc_idx, sh_idx, alpha, n_rows, tm, tk):
    kdim = x.shape[1]
    d = w.shape[1]
    vec = lambda arr_idx: pl.BlockSpec((1, 1, d), lambda i, k: (arr_idx(i), 0, 0))
    kern = functools.partial(_out_ln_kernel, alpha=alpha)
    return pl.pallas_call(
        kern,
        grid=(n_rows // tm, kdim // tk),
        in_specs=[
            pl.BlockSpec((tm, tk), lambda i, k: (i, k)),
            pl.BlockSpec((tk, d), lambda i, k: (k, 0)),
            pl.BlockSpec((tm, d), lambda i, k: (i, 0)),
            vec(gate_idx),
            vec(lambda i: ln_idx),
            vec(lambda i: ln_idx),
            vec(sc_idx),
            vec(sh_idx),
        ],
        out_specs=[
            pl.BlockSpec((tm, d), lambda i, k: (i, 0)),
            pl.BlockSpec((tm, d), lambda i, k: (i, 0)),
        ],
        out_shape=[
            jax.ShapeDtypeStruct((n_rows, d), F32),
            jax.ShapeDtypeStruct((n_rows, d), BF16),
        ],
        scratch_shapes=[pltpu.VMEM((tm, d), F32)],
        compiler_params=_params("parallel", "arbitrary"),
    )(x, w, h, mod, ln_g, ln_b, mod, mod)


def _ffn_in_kernel(x_ref, wa_ref, wb_ref, o_ref):
    x = x_ref[...]
    a = jnp.dot(x, wa_ref[...], preferred_element_type=F32)
    b = jnp.dot(x, wb_ref[...], preferred_element_type=F32)
    o_ref[...] = (_silu(a) * b).astype(BF16)


def _ffn_in(u, w_in, n_rows, tm):
    d = u.shape[1]
    f = w_in.shape[1] // 2
    tn = 512
    nb = f // tn
    return pl.pallas_call(
        _ffn_in_kernel,
        grid=(n_rows // tm, nb),
        in_specs=[
            pl.BlockSpec((tm, d), lambda i, j: (i, 0)),
            pl.BlockSpec((d, tn), lambda i, j: (0, j)),
            pl.BlockSpec((d, tn), lambda i, j: (0, nb + j)),
        ],
        out_specs=pl.BlockSpec((tm, tn), lambda i, j: (i, j)),
        out_shape=jax.ShapeDtypeStruct((n_rows, f), BF16),
        compiler_params=_params("parallel", "arbitrary"),
    )(u, w_in, w_in)


def _rope_lanes(x, c_ref, s_up_ref, s_dn_ref):
    half = MLA_D_ROPE // 4
    up = pltpu.roll(x, LANES - half, 1)
    dn = pltpu.roll(x, half, 1)
    return x * c_ref[...] + up * s_up_ref[...] + dn * s_dn_ref[...]


def _rms(x, g):
    return x * lax.rsqrt(jnp.mean(x * x, axis=-1, keepdims=True) + RMS_EPS) * g


def _mla_down_kernel(x_ref, w_ref, gq_ref, gkv_ref, c_ref, su_ref, sd_ref,
                     cq_ref, ckv_ref, kr_ref, *, rq, rkv):
    acc = jnp.dot(x_ref[...], w_ref[...], preferred_element_type=F32)
    cq_ref[...] = _rms(acc[:, :rq], gq_ref[...]).astype(BF16)
    ckv_ref[...] = _rms(acc[:, rq:rq + rkv], gkv_ref[...]).astype(BF16)
    kr_ref[...] = _rope_lanes(acc[:, rq + rkv:], c_ref, su_ref, sd_ref).astype(BF16)


def _mla_down(u, w_down, g_q, g_kv, tabs, tm):
    t, d = u.shape
    rq, rkv = g_q.shape[1], g_kv.shape[1]
    n = w_down.shape[1]
    tab = pl.BlockSpec((tm, LANES), lambda i: (i, 0))
    kern = functools.partial(_mla_down_kernel, rq=rq, rkv=rkv)
    return pl.pallas_call(
        kern,
        grid=(t // tm,),
        in_specs=[
            pl.BlockSpec((tm, d), lambda i: (i, 0)),
            pl.BlockSpec((d, n), lambda i: (0, 0)),
            pl.BlockSpec((1, rq), lambda i: (0, 0)),
            pl.BlockSpec((1, rkv), lambda i: (0, 0)),
            tab, tab, tab,
        ],
        out_specs=[
            pl.BlockSpec((tm, rq), lambda i: (i, 0)),
            pl.BlockSpec((tm, rkv), lambda i: (i, 0)),
            pl.BlockSpec((tm, LANES), lambda i: (i, 0)),
        ],
        out_shape=[
            jax.ShapeDtypeStruct((t, rq), BF16),
            jax.ShapeDtypeStruct((t, rkv), BF16),
            jax.ShapeDtypeStruct((t, LANES), BF16),
        ],
        compiler_params=_params("parallel"),
    )(u, w_down, g_q, g_kv, *tabs)


def _mla_up_kernel(cq_ref, ckv_ref, kr_ref, wq_ref, wkv_ref, c_ref, su_ref, sd_ref,
                   q_ref, k_ref, v_ref):
    cq = cq_ref[...]
    ckv = ckv_ref[...]
    kr = kr_ref[...]
    hp = MLA_HEAD_PAD
    for h in range(MLA_HEADS):
        qh = jnp.dot(cq, wq_ref[:, h * hp:(h + 1) * hp], preferred_element_type=F32) * MLA_SCALE
        q_ref[:, h * hp:h * hp + LANES] = qh[:, :LANES].astype(BF16)
        q_ref[:, h * hp + LANES:(h + 1) * hp] = _rope_lanes(qh[:, LANES:], c_ref, su_ref, sd_ref).astype(BF16)
        kvh = jnp.dot(ckv, wkv_ref[:, h * hp:(h + 1) * hp], preferred_element_type=F32)
        k_ref[:, h * hp:h * hp + LANES] = kvh[:, :LANES].astype(BF16)
        k_ref[:, h * hp + LANES:(h + 1) * hp] = kr
        v_ref[:, h * MLA_D_V:(h + 1) * MLA_D_V] = kvh[:, LANES:].astype(BF16)


def _mla_up(cq, ckv, kr, wq_pad, wkv, tabs, tm):
    t, rq = cq.shape
    rkv = ckv.shape[1]
    nq = wq_pad.shape[1]
    nkv = wkv.shape[1]
    tab = pl.BlockSpec((tm, LANES), lambda i: (i, 0))
    return pl.pallas_call(
        _mla_up_kernel,
        grid=(t // tm,),
        in_specs=[
            pl.BlockSpec((tm, rq), lambda i: (i, 0)),
            pl.BlockSpec((tm, rkv), lambda i: (i, 0)),
            pl.BlockSpec((tm, LANES), lambda i: (i, 0)),
            pl.BlockSpec((rq, nq), lambda i: (0, 0)),
            pl.BlockSpec((rkv, nkv), lambda i: (0, 0)),
            tab, tab, tab,
        ],
        out_specs=[
            pl.BlockSpec((tm, nq), lambda i: (i, 0)),
            pl.BlockSpec((tm, nq), lambda i: (i, 0)),
            pl.BlockSpec((tm, MLA_HEADS * MLA_D_V), lambda i: (i, 0)),
        ],
        out_shape=[
            jax.ShapeDtypeStruct((t, nq), BF16),
            jax.ShapeDtypeStruct((t, nq), BF16),
            jax.ShapeDtypeStruct((t, MLA_HEADS * MLA_D_V), BF16),
        ],
        compiler_params=_params("parallel"),
    )(cq, ckv, kr, wq_pad, wkv, *tabs)


def _scores(q, k):
    return lax.dot_general(q, k, (((1,), (1,)), ((), ())), preferred_element_type=F32)


def _attn_lat_kernel(q_ref, kl_ref, kc_ref, vl_ref, vc_ref, o_ref):
    q = q_ref[...]
    s_l = _scores(q, kl_ref[...])
    s_c = _scores(q, kc_ref[...])
    m = jnp.maximum(jnp.max(s_l, axis=-1, keepdims=True), jnp.max(s_c, axis=-1, keepdims=True))
    p_l = jnp.exp(s_l - m)
    p_c = jnp.exp(s_c - m)
    denom = jnp.sum(p_l, axis=-1, keepdims=True) + jnp.sum(p_c, axis=-1, keepdims=True)
    o = (jnp.dot(p_l.astype(BF16), vl_ref[...], preferred_element_type=F32)
         + jnp.dot(p_c.astype(BF16), vc_ref[...], preferred_element_type=F32))
    o_ref[...] = (o / denom).astype(BF16)


def _attn_ctx_kernel(q_ref, kc_ref, vc_ref, aliased_ref, o_ref):
    del aliased_ref
    s_c = _scores(q_ref[...], kc_ref[...])
    p_c = jnp.exp(s_c - jnp.max(s_c, axis=-1, keepdims=True))
    denom = jnp.sum(p_c, axis=-1, keepdims=True)
    o = jnp.dot(p_c.astype(BF16), vc_ref[...], preferred_element_type=F32)
    o_ref[...] = (o / denom).astype(BF16)


def _attention(q, k, v, batch, seq, ctx_len, with_ctx_queries):
    t = q.shape[0]
    hp, dv = MLA_HEAD_PAD, MLA_D_V
    tq = 256
    nq = seq // tq
    ctx0 = batch * seq // ctx_len
    n_rows = t if with_ctx_queries else batch * seq
    o_lat = pl.pallas_call(
        _attn_lat_kernel,
        grid=(batch, MLA_HEADS, nq),
        in_specs=[
            pl.BlockSpec((tq, hp), lambda b, h, i: (b * nq + i, h)),
            pl.BlockSpec((seq, hp), lambda b, h, i: (b, h)),
            pl.BlockSpec((ctx_len, hp), lambda b, h, i: (ctx0 + b, h)),
            pl.BlockSpec((seq, dv), lambda b, h, i: (b, h)),
            pl.BlockSpec((ctx_len, dv), lambda b, h, i: (ctx0 + b, h)),
        ],
        out_specs=pl.BlockSpec((tq, dv), lambda b, h, i: (b * nq + i, h)),
        out_shape=jax.ShapeDtypeStruct((n_rows, MLA_HEADS * dv), BF16),
        compiler_params=_params("parallel", "parallel", "arbitrary"),
    )(q, k, k, v, v)
    if not with_ctx_queries:
        return o_lat
    return pl.pallas_call(
        _attn_ctx_kernel,
        grid=(batch, MLA_HEADS),
        in_specs=[
            pl.BlockSpec((ctx_len, hp), lambda b, h: (ctx0 + b, h)),
            pl.BlockSpec((ctx_len, hp), lambda b, h: (ctx0 + b, h)),
            pl.BlockSpec((ctx_len, dv), lambda b, h: (ctx0 + b, h)),
            pl.BlockSpec(memory_space=pl.ANY),
        ],
        out_specs=pl.BlockSpec((ctx_len, dv), lambda b, h: (ctx0 + b, h)),
        out_shape=jax.ShapeDtypeStruct((n_rows, MLA_HEADS * dv), BF16),
        input_output_aliases={3: 0},
        compiler_params=_params("parallel", "parallel"),
    )(q, k, v, o_lat)


def _retention_tables(seq, n_ctx, dk):
    inv = RET_ROPE_BASE ** (-jnp.linspace(0.0, 1.0, dk // 2, dtype=F32))
    ang = jnp.arange(seq, dtype=F32)[:, None] * inv[None, :]
    return jnp.cos(ang), jnp.sin(ang), jnp.ones((n_ctx, dk // 2), F32), jnp.zeros((n_ctx, dk // 2), F32)


def _axial_tables(seq):
    n_f = MLA_D_ROPE // 4
    inv = AXIAL_ROPE_BASE ** (-jnp.arange(n_f, dtype=F32) * 2.0 / (MLA_D_ROPE // 2))
    tpos = jnp.arange(seq)
    row = (tpos // GRID_W).astype(F32)[:, None] * inv[None, :]
    col = (tpos % GRID_W).astype(F32)[:, None] * inv[None, :]
    z = jnp.zeros((seq, n_f), F32)
    tail = LANES - MLA_D_ROPE
    cos = jnp.concatenate([jnp.cos(row), jnp.cos(row), jnp.cos(col), jnp.cos(col), jnp.ones((seq, tail), F32)], 1)
    s_up = jnp.concatenate([-jnp.sin(row), z, -jnp.sin(col), z, jnp.zeros((seq, tail), F32)], 1)
    s_dn = jnp.concatenate([z, jnp.sin(row), z, jnp.sin(col), jnp.zeros((seq, tail), F32)], 1)
    return cos, s_up, s_dn


def _token_table(lat, batch, ctx_rows, fill):
    return jnp.concatenate([jnp.tile(lat, (batch, 1)), jnp.full((ctx_rows, lat.shape[1]), fill, F32)], 0)


def kernel(x, c, ctx, c_ctx, ada_w, ada_b, ln_g, ln_b, ret_w_qkv, ret_w_g, ret_decay_logit, ret_w_o,
           mla_w_dq, mla_g_q, mla_w_uq, mla_w_dkv, mla_g_kv, mla_w_ukv, mla_w_o, ffn_w_in, ffn_w_out):
    batch, seq, d = x.shape
    ctx_len = ctx.shape[1]
    depth = ada_w.shape[0]
    n_lat, n_ctx = batch * seq, batch * ctx_len
    n_tok = n_lat + n_ctx
    ret_heads = ret_decay_logit.shape[-1]
    assert ctx_len == RET_CHUNK and seq % RET_CHUNK == 0 and batch + 1 <= 8
    tm = _row_tile(n_lat, n_ctx)
    alpha = (2 * depth) ** 0.25

    cond8 = jnp.concatenate([c, c_ctx[None, :], jnp.zeros((8 - batch - 1, d), F32)], 0)
    mod = _modulation(cond8, ada_w, ada_b).reshape(depth * 8 * 6, 1, d)

    def mod_idx(layer, chunk, tile):
        def f(i):
            r = jnp.where(i * tile >= n_lat, batch, (i * tile) // seq)
            return (layer * 8 + r) * 6 + chunk
        return f

    ln_g3 = ln_g.reshape(depth * 2, 1, d)
    ln_b3 = ln_b.reshape(depth * 2, 1, d)

    ret_cos, ret_sin, one_c, zero_c = _retention_tables(seq, n_ctx, d // ret_heads)
    ret_cos = jnp.concatenate([jnp.tile(ret_cos, (batch, 1)), one_c], 0)
    ret_sin = jnp.concatenate([jnp.tile(ret_sin, (batch, 1)), zero_c], 0)
    ax_cos, ax_up, ax_dn = _axial_tables(seq)
    ax_tabs = (_token_table(ax_cos, batch, n_ctx, 1.0), _token_table(ax_up, batch, n_ctx, 0.0),
               _token_table(ax_dn, batch, n_ctx, 0.0))

    h = jnp.concatenate([x.reshape(n_lat, d), ctx.reshape(n_ctx, d)], 0)
    u = _modulate(h, mod, mod_idx(0, 1, tm), mod_idx(0, 0, tm), tm)
    tm_ln = 512

    for i in range(depth):
        last = i == depth - 1
        n_rows = n_lat if last else n_tok
        j = i // 2
        if i % 2 == 0:
            qkv = _ret_qkv(u, ret_w_qkv[j].astype(BF16), ret_cos, ret_sin, ret_heads, tm)
            decay_tile = jnp.broadcast_to(
                ret_decay_logit[j].astype(F32).reshape(2 * ret_heads, 1, 1), (2 * ret_heads, 8, LANES))
            o_f, o_b = _retention(qkv, decay_tile, batch, seq, ret_heads)
            y = _ret_gate(u, ret_w_g[j].astype(BF16), o_f, o_b, tm)
            w_o = ret_w_o[j].astype(BF16)
        else:
            rq, rkv = mla_g_q.shape[1], mla_g_kv.shape[1]
            w_down = jnp.concatenate(
                [mla_w_dq[j], mla_w_dkv[j], jnp.zeros((d, LANES - MLA_D_ROPE), F32)], 1).astype(BF16)
            cq, ckv, kr = _mla_down(u, w_down, mla_g_q[j][None, :], mla_g_kv[j][None, :], ax_tabs, 512)
            wq_pad = jnp.pad(
                mla_w_uq[j].reshape(rq, MLA_HEADS, MLA_D_NOPE + MLA_D_ROPE),
                ((0, 0), (0, 0), (0, MLA_HEAD_PAD - MLA_D_NOPE - MLA_D_ROPE)),
            ).reshape(rq, MLA_HEADS * MLA_HEAD_PAD).astype(BF16)
            q, k, v = _mla_up(cq, ckv, kr, wq_pad, mla_w_ukv[j].astype(BF16), ax_tabs, 512)
            y = _attention(q, k, v, batch, seq, ctx_len, not last)
            w_o = mla_w_o[j].astype(BF16)

        h, u = _out_ln(y, w_o, h, mod, ln_g3, ln_b3, gate_idx=mod_idx(i, 2, tm_ln), ln_idx=2 * i,
                       sc_idx=mod_idx(i, 4, tm_ln), sh_idx=mod_idx(i, 3, tm_ln), alpha=alpha,
                       n_rows=n_rows, tm=tm_ln, tk=1024)
        hid = _ffn_in(u, ffn_w_in[i].astype(BF16), n_rows, tm)
        nxt = min(i + 1, depth - 1)
        h, u = _out_ln(hid, ffn_w_out[i].astype(BF16), h, mod, ln_g3, ln_b3, gate_idx=mod_idx(i, 5, tm_ln),
                       ln_idx=2 * i + 1, sc_idx=mod_idx(nxt, 1, tm_ln), sh_idx=mod_idx(nxt, 0, tm_ln),
                       alpha=alpha, n_rows=n_rows, tm=tm_ln, tk=512)
    return h.reshape(batch, seq, d)
```

```python
import functools

import jax
import jax.numpy as jnp
from jax import lax
from jax.experimental import pallas as pl
from jax.experimental.pallas import tpu as pltpu

F32 = jnp.float32
BF16 = jnp.bfloat16

GRID_W = 64
RET_ROPE_BASE = 10000.0
GN_EPS = 1e-6
MLA_HEADS = 16
MLA_D_NOPE = 128
MLA_D_ROPE = 64
MLA_D_V = 128
MLA_SCALE = (MLA_D_NOPE + MLA_D_ROPE) ** -0.5
MLA_Q_SCALE = MLA_SCALE * 1.4426950408889634
AXIAL_ROPE_BASE = 10000.0
RMS_EPS = 1e-6
LN_EPS = 1e-5

LANES = 128
MXU_DIM = 256
VMEM_LIMIT_BYTES = 56 * 1024 * 1024

RET_CHUNK = 256
MLA_HEAD_PAD = 2 * LANES


def _params(*sem):
    return pltpu.CompilerParams(dimension_semantics=sem, vmem_limit_bytes=VMEM_LIMIT_BYTES)


def _silu(x):
    return x / (1.0 + jnp.exp(-x))


def _row_tile(n_lat, n_ctx):
    for t in (1024, 512, 256):
        if n_lat % t == 0 and n_ctx % t == 0:
            return t
    raise ValueError("token counts must be multiples of 256")


def _mod_kernel(cond_ref, w_ref, b_ref, o_ref):
    s = _silu(cond_ref[...]).astype(BF16)
    acc = jnp.dot(s, w_ref[0].astype(BF16), preferred_element_type=F32)
    o_ref[0] = acc + b_ref[0]


def _modulation(cond8, ada_w, ada_b):
    depth, d, n = ada_w.shape
    tn = 1024
    return pl.pallas_call(
        _mod_kernel,
        grid=(depth, n // tn),
        in_specs=[
            pl.BlockSpec((8, d), lambda l, j: (0, 0)),
            pl.BlockSpec((1, d, tn), lambda l, j: (l, 0, j)),
            pl.BlockSpec((1, 1, tn), lambda l, j: (l, 0, j)),
        ],
        out_specs=pl.BlockSpec((1, 8, tn), lambda l, j: (l, 0, j)),
        out_shape=jax.ShapeDtypeStruct((depth, 8, n), F32),
        compiler_params=_params("parallel", "parallel"),
        name="adaln_mod",
    )(cond8, ada_w, ada_b.reshape(depth, 1, n))


def _modulate_kernel(h_ref, sc_ref, sh_ref, u_ref):
    u_ref[...] = (h_ref[...] * (1.0 + sc_ref[0]) + sh_ref[0]).astype(BF16)


def _modulate(h, mod, sc_idx, sh_idx, tm):
    t, d = h.shape
    vec = lambda f: pl.BlockSpec((1, 1, d), lambda i: (f(i), 0, 0))
    return pl.pallas_call(
        _modulate_kernel,
        grid=(t // tm,),
        in_specs=[pl.BlockSpec((tm, d), lambda i: (i, 0)), vec(sc_idx), vec(sh_idx)],
        out_specs=pl.BlockSpec((tm, d), lambda i: (i, 0)),
        out_shape=jax.ShapeDtypeStruct((t, d), BF16),
        compiler_params=_params("parallel"),
        name="modulate_in",
    )(h, mod, mod)


def _ret_qkv_kernel(x_ref, w_ref, cos_ref, sin_ref, o_ref, *, n_q, n_k, k_scale):
    j = pl.program_id(1)
    acc = jnp.dot(x_ref[...], w_ref[...], preferred_element_type=F32)
    tn = acc.shape[1]

    @pl.when(j < n_q + n_k)
    def _():
        scale = jnp.where(j < n_q, 1.0, k_scale).astype(F32)
        c = cos_ref[...]
        s = sin_ref[...]
        for g in range(tn // MXU_DIM):
            lo = g * MXU_DIM
            x1 = acc[:, lo:lo + LANES] * scale
            x2 = acc[:, lo + LANES:lo + MXU_DIM] * scale
            o_ref[:, lo:lo + LANES] = (x1 * c - x2 * s).astype(BF16)
            o_ref[:, lo + LANES:lo + MXU_DIM] = (x1 * s + x2 * c).astype(BF16)

    @pl.when(j >= n_q + n_k)
    def _():
        o_ref[...] = acc.astype(BF16)


def _ret_qkv(u, w, layer, cos, sin, n_heads, tm):
    t, d = u.shape
    n = w.shape[2]
    dk = d // n_heads
    tn = 512
    n_q = n_heads * dk // tn
    kern = functools.partial(_ret_qkv_kernel, n_q=n_q, n_k=n_q, k_scale=dk ** -0.5)
    return pl.pallas_call(
        kern,
        grid=(t // tm, n // tn),
        in_specs=[
            pl.BlockSpec((tm, d), lambda i, j: (i, 0)),
            pl.BlockSpec((None, d, tn), lambda i, j: (layer, 0, j)),
            pl.BlockSpec((tm, LANES), lambda i, j: (i, 0)),
            pl.BlockSpec((tm, LANES), lambda i, j: (i, 0)),
        ],
        out_specs=pl.BlockSpec((tm, tn), lambda i, j: (i, j)),
        out_shape=jax.ShapeDtypeStruct((t, n), BF16),
        compiler_params=_params("parallel", "arbitrary"),
        name="ret_qkv",
    )(u, w, cos, sin)


def _retention_kernel(dec_ref, qf_ref, qb_ref, of_ref, ob_ref, s_ref, intra_ref, *, n_heads, dk, dv):
    c = RET_CHUNK
    step = pl.program_id(1)
    row = lax.broadcasted_iota(jnp.int32, (c, c), 0)
    col = lax.broadcasted_iota(jnp.int32, (c, c), 1)
    pos = lax.broadcasted_iota(jnp.int32, (c, 1), 0).astype(F32)

    def log_gamma(idx):
        x = dec_ref[idx][0:1, 0:1]
        return jnp.minimum(x, 0.0) - jnp.log1p(jnp.exp(-jnp.abs(x)))

    @pl.when(step == 0)
    def _():
        s_ref[...] = jnp.zeros_like(s_ref)
        for d in range(2):
            diff = (row - col) if d == 0 else (col - row)
            dist = jnp.maximum(diff, 0).astype(F32)
            for h in range(n_heads):
                idx = d * n_heads + h
                intra_ref[idx] = jnp.where(diff >= 0, jnp.exp(log_gamma(idx) * dist), 0.0)

    for d, (x_ref, o_ref) in enumerate(((qf_ref, of_ref), (qb_ref, ob_ref))):
        ahead = pos + 1.0 if d == 0 else float(c) - pos
        behind = float(c - 1) - pos if d == 0 else pos
        for h in range(n_heads):
            idx = d * n_heads + h
            lg = log_gamma(idx)
            q = x_ref[:, h * dk:(h + 1) * dk]
            k = x_ref[:, n_heads * dk + h * dk:n_heads * dk + (h + 1) * dk]
            v = x_ref[:, 2 * n_heads * dk + h * dv:2 * n_heads * dk + (h + 1) * dv]
            scores = lax.dot_general(q, k, (((1,), (1,)), ((), ())), preferred_element_type=F32)
            p = (scores * intra_ref[idx]).astype(BF16)
            qd = (q.astype(F32) * jnp.exp(lg * ahead)).astype(BF16)
            state = s_ref[idx]
            o = (jnp.dot(p, v, preferred_element_type=F32)
                 + jnp.dot(qd, state.astype(BF16), preferred_element_type=F32))
            kd = (k.astype(F32) * jnp.exp(lg * behind)).astype(BF16)
            s_ref[idx] = state * jnp.exp(lg * float(c)) + lax.dot_general(
                kd, v, (((0,), (0,)), ((), ())), preferred_element_type=F32)
            mu = jnp.mean(o, axis=-1, keepdims=True)
            ctr = o - mu
            var = jnp.mean(ctr * ctr, axis=-1, keepdims=True)
            o_ref[:, h * dv:(h + 1) * dv] = (ctr * lax.rsqrt(var + GN_EPS)).astype(BF16)


def _retention(qkv, decay_tile, batch, seq, n_heads):
    t, n = qkv.shape
    d = n // 4
    dk, dv = d // n_heads, 2 * d // n_heads
    c = RET_CHUNK
    n_lat = seq // c
    ctx_blk = batch * n_lat

    def fwd_blk(b, s):
        return jnp.where(s == 0, ctx_blk + b, b * n_lat + s - 1)

    def bwd_blk(b, s):
        return jnp.where(s == 0, ctx_blk + b, b * n_lat + n_lat - s)

    kern = functools.partial(_retention_kernel, n_heads=n_heads, dk=dk, dv=dv)
    out = jax.ShapeDtypeStruct((t, n_heads * dv), BF16)
    return pl.pallas_call(
        kern,
        grid=(batch, n_lat + 1),
        in_specs=[
            pl.BlockSpec((2 * n_heads, 8, LANES), lambda b, s: (0, 0, 0)),
            pl.BlockSpec((c, n), lambda b, s: (fwd_blk(b, s), 0)),
            pl.BlockSpec((c, n), lambda b, s: (bwd_blk(b, s), 0)),
        ],
        out_specs=[
            pl.BlockSpec((c, n_heads * dv), lambda b, s: (fwd_blk(b, s), 0)),
            pl.BlockSpec((c, n_heads * dv), lambda b, s: (bwd_blk(b, s), 0)),
        ],
        out_shape=[out, out],
        scratch_shapes=[
            pltpu.VMEM((2 * n_heads, dk, dv), F32),
            pltpu.VMEM((2 * n_heads, c, c), F32),
        ],
        compiler_params=_params("parallel", "arbitrary"),
        name="retention_scan",
    )(decay_tile, qkv, qkv)


def _ret_gate_kernel(x_ref, wf_ref, wb_ref, of_ref, ob_ref, y_ref):
    x = x_ref[...]
    gf = jnp.dot(x, wf_ref[...], preferred_element_type=F32)
    gb = jnp.dot(x, wb_ref[...], preferred_element_type=F32)
    y = _silu(gf) * of_ref[...].astype(F32) + _silu(gb) * ob_ref[...].astype(F32)
    y_ref[...] = y.astype(BF16)


def _ret_gate(u, w_g, layer, o_f, o_b, tm):
    t, d = u.shape
    n = w_g.shape[2] // 2
    tn = 512
    nb = n // tn
    return pl.pallas_call(
        _ret_gate_kernel,
        grid=(t // tm, nb),
        in_specs=[
            pl.BlockSpec((tm, d), lambda i, j: (i, 0)),
            pl.BlockSpec((None, d, tn), lambda i, j: (layer, 0, j)),
            pl.BlockSpec((None, d, tn), lambda i, j: (layer, 0, nb + j)),
            pl.BlockSpec((tm, tn), lambda i, j: (i, j)),
            pl.BlockSpec((tm, tn), lambda i, j: (i, j)),
        ],
        out_specs=pl.BlockSpec((tm, tn), lambda i, j: (i, j)),
        out_shape=jax.ShapeDtypeStruct((t, n), BF16),
        compiler_params=_params("parallel", "arbitrary"),
        name="ret_gate",
    )(u, w_g, w_g, o_f, o_b)


def _out_ln_kernel(x_ref, w_ref, h_ref, gate_ref, lng_ref, lnb_ref, sc_ref, sh_ref,
                   hout_ref, u_ref, *acc, alpha, n_k, n_split):
    k = pl.program_id(1)
    rows_per = x_ref.shape[0] // n_split

    def partial_product(rows):
        return jnp.dot(x_ref[rows, :], w_ref[...], preferred_element_type=F32)

    def finish(rows, y):
        z = alpha * h_ref[rows, :] + gate_ref[0] * y
        mu = jnp.mean(z, axis=-1, keepdims=True)
        ctr = z - mu
        var = jnp.mean(ctr * ctr, axis=-1, keepdims=True)
        hn = ctr * lax.rsqrt(var + LN_EPS) * lng_ref[0] + lnb_ref[0]
        hout_ref[rows, :] = hn
        u_ref[rows, :] = (hn * (1.0 + sc_ref[0]) + sh_ref[0]).astype(BF16)

    def last_step(prev):
        for r in range(n_split):
            rows = pl.ds(r * rows_per, rows_per)
            y = partial_product(rows)
            finish(rows, y if prev is None else prev[rows, :] + y)

    if n_k == 1:
        last_step(None)
        return
    acc_ref, = acc

    @pl.when(k == 0)
    def _():
        acc_ref[...] = partial_product(slice(None))

    @pl.when(jnp.logical_and(k > 0, k < n_k - 1))
    def _():
        acc_ref[...] += partial_product(slice(None))

    @pl.when(k == n_k - 1)
    def _():
        last_step(acc_ref)


def _out_ln(x, w, layer, h, mod, ln_g, ln_b, *, gate_idx, ln_idx, sc_idx, sh_idx, alpha, n_rows, tm, tk,
            n_split=2):
    kdim = x.shape[1]
    d = w.shape[2]
    vec = lambda arr_idx: pl.BlockSpec((1, 1, d), lambda i, k: (arr_idx(i), 0, 0))
    n_k = kdim // tk
    kern = functools.partial(_out_ln_kernel, alpha=alpha, n_k=n_k, n_split=n_split)
    return pl.pallas_call(
        kern,
        grid=(n_rows // tm, n_k),
        in_specs=[
            pl.BlockSpec((tm, tk), lambda i, k: (i, k)),
            pl.BlockSpec((None, tk, d), lambda i, k: (layer, k, 0)),
            pl.BlockSpec((tm, d), lambda i, k: (i, 0)),
            vec(gate_idx),
            vec(lambda i: ln_idx),
            vec(lambda i: ln_idx),
            vec(sc_idx),
            vec(sh_idx),
        ],
        out_specs=[
            pl.BlockSpec((tm, d), lambda i, k: (i, 0)),
            pl.BlockSpec((tm, d), lambda i, k: (i, 0)),
        ],
        out_shape=[
            jax.ShapeDtypeStruct((n_rows, d), F32),
            jax.ShapeDtypeStruct((n_rows, d), BF16),
        ],
        scratch_shapes=[pltpu.VMEM((tm, d), F32)] if n_k > 1 else [],
        compiler_params=_params("parallel", "arbitrary"),
        name="out_ln",
    )(x, w, h, mod, ln_g, ln_b, mod, mod)


def _ffn_in_kernel(x_ref, wa_ref, wb_ref, o_ref):
    x = x_ref[...]
    a = jnp.dot(x, wa_ref[...], preferred_element_type=F32)
    b = jnp.dot(x, wb_ref[...], preferred_element_type=F32)
    o_ref[...] = (_silu(a) * b).astype(BF16)


def _ffn_in(u, w_in, layer, n_rows, tm):
    d = u.shape[1]
    f = w_in.shape[2] // 2
    tn = 512
    nb = f // tn
    return pl.pallas_call(
        _ffn_in_kernel,
        grid=(n_rows // tm, nb),
        in_specs=[
            pl.BlockSpec((tm, d), lambda i, j: (i, 0)),
            pl.BlockSpec((None, d, tn), lambda i, j: (layer, 0, j)),
            pl.BlockSpec((None, d, tn), lambda i, j: (layer, 0, nb + j)),
        ],
        out_specs=pl.BlockSpec((tm, tn), lambda i, j: (i, j)),
        out_shape=jax.ShapeDtypeStruct((n_rows, f), BF16),
        compiler_params=_params("parallel", "arbitrary"),
        name="ffn_in",
    )(u, w_in, w_in)


def _rope_lanes(x, c_ref, s_up_ref, s_dn_ref):
    half = MLA_D_ROPE // 4
    up = pltpu.roll(x, LANES - half, 1)
    dn = pltpu.roll(x, half, 1)
    return x * c_ref[...] + up * s_up_ref[...] + dn * s_dn_ref[...]


def _rms(x, g):
    return x * lax.rsqrt(jnp.mean(x * x, axis=-1, keepdims=True) + RMS_EPS) * g


def _mla_down_kernel(x_ref, w_ref, gq_ref, gkv_ref, c_ref, su_ref, sd_ref,
                     cq_ref, ckv_ref, kr_ref, *, rq, rkv):
    acc = jnp.dot(x_ref[...], w_ref[...], preferred_element_type=F32)
    cq_ref[...] = _rms(acc[:, :rq], gq_ref[...]).astype(BF16)
    ckv_ref[...] = _rms(acc[:, rq:rq + rkv], gkv_ref[...]).astype(BF16)
    kr_ref[...] = _rope_lanes(acc[:, rq + rkv:], c_ref, su_ref, sd_ref).astype(BF16)


def _mla_down(u, w_down, g_q, g_kv, tabs, tm):
    t, d = u.shape
    rq, rkv = g_q.shape[1], g_kv.shape[1]
    n = w_down.shape[1]
    tab = pl.BlockSpec((tm, LANES), lambda i: (i, 0))
    kern = functools.partial(_mla_down_kernel, rq=rq, rkv=rkv)
    return pl.pallas_call(
        kern,
        grid=(t // tm,),
        in_specs=[
            pl.BlockSpec((tm, d), lambda i: (i, 0)),
            pl.BlockSpec((d, n), lambda i: (0, 0)),
            pl.BlockSpec((1, rq), lambda i: (0, 0)),
            pl.BlockSpec((1, rkv), lambda i: (0, 0)),
            tab, tab, tab,
        ],
        out_specs=[
            pl.BlockSpec((tm, rq), lambda i: (i, 0)),
            pl.BlockSpec((tm, rkv), lambda i: (i, 0)),
            pl.BlockSpec((tm, LANES), lambda i: (i, 0)),
        ],
        out_shape=[
            jax.ShapeDtypeStruct((t, rq), BF16),
            jax.ShapeDtypeStruct((t, rkv), BF16),
            jax.ShapeDtypeStruct((t, LANES), BF16),
        ],
        compiler_params=_params("parallel"),
        name="mla_down",
    )(u, w_down, g_q, g_kv, *tabs)


def _mla_up_kernel(cq_ref, ckv_ref, kr_ref, wq_ref, wkv_ref, c_ref, su_ref, sd_ref,
                   q_ref, k_ref, v_ref):
    cq = cq_ref[...]
    ckv = ckv_ref[...]
    kr = kr_ref[...]
    hp = MLA_HEAD_PAD
    for h in range(MLA_HEADS):
        qh = jnp.dot(cq, wq_ref[:, h * hp:(h + 1) * hp], preferred_element_type=F32) * MLA_Q_SCALE
        q_ref[:, h * hp:h * hp + LANES] = qh[:, :LANES].astype(BF16)
        q_ref[:, h * hp + LANES:(h + 1) * hp] = _rope_lanes(qh[:, LANES:], c_ref, su_ref, sd_ref).astype(BF16)
        kvh = jnp.dot(ckv, wkv_ref[:, h * hp:(h + 1) * hp], preferred_element_type=F32)
        k_ref[:, h * hp:h * hp + LANES] = kvh[:, :LANES].astype(BF16)
        k_ref[:, h * hp + LANES:(h + 1) * hp] = kr
        v_ref[:, h * MLA_D_V:(h + 1) * MLA_D_V] = kvh[:, LANES:].astype(BF16)


def _mla_up(cq, ckv, kr, wq_pad, wkv, tabs, tm):
    t, rq = cq.shape
    rkv = ckv.shape[1]
    nq = wq_pad.shape[1]
    nkv = wkv.shape[1]
    tab = pl.BlockSpec((tm, LANES), lambda i: (i, 0))
    return pl.pallas_call(
        _mla_up_kernel,
        grid=(t // tm,),
        in_specs=[
            pl.BlockSpec((tm, rq), lambda i: (i, 0)),
            pl.BlockSpec((tm, rkv), lambda i: (i, 0)),
            pl.BlockSpec((tm, LANES), lambda i: (i, 0)),
            pl.BlockSpec((rq, nq), lambda i: (0, 0)),
            pl.BlockSpec((rkv, nkv), lambda i: (0, 0)),
            tab, tab, tab,
        ],
        out_specs=[
            pl.BlockSpec((tm, nq), lambda i: (i, 0)),
            pl.BlockSpec((tm, nq), lambda i: (i, 0)),
            pl.BlockSpec((tm, MLA_HEADS * MLA_D_V), lambda i: (i, 0)),
        ],
        out_shape=[
            jax.ShapeDtypeStruct((t, nq), BF16),
            jax.ShapeDtypeStruct((t, nq), BF16),
            jax.ShapeDtypeStruct((t, MLA_HEADS * MLA_D_V), BF16),
        ],
        compiler_params=_params("parallel"),
        name="mla_up",
    )(cq, ckv, kr, wq_pad, wkv, *tabs)


def _scores(q, k):
    return lax.dot_general(q, k, (((1,), (1,)), ((), ())), preferred_element_type=F32)


def _attn_lat_kernel(q_ref, kl_ref, kc_ref, vl_ref, vc_ref, o_ref, *, key_chunk):
    q = q_ref[...]
    n_lat = kl_ref.shape[0] // key_chunk
    chunks = [(kl_ref, vl_ref, c * key_chunk, key_chunk) for c in range(n_lat)]
    chunks.append((kc_ref, vc_ref, 0, kc_ref.shape[0]))

    def chunk_scores(c):
        k_ref, _, lo, size = chunks[c]
        return _scores(q, k_ref[lo:lo + size, :])

    s = chunk_scores(0)
    m = l = acc = None
    for c, (_, v_ref, lo, size) in enumerate(chunks):
        s_next = chunk_scores(c + 1) if c + 1 < len(chunks) else None
        m_c = jnp.max(s, axis=-1, keepdims=True)
        v = v_ref[lo:lo + size, :]
        if c == 0:
            m = m_c
            p = jnp.exp2(s - m)
            l = jnp.sum(p, axis=-1, keepdims=True)
            acc = jnp.dot(p.astype(BF16), v, preferred_element_type=F32)
        else:
            m_new = jnp.maximum(m, m_c)
            alpha = jnp.exp2(m - m_new)
            p = jnp.exp2(s - m_new)
            l = alpha * l + jnp.sum(p, axis=-1, keepdims=True)
            acc = alpha * acc + jnp.dot(p.astype(BF16), v, preferred_element_type=F32)
            m = m_new
        s = s_next
    o_ref[...] = (acc / l).astype(BF16)


def _attn_ctx_kernel(q_ref, kc_ref, vc_ref, aliased_ref, o_ref):
    del aliased_ref
    s_c = _scores(q_ref[...], kc_ref[...])
    p_c = jnp.exp2(s_c - jnp.max(s_c, axis=-1, keepdims=True))
    denom = jnp.sum(p_c, axis=-1, keepdims=True)
    o = jnp.dot(p_c.astype(BF16), vc_ref[...], preferred_element_type=F32)
    o_ref[...] = (o / denom).astype(BF16)


def _attention(q, k, v, batch, seq, ctx_len, with_ctx_queries):
    t = q.shape[0]
    hp, dv = MLA_HEAD_PAD, MLA_D_V
    tq = 512
    nq = seq // tq
    ctx0 = batch * seq // ctx_len
    n_rows = t if with_ctx_queries else batch * seq
    o_lat = pl.pallas_call(
        functools.partial(_attn_lat_kernel, key_chunk=min(seq, 512)),
        grid=(batch, MLA_HEADS, nq),
        in_specs=[
            pl.BlockSpec((tq, hp), lambda b, h, i: (b * nq + i, h)),
            pl.BlockSpec((seq, hp), lambda b, h, i: (b, h)),
            pl.BlockSpec((ctx_len, hp), lambda b, h, i: (ctx0 + b, h)),
            pl.BlockSpec((seq, dv), lambda b, h, i: (b, h)),
            pl.BlockSpec((ctx_len, dv), lambda b, h, i: (ctx0 + b, h)),
        ],
        out_specs=pl.BlockSpec((tq, dv), lambda b, h, i: (b * nq + i, h)),
        out_shape=jax.ShapeDtypeStruct((n_rows, MLA_HEADS * dv), BF16),
        compiler_params=_params("parallel", "parallel", "arbitrary"),
        name="attn_lat",
    )(q, k, k, v, v)
    if not with_ctx_queries:
        return o_lat
    return pl.pallas_call(
        _attn_ctx_kernel,
        grid=(batch, MLA_HEADS),
        in_specs=[
            pl.BlockSpec((ctx_len, hp), lambda b, h: (ctx0 + b, h)),
            pl.BlockSpec((ctx_len, hp), lambda b, h: (ctx0 + b, h)),
            pl.BlockSpec((ctx_len, dv), lambda b, h: (ctx0 + b, h)),
            pl.BlockSpec(memory_space=pl.ANY),
        ],
        out_specs=pl.BlockSpec((ctx_len, dv), lambda b, h: (ctx0 + b, h)),
        out_shape=jax.ShapeDtypeStruct((n_rows, MLA_HEADS * dv), BF16),
        input_output_aliases={3: 0},
        compiler_params=_params("parallel", "parallel"),
        name="attn_ctx",
    )(q, k, v, o_lat)


def _retention_tables(seq, n_ctx, dk):
    inv = RET_ROPE_BASE ** (-jnp.linspace(0.0, 1.0, dk // 2, dtype=F32))
    ang = jnp.arange(seq, dtype=F32)[:, None] * inv[None, :]
    return jnp.cos(ang), jnp.sin(ang), jnp.ones((n_ctx, dk // 2), F32), jnp.zeros((n_ctx, dk // 2), F32)


def _axial_tables(seq):
    n_f = MLA_D_ROPE // 4
    inv = AXIAL_ROPE_BASE ** (-jnp.arange(n_f, dtype=F32) * 2.0 / (MLA_D_ROPE // 2))
    tpos = jnp.arange(seq)
    row = (tpos // GRID_W).astype(F32)[:, None] * inv[None, :]
    col = (tpos % GRID_W).astype(F32)[:, None] * inv[None, :]
    z = jnp.zeros((seq, n_f), F32)
    tail = LANES - MLA_D_ROPE
    cos = jnp.concatenate([jnp.cos(row), jnp.cos(row), jnp.cos(col), jnp.cos(col), jnp.ones((seq, tail), F32)], 1)
    s_up = jnp.concatenate([-jnp.sin(row), z, -jnp.sin(col), z, jnp.zeros((seq, tail), F32)], 1)
    s_dn = jnp.concatenate([z, jnp.sin(row), z, jnp.sin(col), jnp.zeros((seq, tail), F32)], 1)
    return cos, s_up, s_dn


def _token_table(lat, batch, ctx_rows, fill):
    return jnp.concatenate([jnp.tile(lat, (batch, 1)), jnp.full((ctx_rows, lat.shape[1]), fill, F32)], 0)


def kernel(x, c, ctx, c_ctx, ada_w, ada_b, ln_g, ln_b, ret_w_qkv, ret_w_g, ret_decay_logit, ret_w_o,
           mla_w_dq, mla_g_q, mla_w_uq, mla_w_dkv, mla_g_kv, mla_w_ukv, mla_w_o, ffn_w_in, ffn_w_out):
    batch, seq, d = x.shape
    ctx_len = ctx.shape[1]
    depth = ada_w.shape[0]
    n_lat, n_ctx = batch * seq, batch * ctx_len
    n_tok = n_lat + n_ctx
    ret_heads = ret_decay_logit.shape[-1]
    assert ctx_len == RET_CHUNK and seq % RET_CHUNK == 0 and batch + 1 <= 8
    tm = _row_tile(n_lat, n_ctx)
    alpha = (2 * depth) ** 0.25

    cond8 = jnp.concatenate([c, c_ctx[None, :], jnp.zeros((8 - batch - 1, d), F32)], 0)
    mod = _modulation(cond8, ada_w, ada_b).reshape(depth * 8 * 6, 1, d)

    def mod_idx(layer, chunk, tile):
        def f(i):
            r = jnp.where(i * tile >= n_lat, batch, (i * tile) // seq)
            return (layer * 8 + r) * 6 + chunk
        return f

    ln_g3 = ln_g.reshape(depth * 2, 1, d)
    ln_b3 = ln_b.reshape(depth * 2, 1, d)

    ret_cos, ret_sin, one_c, zero_c = _retention_tables(seq, n_ctx, d // ret_heads)
    ret_cos = jnp.concatenate([jnp.tile(ret_cos, (batch, 1)), one_c], 0)
    ret_sin = jnp.concatenate([jnp.tile(ret_sin, (batch, 1)), zero_c], 0)
    ax_cos, ax_up, ax_dn = _axial_tables(seq)
    ax_tabs = (_token_table(ax_cos, batch, n_ctx, 1.0), _token_table(ax_up, batch, n_ctx, 0.0),
               _token_table(ax_dn, batch, n_ctx, 0.0))

    h = jnp.concatenate([x.reshape(n_lat, d), ctx.reshape(n_ctx, d)], 0)
    u = _modulate(h, mod, mod_idx(0, 1, tm), mod_idx(0, 0, tm), tm)
    tm_ln = 512

    ret_w_qkv_b, ret_w_g_b, ret_w_o_b = (w.astype(BF16) for w in (ret_w_qkv, ret_w_g, ret_w_o))
    mla_w_o_b, ffn_w_in_b, ffn_w_out_b = (w.astype(BF16) for w in (mla_w_o, ffn_w_in, ffn_w_out))

    for i in range(depth):
        last = i == depth - 1
        n_rows = n_lat if last else n_tok
        j = i // 2
        if i % 2 == 0:
            qkv = _ret_qkv(u, ret_w_qkv_b, j, ret_cos, ret_sin, ret_heads, tm)
            decay_tile = jnp.broadcast_to(
                ret_decay_logit[j].astype(F32).reshape(2 * ret_heads, 1, 1), (2 * ret_heads, 8, LANES))
            o_f, o_b = _retention(qkv, decay_tile, batch, seq, ret_heads)
            y = _ret_gate(u, ret_w_g_b, j, o_f, o_b, tm)
            w_o = ret_w_o_b
        else:
            rq, rkv = mla_g_q.shape[1], mla_g_kv.shape[1]
            w_down = jnp.concatenate(
                [mla_w_dq[j], mla_w_dkv[j], jnp.zeros((d, LANES - MLA_D_ROPE), F32)], 1).astype(BF16)
            cq, ckv, kr = _mla_down(u, w_down, mla_g_q[j][None, :], mla_g_kv[j][None, :], ax_tabs, 512)
            wq_pad = jnp.pad(
                mla_w_uq[j].reshape(rq, MLA_HEADS, MLA_D_NOPE + MLA_D_ROPE),
                ((0, 0), (0, 0), (0, MLA_HEAD_PAD - MLA_D_NOPE - MLA_D_ROPE)),
            ).reshape(rq, MLA_HEADS * MLA_HEAD_PAD).astype(BF16)
            q, k, v = _mla_up(cq, ckv, kr, wq_pad, mla_w_ukv[j].astype(BF16), ax_tabs, 512)
            y = _attention(q, k, v, batch, seq, ctx_len, not last)
            w_o = mla_w_o_b

        h, u = _out_ln(y, w_o, j, h, mod, ln_g3, ln_b3, gate_idx=mod_idx(i, 2, tm_ln), ln_idx=2 * i,
                       sc_idx=mod_idx(i, 4, tm_ln), sh_idx=mod_idx(i, 3, tm_ln), alpha=alpha,
                       n_rows=n_rows, tm=tm_ln, tk=2048)
        hid = _ffn_in(u, ffn_w_in_b, i, n_rows, tm)
        nxt = min(i + 1, depth - 1)
        h, u = _out_ln(hid, ffn_w_out_b, i, h, mod, ln_g3, ln_b3, gate_idx=mod_idx(i, 5, tm_ln),
                       ln_idx=2 * i + 1, sc_idx=mod_idx(nxt, 1, tm_ln), sh_idx=mod_idx(nxt, 0, tm_ln),
                       alpha=alpha, n_rows=n_rows, tm=tm_ln, tk=ffn_w_out.shape[1] // 4)
    return h.reshape(batch, seq, d)
```

```python
import functools

import jax
import jax.numpy as jnp
from jax import lax
from jax.experimental import pallas as pl
from jax.experimental.pallas import tpu as pltpu

F32 = jnp.float32
BF16 = jnp.bfloat16

GRID_W = 64
RET_ROPE_BASE = 10000.0
GN_EPS = 1e-6
MLA_HEADS = 16
MLA_D_NOPE = 128
MLA_D_ROPE = 64
MLA_D_V = 128
MLA_SCALE = (MLA_D_NOPE + MLA_D_ROPE) ** -0.5
MLA_Q_SCALE = MLA_SCALE * 1.4426950408889634
AXIAL_ROPE_BASE = 10000.0
RMS_EPS = 1e-6
LN_EPS = 1e-5

LANES = 128
MXU_DIM = 256
VMEM_LIMIT_BYTES = 56 * 1024 * 1024

RET_CHUNK = 256
MLA_HEAD_PAD = 2 * LANES


def _params(*sem):
    return pltpu.CompilerParams(dimension_semantics=sem, vmem_limit_bytes=VMEM_LIMIT_BYTES)


def _silu(x):
    return x / (1.0 + jnp.exp(-x))


def _row_tile(n_lat, n_ctx):
    for t in (1024, 512, 256):
        if n_lat % t == 0 and n_ctx % t == 0:
            return t
    raise ValueError("token counts must be multiples of 256")


def _mod_kernel(cond_ref, w_ref, b_ref, o_ref):
    s = _silu(cond_ref[...]).astype(BF16)
    acc = jnp.dot(s, w_ref[0].astype(BF16), preferred_element_type=F32)
    o_ref[0] = acc + b_ref[0]


def _modulation(cond8, ada_w, ada_b):
    depth, d, n = ada_w.shape
    tn = 1024
    return pl.pallas_call(
        _mod_kernel,
        grid=(depth, n // tn),
        in_specs=[
            pl.BlockSpec((8, d), lambda l, j: (0, 0)),
            pl.BlockSpec((1, d, tn), lambda l, j: (l, 0, j)),
            pl.BlockSpec((1, 1, tn), lambda l, j: (l, 0, j)),
        ],
        out_specs=pl.BlockSpec((1, 8, tn), lambda l, j: (l, 0, j)),
        out_shape=jax.ShapeDtypeStruct((depth, 8, n), F32),
        compiler_params=_params("parallel", "parallel"),
        name="adaln_mod",
    )(cond8, ada_w, ada_b.reshape(depth, 1, n))


def _modulate_kernel(x_ref, ctx_ref, sc_ref, sh_ref, h_ref, u_ref, *, n_lat_tiles):
    def emit(src_ref):
        hv = src_ref[...]
        h_ref[...] = hv
        u_ref[...] = (hv * (1.0 + sc_ref[0]) + sh_ref[0]).astype(BF16)

    is_lat = pl.program_id(0) < n_lat_tiles
    pl.when(is_lat)(lambda: emit(x_ref))
    pl.when(jnp.logical_not(is_lat))(lambda: emit(ctx_ref))


def _modulate(x2d, ctx2d, mod, sc_idx, sh_idx, tm):
    n_lat, d = x2d.shape
    n_ctx = ctx2d.shape[0]
    t = n_lat + n_ctx
    n_lat_tiles = n_lat // tm
    vec = lambda f: pl.BlockSpec((1, 1, d), lambda i: (f(i), 0, 0))
    row = pl.BlockSpec((tm, d), lambda i: (i, 0))
    return pl.pallas_call(
        functools.partial(_modulate_kernel, n_lat_tiles=n_lat_tiles),
        grid=(t // tm,),
        in_specs=[
            pl.BlockSpec((tm, d), lambda i: (jnp.minimum(i, n_lat_tiles - 1), 0)),
            pl.BlockSpec((tm, d), lambda i: (jnp.maximum(i - n_lat_tiles, 0), 0)),
            vec(sc_idx), vec(sh_idx),
        ],
        out_specs=[row, row],
        out_shape=[jax.ShapeDtypeStruct((t, d), F32), jax.ShapeDtypeStruct((t, d), BF16)],
        compiler_params=_params("arbitrary"),
        name="modulate_in",
    )(x2d, ctx2d, mod, mod)


def _cast_weights_once(pairs):
    @pl.when(pl.program_id(1) == 0)
    def _():
        for src_ref, dst_ref in pairs:
            dst_ref[...] = src_ref[...].astype(BF16)


def _row_groups(ref, n_split):
    rows_per = ref.shape[0] // n_split
    return [pl.ds(r * rows_per, rows_per) for r in range(n_split)]


def _ret_qkv_kernel(x_ref, w_ref, cos_ref, sin_ref, o_ref, wb_ref, *, n_q, n_k, k_scale, n_split):
    j = pl.program_id(0)
    tn = o_ref.shape[1]
    _cast_weights_once([(w_ref, wb_ref)])

    def product(rows):
        return jnp.dot(x_ref[rows, :], wb_ref[...], preferred_element_type=F32)

    @pl.when(j < n_q + n_k)
    def _():
        scale = jnp.where(j < n_q, 1.0, k_scale).astype(F32)
        for rows in _row_groups(o_ref, n_split):
            acc = product(rows)
            c = cos_ref[rows, :]
            s = sin_ref[rows, :]
            for g in range(tn // MXU_DIM):
                lo = g * MXU_DIM
                x1 = acc[:, lo:lo + LANES] * scale
                x2 = acc[:, lo + LANES:lo + MXU_DIM] * scale
                o_ref[rows, lo:lo + LANES] = (x1 * c - x2 * s).astype(BF16)
                o_ref[rows, lo + LANES:lo + MXU_DIM] = (x1 * s + x2 * c).astype(BF16)

    @pl.when(j >= n_q + n_k)
    def _():
        for rows in _row_groups(o_ref, n_split):
            o_ref[rows, :] = product(rows).astype(BF16)


def _ret_qkv(u, w, layer, cos, sin, n_heads, tm):
    t, d = u.shape
    n = w.shape[2]
    dk = d // n_heads
    tn = 1024
    n_q = n_heads * dk // tn
    kern = functools.partial(_ret_qkv_kernel, n_q=n_q, n_k=n_q, k_scale=dk ** -0.5, n_split=2)
    return pl.pallas_call(
        kern,
        grid=(n // tn, t // tm),
        in_specs=[
            pl.BlockSpec((tm, d), lambda j, i: (i, 0)),
            pl.BlockSpec((None, d, tn), lambda j, i: (layer, 0, j)),
            pl.BlockSpec((tm, LANES), lambda j, i: (i, 0)),
            pl.BlockSpec((tm, LANES), lambda j, i: (i, 0)),
        ],
        out_specs=pl.BlockSpec((tm, tn), lambda j, i: (i, j)),
        out_shape=jax.ShapeDtypeStruct((t, n), BF16),
        scratch_shapes=[pltpu.VMEM((d, tn), BF16)],
        compiler_params=_params("parallel", "arbitrary"),
        name="ret_qkv",
    )(u, w, cos, sin)


def _retention_kernel(dec_ref, qf_ref, qb_ref, of_ref, ob_ref, s_ref, intra_ref, *, n_heads, dk, dv):
    c = RET_CHUNK
    step = pl.program_id(1)
    row = lax.broadcasted_iota(jnp.int32, (c, c), 0)
    col = lax.broadcasted_iota(jnp.int32, (c, c), 1)
    pos = lax.broadcasted_iota(jnp.int32, (c, 1), 0).astype(F32)

    def log_gamma(idx):
        x = dec_ref[idx][0:1, 0:1]
        return jnp.minimum(x, 0.0) - jnp.log1p(jnp.exp(-jnp.abs(x)))

    @pl.when(step == 0)
    def _():
        s_ref[...] = jnp.zeros_like(s_ref)
        for d in range(2):
            diff = (row - col) if d == 0 else (col - row)
            dist = jnp.maximum(diff, 0).astype(F32)
            for h in range(n_heads):
                idx = d * n_heads + h
                intra_ref[idx] = jnp.where(diff >= 0, jnp.exp(log_gamma(idx) * dist), 0.0)

    for d, (x_ref, o_ref) in enumerate(((qf_ref, of_ref), (qb_ref, ob_ref))):
        ahead = pos + 1.0 if d == 0 else float(c) - pos
        behind = float(c - 1) - pos if d == 0 else pos
        for h in range(n_heads):
            idx = d * n_heads + h
            lg = log_gamma(idx)
            q = x_ref[:, h * dk:(h + 1) * dk]
            k = x_ref[:, n_heads * dk + h * dk:n_heads * dk + (h + 1) * dk]
            v = x_ref[:, 2 * n_heads * dk + h * dv:2 * n_heads * dk + (h + 1) * dv]
            scores = lax.dot_general(q, k, (((1,), (1,)), ((), ())), preferred_element_type=F32)
            p = (scores * intra_ref[idx]).astype(BF16)
            qd = (q.astype(F32) * jnp.exp(lg * ahead)).astype(BF16)
            state = s_ref[idx]
            o = (jnp.dot(p, v, preferred_element_type=F32)
                 + jnp.dot(qd, state.astype(BF16), preferred_element_type=F32))
            kd = (k.astype(F32) * jnp.exp(lg * behind)).astype(BF16)
            s_ref[idx] = state * jnp.exp(lg * float(c)) + lax.dot_general(
                kd, v, (((0,), (0,)), ((), ())), preferred_element_type=F32)
            mu = jnp.mean(o, axis=-1, keepdims=True)
            ctr = o - mu
            var = jnp.mean(ctr * ctr, axis=-1, keepdims=True)
            o_ref[:, h * dv:(h + 1) * dv] = (ctr * lax.rsqrt(var + GN_EPS)).astype(BF16)


def _retention(qkv, decay_tile, batch, seq, n_heads):
    t, n = qkv.shape
    d = n // 4
    dk, dv = d // n_heads, 2 * d // n_heads
    c = RET_CHUNK
    n_lat = seq // c
    ctx_blk = batch * n_lat

    def fwd_blk(b, s):
        return jnp.where(s == 0, ctx_blk + b, b * n_lat + s - 1)

    def bwd_blk(b, s):
        return jnp.where(s == 0, ctx_blk + b, b * n_lat + n_lat - s)

    kern = functools.partial(_retention_kernel, n_heads=n_heads, dk=dk, dv=dv)
    out = jax.ShapeDtypeStruct((t, n_heads * dv), BF16)
    return pl.pallas_call(
        kern,
        grid=(batch, n_lat + 1),
        in_specs=[
            pl.BlockSpec((2 * n_heads, 8, LANES), lambda b, s: (0, 0, 0)),
            pl.BlockSpec((c, n), lambda b, s: (fwd_blk(b, s), 0)),
            pl.BlockSpec((c, n), lambda b, s: (bwd_blk(b, s), 0)),
        ],
        out_specs=[
            pl.BlockSpec((c, n_heads * dv), lambda b, s: (fwd_blk(b, s), 0)),
            pl.BlockSpec((c, n_heads * dv), lambda b, s: (bwd_blk(b, s), 0)),
        ],
        out_shape=[out, out],
        scratch_shapes=[
            pltpu.VMEM((2 * n_heads, dk, dv), F32),
            pltpu.VMEM((2 * n_heads, c, c), F32),
        ],
        compiler_params=_params("parallel", "arbitrary"),
        name="retention_scan",
    )(decay_tile, qkv, qkv)


def _ret_gate_kernel(x_ref, wf_ref, wb_ref, of_ref, ob_ref, y_ref, wf_bf, wb_bf, *, n_split):
    _cast_weights_once([(wf_ref, wf_bf), (wb_ref, wb_bf)])
    for rows in _row_groups(y_ref, n_split):
        x = x_ref[rows, :]
        gf = jnp.dot(x, wf_bf[...], preferred_element_type=F32)
        gb = jnp.dot(x, wb_bf[...], preferred_element_type=F32)
        y = _silu(gf) * of_ref[rows, :].astype(F32) + _silu(gb) * ob_ref[rows, :].astype(F32)
        y_ref[rows, :] = y.astype(BF16)


def _ret_gate(u, w_g, layer, o_f, o_b, tm):
    t, d = u.shape
    n = w_g.shape[2] // 2
    tn = 512
    nb = n // tn
    return pl.pallas_call(
        functools.partial(_ret_gate_kernel, n_split=2),
        grid=(nb, t // tm),
        in_specs=[
            pl.BlockSpec((tm, d), lambda j, i: (i, 0)),
            pl.BlockSpec((None, d, tn), lambda j, i: (layer, 0, j)),
            pl.BlockSpec((None, d, tn), lambda j, i: (layer, 0, nb + j)),
            pl.BlockSpec((tm, tn), lambda j, i: (i, j)),
            pl.BlockSpec((tm, tn), lambda j, i: (i, j)),
        ],
        out_specs=pl.BlockSpec((tm, tn), lambda j, i: (i, j)),
        out_shape=jax.ShapeDtypeStruct((t, n), BF16),
        scratch_shapes=[pltpu.VMEM((d, tn), BF16), pltpu.VMEM((d, tn), BF16)],
        compiler_params=_params("parallel", "arbitrary"),
        name="ret_gate",
    )(u, w_g, w_g, o_f, o_b)


def _out_ln_kernel(x_ref, w_ref, h_ref, gate_ref, lng_ref, lnb_ref, sc_ref, sh_ref,
                   hout_ref, u_ref, *acc, alpha, n_k, n_split):
    k = pl.program_id(1)
    rows_per = x_ref.shape[0] // n_split

    def partial_product(rows):
        return jnp.dot(x_ref[rows, :], w_ref[...], preferred_element_type=F32)

    def finish(rows, y):
        z = alpha * h_ref[rows, :] + gate_ref[0] * y
        mu = jnp.mean(z, axis=-1, keepdims=True)
        ctr = z - mu
        var = jnp.mean(ctr * ctr, axis=-1, keepdims=True)
        hn = ctr * lax.rsqrt(var + LN_EPS) * lng_ref[0] + lnb_ref[0]
        hout_ref[rows, :] = hn
        u_ref[rows, :] = (hn * (1.0 + sc_ref[0]) + sh_ref[0]).astype(BF16)

    def last_step(prev):
        for r in range(n_split):
            rows = pl.ds(r * rows_per, rows_per)
            y = partial_product(rows)
            finish(rows, y if prev is None else prev[rows, :] + y)

    if n_k == 1:
        last_step(None)
        return
    acc_ref, = acc

    @pl.when(k == 0)
    def _():
        acc_ref[...] = partial_product(slice(None))

    @pl.when(jnp.logical_and(k > 0, k < n_k - 1))
    def _():
        acc_ref[...] += partial_product(slice(None))

    @pl.when(k == n_k - 1)
    def _():
        last_step(acc_ref)


def _out_ln(x, w, layer, h, mod, ln_g, ln_b, *, gate_idx, ln_idx, sc_idx, sh_idx, alpha, n_rows, tm, tk,
            n_split=2):
    kdim = x.shape[1]
    d = w.shape[2]
    vec = lambda arr_idx: pl.BlockSpec((1, 1, d), lambda i, k: (arr_idx(i), 0, 0))
    n_k = kdim // tk
    kern = functools.partial(_out_ln_kernel, alpha=alpha, n_k=n_k, n_split=n_split)
    return pl.pallas_call(
        kern,
        grid=(n_rows // tm, n_k),
        in_specs=[
            pl.BlockSpec((tm, tk), lambda i, k: (i, k)),
            pl.BlockSpec((None, tk, d), lambda i, k: (layer, k, 0)),
            pl.BlockSpec((tm, d), lambda i, k: (i, 0)),
            vec(gate_idx),
            vec(lambda i: ln_idx),
            vec(lambda i: ln_idx),
            vec(sc_idx),
            vec(sh_idx),
        ],
        out_specs=[
            pl.BlockSpec((tm, d), lambda i, k: (i, 0)),
            pl.BlockSpec((tm, d), lambda i, k: (i, 0)),
        ],
        out_shape=[
            jax.ShapeDtypeStruct((n_rows, d), F32),
            jax.ShapeDtypeStruct((n_rows, d), BF16),
        ],
        scratch_shapes=[pltpu.VMEM((tm, d), F32)] if n_k > 1 else [],
        compiler_params=_params("parallel", "arbitrary"),
        name="out_ln",
    )(x, w, h, mod, ln_g, ln_b, mod, mod)


def _ffn_in_kernel(x_ref, wa_ref, wb_ref, o_ref, wa_bf, wb_bf, *, n_split):
    _cast_weights_once([(wa_ref, wa_bf), (wb_ref, wb_bf)])
    for rows in _row_groups(o_ref, n_split):
        x = x_ref[rows, :]
        a = jnp.dot(x, wa_bf[...], preferred_element_type=F32)
        b = jnp.dot(x, wb_bf[...], preferred_element_type=F32)
        o_ref[rows, :] = (_silu(a) * b).astype(BF16)


def _ffn_in(u, w_in, layer, n_rows, tm):
    d = u.shape[1]
    f = w_in.shape[2] // 2
    tn = 512
    nb = f // tn
    return pl.pallas_call(
        functools.partial(_ffn_in_kernel, n_split=2),
        grid=(nb, n_rows // tm),
        in_specs=[
            pl.BlockSpec((tm, d), lambda j, i: (i, 0)),
            pl.BlockSpec((None, d, tn), lambda j, i: (layer, 0, j)),
            pl.BlockSpec((None, d, tn), lambda j, i: (layer, 0, nb + j)),
        ],
        out_specs=pl.BlockSpec((tm, tn), lambda j, i: (i, j)),
        out_shape=jax.ShapeDtypeStruct((n_rows, f), BF16),
        scratch_shapes=[pltpu.VMEM((d, tn), BF16), pltpu.VMEM((d, tn), BF16)],
        compiler_params=_params("parallel", "arbitrary"),
        name="ffn_in",
    )(u, w_in, w_in)


def _rope_lanes(x, c_ref, s_up_ref, s_dn_ref):
    half = MLA_D_ROPE // 4
    up = pltpu.roll(x, LANES - half, 1)
    dn = pltpu.roll(x, half, 1)
    return x * c_ref[...] + up * s_up_ref[...] + dn * s_dn_ref[...]


def _rms(x, g):
    return x * lax.rsqrt(jnp.mean(x * x, axis=-1, keepdims=True) + RMS_EPS) * g


def _mla_down_kernel(x_ref, w_ref, gq_ref, gkv_ref, c_ref, su_ref, sd_ref,
                     cq_ref, ckv_ref, kr_ref, *, rq, rkv):
    acc = jnp.dot(x_ref[...], w_ref[...], preferred_element_type=F32)
    cq_ref[...] = _rms(acc[:, :rq], gq_ref[...]).astype(BF16)
    ckv_ref[...] = _rms(acc[:, rq:rq + rkv], gkv_ref[...]).astype(BF16)
    kr_ref[...] = _rope_lanes(acc[:, rq + rkv:], c_ref, su_ref, sd_ref).astype(BF16)


def _mla_down(u, w_down, g_q, g_kv, tabs, tm):
    t, d = u.shape
    rq, rkv = g_q.shape[1], g_kv.shape[1]
    n = w_down.shape[1]
    tab = pl.BlockSpec((tm, LANES), lambda i: (i, 0))
    kern = functools.partial(_mla_down_kernel, rq=rq, rkv=rkv)
    return pl.pallas_call(
        kern,
        grid=(t // tm,),
        in_specs=[
            pl.BlockSpec((tm, d), lambda i: (i, 0)),
            pl.BlockSpec((d, n), lambda i: (0, 0)),
            pl.BlockSpec((1, rq), lambda i: (0, 0)),
            pl.BlockSpec((1, rkv), lambda i: (0, 0)),
            tab, tab, tab,
        ],
        out_specs=[
            pl.BlockSpec((tm, rq), lambda i: (i, 0)),
            pl.BlockSpec((tm, rkv), lambda i: (i, 0)),
            pl.BlockSpec((tm, LANES), lambda i: (i, 0)),
        ],
        out_shape=[
            jax.ShapeDtypeStruct((t, rq), BF16),
            jax.ShapeDtypeStruct((t, rkv), BF16),
            jax.ShapeDtypeStruct((t, LANES), BF16),
        ],
        compiler_params=_params("parallel"),
        name="mla_down",
    )(u, w_down, g_q, g_kv, *tabs)


def _mla_up_kernel(cq_ref, ckv_ref, kr_ref, wq_ref, wkv_ref, c_ref, su_ref, sd_ref,
                   q_ref, k_ref, v_ref):
    cq = cq_ref[...]
    ckv = ckv_ref[...]
    kr = kr_ref[...]
    hp = MLA_HEAD_PAD
    for h in range(MLA_HEADS):
        qh = jnp.dot(cq, wq_ref[:, h * hp:(h + 1) * hp], preferred_element_type=F32) * MLA_Q_SCALE
        q_ref[:, h * hp:h * hp + LANES] = qh[:, :LANES].astype(BF16)
        q_ref[:, h * hp + LANES:(h + 1) * hp] = _rope_lanes(qh[:, LANES:], c_ref, su_ref, sd_ref).astype(BF16)
        kvh = jnp.dot(ckv, wkv_ref[:, h * hp:(h + 1) * hp], preferred_element_type=F32)
        k_ref[:, h * hp:h * hp + LANES] = kvh[:, :LANES].astype(BF16)
        k_ref[:, h * hp + LANES:(h + 1) * hp] = kr
        v_ref[:, h * MLA_D_V:(h + 1) * MLA_D_V] = kvh[:, LANES:].astype(BF16)


def _mla_up(cq, ckv, kr, wq_pad, wkv, tabs, tm):
    t, rq = cq.shape
    rkv = ckv.shape[1]
    nq = wq_pad.shape[1]
    nkv = wkv.shape[1]
    tab = pl.BlockSpec((tm, LANES), lambda i: (i, 0))
    return pl.pallas_call(
        _mla_up_kernel,
        grid=(t // tm,),
        in_specs=[
            pl.BlockSpec((tm, rq), lambda i: (i, 0)),
            pl.BlockSpec((tm, rkv), lambda i: (i, 0)),
            pl.BlockSpec((tm, LANES), lambda i: (i, 0)),
            pl.BlockSpec((rq, nq), lambda i: (0, 0)),
            pl.BlockSpec((rkv, nkv), lambda i: (0, 0)),
            tab, tab, tab,
        ],
        out_specs=[
            pl.BlockSpec((tm, nq), lambda i: (i, 0)),
            pl.BlockSpec((tm, nq), lambda i: (i, 0)),
            pl.BlockSpec((tm, MLA_HEADS * MLA_D_V), lambda i: (i, 0)),
        ],
        out_shape=[
            jax.ShapeDtypeStruct((t, nq), BF16),
            jax.ShapeDtypeStruct((t, nq), BF16),
            jax.ShapeDtypeStruct((t, MLA_HEADS * MLA_D_V), BF16),
        ],
        compiler_params=_params("parallel"),
        name="mla_up",
    )(cq, ckv, kr, wq_pad, wkv, *tabs)


def _scores(q, k):
    return lax.dot_general(q, k, (((1,), (1,)), ((), ())), preferred_element_type=F32)


def _attn_lat_kernel(q_ref, kl_ref, kc_ref, vl_ref, vc_ref, o_ref, *, key_chunk):
    q = q_ref[...]
    n_lat = kl_ref.shape[0] // key_chunk
    chunks = [(kl_ref, vl_ref, c * key_chunk, key_chunk) for c in range(n_lat)]
    chunks.append((kc_ref, vc_ref, 0, kc_ref.shape[0]))

    def chunk_scores(c):
        k_ref, _, lo, size = chunks[c]
        return _scores(q, k_ref[lo:lo + size, :])

    s = chunk_scores(0)
    m = l = acc = None
    for c, (_, v_ref, lo, size) in enumerate(chunks):
        s_next = chunk_scores(c + 1) if c + 1 < len(chunks) else None
        m_c = jnp.max(s, axis=-1, keepdims=True)
        v = v_ref[lo:lo + size, :]
        if c == 0:
            m = m_c
            p = jnp.exp2(s - m)
            l = jnp.sum(p, axis=-1, keepdims=True)
            acc = jnp.dot(p.astype(BF16), v, preferred_element_type=F32)
        else:
            m_new = jnp.maximum(m, m_c)
            alpha = jnp.exp2(m - m_new)
            p = jnp.exp2(s - m_new)
            l = alpha * l + jnp.sum(p, axis=-1, keepdims=True)
            acc = alpha * acc + jnp.dot(p.astype(BF16), v, preferred_element_type=F32)
            m = m_new
        s = s_next
    o_ref[...] = (acc / l).astype(BF16)


def _attn_ctx_kernel(q_ref, kc_ref, vc_ref, aliased_ref, o_ref):
    del aliased_ref
    s_c = _scores(q_ref[...], kc_ref[...])
    p_c = jnp.exp2(s_c - jnp.max(s_c, axis=-1, keepdims=True))
    denom = jnp.sum(p_c, axis=-1, keepdims=True)
    o = jnp.dot(p_c.astype(BF16), vc_ref[...], preferred_element_type=F32)
    o_ref[...] = (o / denom).astype(BF16)


def _attention(q, k, v, batch, seq, ctx_len, with_ctx_queries):
    t = q.shape[0]
    hp, dv = MLA_HEAD_PAD, MLA_D_V
    tq = 512
    nq = seq // tq
    ctx0 = batch * seq // ctx_len
    n_rows = t if with_ctx_queries else batch * seq
    o_lat = pl.pallas_call(
        functools.partial(_attn_lat_kernel, key_chunk=min(seq, 512)),
        grid=(batch, MLA_HEADS, nq),
        in_specs=[
            pl.BlockSpec((tq, hp), lambda b, h, i: (b * nq + i, h)),
            pl.BlockSpec((seq, hp), lambda b, h, i: (b, h)),
            pl.BlockSpec((ctx_len, hp), lambda b, h, i: (ctx0 + b, h)),
            pl.BlockSpec((seq, dv), lambda b, h, i: (b, h)),
            pl.BlockSpec((ctx_len, dv), lambda b, h, i: (ctx0 + b, h)),
        ],
        out_specs=pl.BlockSpec((tq, dv), lambda b, h, i: (b * nq + i, h)),
        out_shape=jax.ShapeDtypeStruct((n_rows, MLA_HEADS * dv), BF16),
        compiler_params=_params("parallel", "parallel", "arbitrary"),
        name="attn_lat",
    )(q, k, k, v, v)
    if not with_ctx_queries:
        return o_lat
    return pl.pallas_call(
        _attn_ctx_kernel,
        grid=(batch, MLA_HEADS),
        in_specs=[
            pl.BlockSpec((ctx_len, hp), lambda b, h: (ctx0 + b, h)),
            pl.BlockSpec((ctx_len, hp), lambda b, h: (ctx0 + b, h)),
            pl.BlockSpec((ctx_len, dv), lambda b, h: (ctx0 + b, h)),
            pl.BlockSpec(memory_space=pl.ANY),
        ],
        out_specs=pl.BlockSpec((ctx_len, dv), lambda b, h: (ctx0 + b, h)),
        out_shape=jax.ShapeDtypeStruct((n_rows, MLA_HEADS * dv), BF16),
        input_output_aliases={3: 0},
        compiler_params=_params("parallel", "parallel"),
        name="attn_ctx",
    )(q, k, v, o_lat)


def _retention_tables(seq, n_ctx, dk):
    inv = RET_ROPE_BASE ** (-jnp.linspace(0.0, 1.0, dk // 2, dtype=F32))
    ang = jnp.arange(seq, dtype=F32)[:, None] * inv[None, :]
    return jnp.cos(ang), jnp.sin(ang), jnp.ones((n_ctx, dk // 2), F32), jnp.zeros((n_ctx, dk // 2), F32)


def _axial_tables(seq):
    n_f = MLA_D_ROPE // 4
    inv = AXIAL_ROPE_BASE ** (-jnp.arange(n_f, dtype=F32) * 2.0 / (MLA_D_ROPE // 2))
    tpos = jnp.arange(seq)
    row = (tpos // GRID_W).astype(F32)[:, None] * inv[None, :]
    col = (tpos % GRID_W).astype(F32)[:, None] * inv[None, :]
    z = jnp.zeros((seq, n_f), F32)
    tail = LANES - MLA_D_ROPE
    cos = jnp.concatenate([jnp.cos(row), jnp.cos(row), jnp.cos(col), jnp.cos(col), jnp.ones((seq, tail), F32)], 1)
    s_up = jnp.concatenate([-jnp.sin(row), z, -jnp.sin(col), z, jnp.zeros((seq, tail), F32)], 1)
    s_dn = jnp.concatenate([z, jnp.sin(row), z, jnp.sin(col), jnp.zeros((seq, tail), F32)], 1)
    return cos, s_up, s_dn


def _token_table(lat, batch, ctx_rows, fill):
    return jnp.concatenate([jnp.tile(lat, (batch, 1)), jnp.full((ctx_rows, lat.shape[1]), fill, F32)], 0)


def kernel(x, c, ctx, c_ctx, ada_w, ada_b, ln_g, ln_b, ret_w_qkv, ret_w_g, ret_decay_logit, ret_w_o,
           mla_w_dq, mla_g_q, mla_w_uq, mla_w_dkv, mla_g_kv, mla_w_ukv, mla_w_o, ffn_w_in, ffn_w_out):
    batch, seq, d = x.shape
    ctx_len = ctx.shape[1]
    depth = ada_w.shape[0]
    n_lat, n_ctx = batch * seq, batch * ctx_len
    n_tok = n_lat + n_ctx
    ret_heads = ret_decay_logit.shape[-1]
    assert ctx_len == RET_CHUNK and seq % RET_CHUNK == 0 and batch + 1 <= 8
    tm = _row_tile(n_lat, n_ctx)
    alpha = (2 * depth) ** 0.25

    cond8 = jnp.concatenate([c, c_ctx[None, :], jnp.zeros((8 - batch - 1, d), F32)], 0)
    mod = _modulation(cond8, ada_w, ada_b).reshape(depth * 8 * 6, 1, d)

    def mod_idx(layer, chunk, tile):
        def f(i):
            r = jnp.where(i * tile >= n_lat, batch, (i * tile) // seq)
            return (layer * 8 + r) * 6 + chunk
        return f

    ln_g3 = ln_g.reshape(depth * 2, 1, d)
    ln_b3 = ln_b.reshape(depth * 2, 1, d)

    ret_cos, ret_sin, one_c, zero_c = _retention_tables(seq, n_ctx, d // ret_heads)
    ret_cos = jnp.concatenate([jnp.tile(ret_cos, (batch, 1)), one_c], 0)
    ret_sin = jnp.concatenate([jnp.tile(ret_sin, (batch, 1)), zero_c], 0)
    ax_cos, ax_up, ax_dn = _axial_tables(seq)
    ax_tabs = (_token_table(ax_cos, batch, n_ctx, 1.0), _token_table(ax_up, batch, n_ctx, 0.0),
               _token_table(ax_dn, batch, n_ctx, 0.0))

    tm_ln = 512
    h, u = _modulate(x.reshape(n_lat, d), ctx.reshape(n_ctx, d), mod,
                     mod_idx(0, 1, tm_ln), mod_idx(0, 0, tm_ln), tm_ln)

    ret_w_o_b, mla_w_o_b, ffn_w_out_b = (w.astype(BF16) for w in (ret_w_o, mla_w_o, ffn_w_out))

    for i in range(depth):
        last = i == depth - 1
        n_rows = n_lat if last else n_tok
        j = i // 2
        if i % 2 == 0:
            qkv = _ret_qkv(u, ret_w_qkv, j, ret_cos, ret_sin, ret_heads, tm)
            decay_tile = jnp.broadcast_to(
                ret_decay_logit[j].astype(F32).reshape(2 * ret_heads, 1, 1), (2 * ret_heads, 8, LANES))
            o_f, o_b = _retention(qkv, decay_tile, batch, seq, ret_heads)
            y = _ret_gate(u, ret_w_g, j, o_f, o_b, tm)
            w_o = ret_w_o_b
        else:
            rq, rkv = mla_g_q.shape[1], mla_g_kv.shape[1]
            w_down = jnp.concatenate(
                [mla_w_dq[j], mla_w_dkv[j], jnp.zeros((d, LANES - MLA_D_ROPE), F32)], 1).astype(BF16)
            cq, ckv, kr = _mla_down(u, w_down, mla_g_q[j][None, :], mla_g_kv[j][None, :], ax_tabs, 512)
            wq_pad = jnp.pad(
                mla_w_uq[j].reshape(rq, MLA_HEADS, MLA_D_NOPE + MLA_D_ROPE),
                ((0, 0), (0, 0), (0, MLA_HEAD_PAD - MLA_D_NOPE - MLA_D_ROPE)),
            ).reshape(rq, MLA_HEADS * MLA_HEAD_PAD).astype(BF16)
            q, k, v = _mla_up(cq, ckv, kr, wq_pad, mla_w_ukv[j].astype(BF16), ax_tabs, 512)
            y = _attention(q, k, v, batch, seq, ctx_len, not last)
            w_o = mla_w_o_b

        h, u = _out_ln(y, w_o, j, h, mod, ln_g3, ln_b3, gate_idx=mod_idx(i, 2, tm_ln), ln_idx=2 * i,
                       sc_idx=mod_idx(i, 4, tm_ln), sh_idx=mod_idx(i, 3, tm_ln), alpha=alpha,
                       n_rows=n_rows, tm=tm_ln, tk=2048)
        hid = _ffn_in(u, ffn_w_in, i, n_rows, tm)
        nxt = min(i + 1, depth - 1)
        h, u = _out_ln(hid, ffn_w_out_b, i, h, mod, ln_g3, ln_b3, gate_idx=mod_idx(i, 5, tm_ln),
                       ln_idx=2 * i + 1, sc_idx=mod_idx(nxt, 1, tm_ln), sh_idx=mod_idx(nxt, 0, tm_ln),
                       alpha=alpha, n_rows=n_rows, tm=tm_ln, tk=ffn_w_out.shape[1] // 4)
    return h.reshape(batch, seq, d)
```

```python
import functools

import jax
import jax.numpy as jnp
from jax import lax
from jax.experimental import pallas as pl
from jax.experimental.pallas import tpu as pltpu

F32 = jnp.float32
BF16 = jnp.bfloat16

GRID_W = 64
RET_ROPE_BASE = 10000.0
GN_EPS = 1e-6
MLA_HEADS = 16
MLA_D_NOPE = 128
MLA_D_ROPE = 64
MLA_D_V = 128
MLA_SCALE = (MLA_D_NOPE + MLA_D_ROPE) ** -0.5
MLA_Q_SCALE = MLA_SCALE * 1.4426950408889634
AXIAL_ROPE_BASE = 10000.0
RMS_EPS = 1e-6
LN_EPS = 1e-5

LANES = 128
MXU_DIM = 256
VMEM_LIMIT_BYTES = 56 * 1024 * 1024

RET_CHUNK = 256
MLA_HEAD_PAD = 2 * LANES


def _params(*sem):
    return pltpu.CompilerParams(dimension_semantics=sem, vmem_limit_bytes=VMEM_LIMIT_BYTES)


def _silu(x):
    return x / (1.0 + jnp.exp(-x))


def _row_tile(n_lat, n_ctx):
    for t in (1024, 512, 256):
        if n_lat % t == 0 and n_ctx % t == 0:
            return t
    raise ValueError("token counts must be multiples of 256")


def _mod_kernel(cond_ref, w_ref, b_ref, o_ref):
    s = _silu(cond_ref[...]).astype(BF16)
    acc = jnp.dot(s, w_ref[0].astype(BF16), preferred_element_type=F32)
    o_ref[0] = acc + b_ref[0]


def _modulation(cond8, ada_w, ada_b):
    depth, d, n = ada_w.shape
    tn = 1024
    return pl.pallas_call(
        _mod_kernel,
        grid=(depth, n // tn),
        in_specs=[
            pl.BlockSpec((8, d), lambda l, j: (0, 0)),
            pl.BlockSpec((1, d, tn), lambda l, j: (l, 0, j)),
            pl.BlockSpec((1, 1, tn), lambda l, j: (l, 0, j)),
        ],
        out_specs=pl.BlockSpec((1, 8, tn), lambda l, j: (l, 0, j)),
        out_shape=jax.ShapeDtypeStruct((depth, 8, n), F32),
        compiler_params=_params("parallel", "parallel"),
        name="adaln_mod",
    )(cond8, ada_w, ada_b.reshape(depth, 1, n))


def _modulate_kernel(x_ref, ctx_ref, sc_ref, sh_ref, h_ref, u_ref, *, n_lat_tiles):
    def emit(src_ref):
        hv = src_ref[...]
        h_ref[...] = hv
        u_ref[...] = (hv * (1.0 + sc_ref[0]) + sh_ref[0]).astype(BF16)

    is_lat = pl.program_id(0) < n_lat_tiles
    pl.when(is_lat)(lambda: emit(x_ref))
    pl.when(jnp.logical_not(is_lat))(lambda: emit(ctx_ref))


def _modulate(x2d, ctx2d, mod, sc_idx, sh_idx, tm):
    n_lat, d = x2d.shape
    n_ctx = ctx2d.shape[0]
    t = n_lat + n_ctx
    n_lat_tiles = n_lat // tm
    vec = lambda f: pl.BlockSpec((1, 1, d), lambda i: (f(i), 0, 0))
    row = pl.BlockSpec((tm, d), lambda i: (i, 0))
    return pl.pallas_call(
        functools.partial(_modulate_kernel, n_lat_tiles=n_lat_tiles),
        grid=(t // tm,),
        in_specs=[
            pl.BlockSpec((tm, d), lambda i: (jnp.minimum(i, n_lat_tiles - 1), 0)),
            pl.BlockSpec((tm, d), lambda i: (jnp.maximum(i - n_lat_tiles, 0), 0)),
            vec(sc_idx), vec(sh_idx),
        ],
        out_specs=[row, row],
        out_shape=[jax.ShapeDtypeStruct((t, d), F32), jax.ShapeDtypeStruct((t, d), BF16)],
        compiler_params=_params("arbitrary"),
        name="modulate_in",
    )(x2d, ctx2d, mod, mod)


def _cast_weights_once(pairs):
    @pl.when(pl.program_id(1) == 0)
    def _():
        for src_ref, dst_ref in pairs:
            dst_ref[...] = src_ref[...].astype(BF16)


def _row_groups(ref, n_split):
    rows_per = ref.shape[0] // n_split
    return [pl.ds(r * rows_per, rows_per) for r in range(n_split)]


def _ret_qkv_kernel(x_ref, w_ref, cos_ref, sin_ref, o_ref, wb_ref, *, n_q, n_k, k_scale, n_split):
    j = pl.program_id(0)
    tn = o_ref.shape[1]
    _cast_weights_once([(w_ref, wb_ref)])

    def product(rows):
        return jnp.dot(x_ref[rows, :], wb_ref[...], preferred_element_type=F32)

    @pl.when(j < n_q + n_k)
    def _():
        scale = jnp.where(j < n_q, 1.0, k_scale).astype(F32)
        for rows in _row_groups(o_ref, n_split):
            acc = product(rows)
            c = cos_ref[rows, :]
            s = sin_ref[rows, :]
            for g in range(tn // MXU_DIM):
                lo = g * MXU_DIM
                x1 = acc[:, lo:lo + LANES] * scale
                x2 = acc[:, lo + LANES:lo + MXU_DIM] * scale
                o_ref[rows, lo:lo + LANES] = (x1 * c - x2 * s).astype(BF16)
                o_ref[rows, lo + LANES:lo + MXU_DIM] = (x1 * s + x2 * c).astype(BF16)

    @pl.when(j >= n_q + n_k)
    def _():
        for rows in _row_groups(o_ref, n_split):
            o_ref[rows, :] = product(rows).astype(BF16)


def _ret_qkv(u, w, layer, cos, sin, n_heads, tm):
    t, d = u.shape
    n = w.shape[2]
    dk = d // n_heads
    tn = 1024
    n_q = n_heads * dk // tn
    kern = functools.partial(_ret_qkv_kernel, n_q=n_q, n_k=n_q, k_scale=dk ** -0.5, n_split=2)
    return pl.pallas_call(
        kern,
        grid=(n // tn, t // tm),
        in_specs=[
            pl.BlockSpec((tm, d), lambda j, i: (i, 0)),
            pl.BlockSpec((None, d, tn), lambda j, i: (layer, 0, j)),
            pl.BlockSpec((tm, LANES), lambda j, i: (i, 0)),
            pl.BlockSpec((tm, LANES), lambda j, i: (i, 0)),
        ],
        out_specs=pl.BlockSpec((tm, tn), lambda j, i: (i, j)),
        out_shape=jax.ShapeDtypeStruct((t, n), BF16),
        scratch_shapes=[pltpu.VMEM((d, tn), BF16)],
        compiler_params=_params("parallel", "arbitrary"),
        name="ret_qkv",
    )(u, w, cos, sin)


def _retention_kernel(dec_ref, qf_ref, qb_ref, of_ref, ob_ref, s_ref, intra_ref, qdec_ref, kdec_ref,
                      *, n_heads, dk, dv):
    c = RET_CHUNK
    step = pl.program_id(1)

    def log_gamma(idx):
        x = dec_ref[idx][0:1, 0:1]
        return jnp.minimum(x, 0.0) - jnp.log1p(jnp.exp(-jnp.abs(x)))

    @pl.when(step == 0)
    def _():
        s_ref[...] = jnp.zeros_like(s_ref)
        row = lax.broadcasted_iota(jnp.int32, (c, c), 0)
        col = lax.broadcasted_iota(jnp.int32, (c, c), 1)
        pos = lax.broadcasted_iota(jnp.int32, (c, dk), 0).astype(F32)
        for d in range(2):
            diff = (row - col) if d == 0 else (col - row)
            dist = jnp.maximum(diff, 0).astype(F32)
            ahead = pos + 1.0 if d == 0 else float(c) - pos
            behind = float(c - 1) - pos if d == 0 else pos
            for h in range(n_heads):
                idx = d * n_heads + h
                lg = log_gamma(idx)
                intra_ref[idx] = jnp.where(diff >= 0, jnp.exp(lg * dist), 0.0)
                qdec_ref[idx] = jnp.exp(lg * ahead).astype(BF16)
                kdec_ref[idx] = jnp.exp(lg * behind).astype(BF16)

    for d, (x_ref, o_ref) in enumerate(((qf_ref, of_ref), (qb_ref, ob_ref))):
        for h in range(n_heads):
            idx = d * n_heads + h
            q = x_ref[:, h * dk:(h + 1) * dk]
            k = x_ref[:, n_heads * dk + h * dk:n_heads * dk + (h + 1) * dk]
            v = x_ref[:, 2 * n_heads * dk + h * dv:2 * n_heads * dk + (h + 1) * dv]
            scores = lax.dot_general(q, k, (((1,), (1,)), ((), ())), preferred_element_type=F32)
            p = (scores * intra_ref[idx]).astype(BF16)
            state = s_ref[idx]
            o = (jnp.dot(p, v, preferred_element_type=F32)
                 + jnp.dot(q * qdec_ref[idx], state.astype(BF16), preferred_element_type=F32))
            s_ref[idx] = state * jnp.exp(log_gamma(idx) * float(c)) + lax.dot_general(
                k * kdec_ref[idx], v, (((0,), (0,)), ((), ())), preferred_element_type=F32)
            mu = jnp.mean(o, axis=-1, keepdims=True)
            ctr = o - mu
            var = jnp.mean(ctr * ctr, axis=-1, keepdims=True)
            o_ref[:, h * dv:(h + 1) * dv] = (ctr * lax.rsqrt(var + GN_EPS)).astype(BF16)


def _retention(qkv, decay_tile, batch, seq, n_heads):
    t, n = qkv.shape
    d = n // 4
    dk, dv = d // n_heads, 2 * d // n_heads
    c = RET_CHUNK
    n_lat = seq // c
    ctx_blk = batch * n_lat

    def fwd_blk(b, s):
        return jnp.where(s == 0, ctx_blk + b, b * n_lat + s - 1)

    def bwd_blk(b, s):
        return jnp.where(s == 0, ctx_blk + b, b * n_lat + n_lat - s)

    kern = functools.partial(_retention_kernel, n_heads=n_heads, dk=dk, dv=dv)
    out = jax.ShapeDtypeStruct((t, n_heads * dv), BF16)
    return pl.pallas_call(
        kern,
        grid=(batch, n_lat + 1),
        in_specs=[
            pl.BlockSpec((2 * n_heads, 8, LANES), lambda b, s: (0, 0, 0)),
            pl.BlockSpec((c, n), lambda b, s: (fwd_blk(b, s), 0)),
            pl.BlockSpec((c, n), lambda b, s: (bwd_blk(b, s), 0)),
        ],
        out_specs=[
            pl.BlockSpec((c, n_heads * dv), lambda b, s: (fwd_blk(b, s), 0)),
            pl.BlockSpec((c, n_heads * dv), lambda b, s: (bwd_blk(b, s), 0)),
        ],
        out_shape=[out, out],
        scratch_shapes=[
            pltpu.VMEM((2 * n_heads, dk, dv), F32),
            pltpu.VMEM((2 * n_heads, c, c), F32),
            pltpu.VMEM((2 * n_heads, c, dk), BF16),
            pltpu.VMEM((2 * n_heads, c, dk), BF16),
        ],
        compiler_params=_params("parallel", "arbitrary"),
        name="retention_scan",
    )(decay_tile, qkv, qkv)


def _ret_gate_kernel(x_ref, wf_ref, wb_ref, of_ref, ob_ref, y_ref, wf_bf, wb_bf, *, n_split):
    _cast_weights_once([(wf_ref, wf_bf), (wb_ref, wb_bf)])
    for rows in _row_groups(y_ref, n_split):
        x = x_ref[rows, :]
        gf = jnp.dot(x, wf_bf[...], preferred_element_type=F32)
        gb = jnp.dot(x, wb_bf[...], preferred_element_type=F32)
        y = _silu(gf) * of_ref[rows, :].astype(F32) + _silu(gb) * ob_ref[rows, :].astype(F32)
        y_ref[rows, :] = y.astype(BF16)


def _ret_gate(u, w_g, layer, o_f, o_b, tm):
    t, d = u.shape
    n = w_g.shape[2] // 2
    tn = 512
    nb = n // tn
    return pl.pallas_call(
        functools.partial(_ret_gate_kernel, n_split=2),
        grid=(nb, t // tm),
        in_specs=[
            pl.BlockSpec((tm, d), lambda j, i: (i, 0)),
            pl.BlockSpec((None, d, tn), lambda j, i: (layer, 0, j)),
            pl.BlockSpec((None, d, tn), lambda j, i: (layer, 0, nb + j)),
            pl.BlockSpec((tm, tn), lambda j, i: (i, j)),
            pl.BlockSpec((tm, tn), lambda j, i: (i, j)),
        ],
        out_specs=pl.BlockSpec((tm, tn), lambda j, i: (i, j)),
        out_shape=jax.ShapeDtypeStruct((t, n), BF16),
        scratch_shapes=[pltpu.VMEM((d, tn), BF16), pltpu.VMEM((d, tn), BF16)],
        compiler_params=_params("parallel", "arbitrary"),
        name="ret_gate",
    )(u, w_g, w_g, o_f, o_b)


def _out_ln_kernel(x_ref, w_ref, h_ref, gate_ref, lng_ref, lnb_ref, sc_ref, sh_ref,
                   hout_ref, u_ref, *acc, alpha, n_k, n_split):
    k = pl.program_id(1)

    def partial_product(rows):
        return jnp.dot(x_ref[rows, :], w_ref[...], preferred_element_type=F32)

    def finish(rows, y):
        z = alpha * h_ref[rows, :] + gate_ref[0] * y
        mu = jnp.mean(z, axis=-1, keepdims=True)
        ctr = z - mu
        var = jnp.mean(ctr * ctr, axis=-1, keepdims=True)
        hn = ctr * lax.rsqrt(var + LN_EPS) * lng_ref[0] + lnb_ref[0]
        hout_ref[rows, :] = hn
        u_ref[rows, :] = (hn * (1.0 + sc_ref[0]) + sh_ref[0]).astype(BF16)

    def last_step(prev):
        for rows in _row_groups(x_ref, n_split):
            y = partial_product(rows)
            finish(rows, y if prev is None else prev[rows, :] + y)

    if n_k == 1:
        last_step(None)
        return
    acc_ref = acc[0].at[pl.program_id(2)]

    @pl.when(k == 0)
    def _():
        acc_ref[...] = partial_product(slice(None))

    @pl.when(jnp.logical_and(k > 0, k < n_k - 1))
    def _():
        acc_ref[...] += partial_product(slice(None))

    @pl.when(k == n_k - 1)
    def _():
        last_step(acc_ref)


def _out_ln(x, w, layer, h, mod, ln_g, ln_b, *, gate_idx, ln_idx, sc_idx, sh_idx, alpha, n_rows, tm, tk,
            n_split=2):
    kdim = x.shape[1]
    d = w.shape[2]
    n_k = kdim // tk
    pair = 2 if n_rows % (2 * tm) == 0 else 1
    assert n_rows % (pair * tm) == 0 and kdim % tk == 0

    def row_blk(i, k, m):
        return pair * i + jnp.where(k == n_k - 1, m, 0)

    vec = lambda arr_idx: pl.BlockSpec((1, 1, d), lambda i, k, m: (arr_idx(row_blk(i, k, m)), 0, 0))
    row = pl.BlockSpec((tm, d), lambda i, k, m: (row_blk(i, k, m), 0))
    kern = functools.partial(_out_ln_kernel, alpha=alpha, n_k=n_k, n_split=n_split)
    return pl.pallas_call(
        kern,
        grid=(n_rows // (pair * tm), n_k, pair),
        in_specs=[
            pl.BlockSpec((tm, tk), lambda i, k, m: (pair * i + m, k)),
            pl.BlockSpec((None, tk, d), lambda i, k, m: (layer, k, 0)),
            row,
            vec(gate_idx),
            vec(lambda r: ln_idx),
            vec(lambda r: ln_idx),
            vec(sc_idx),
            vec(sh_idx),
        ],
        out_specs=[row, row],
        out_shape=[
            jax.ShapeDtypeStruct((n_rows, d), F32),
            jax.ShapeDtypeStruct((n_rows, d), BF16),
        ],
        scratch_shapes=[pltpu.VMEM((pair, tm, d), F32)] if n_k > 1 else [],
        compiler_params=_params("parallel", "arbitrary", "arbitrary"),
        name="out_ln",
    )(x, w, h, mod, ln_g, ln_b, mod, mod)


def _ffn_in_kernel(x_ref, wa_ref, wb_ref, o_ref, wa_bf, wb_bf, *, n_split):
    _cast_weights_once([(wa_ref, wa_bf), (wb_ref, wb_bf)])
    for rows in _row_groups(o_ref, n_split):
        x = x_ref[rows, :]
        a = jnp.dot(x, wa_bf[...], preferred_element_type=F32)
        b = jnp.dot(x, wb_bf[...], preferred_element_type=F32)
        o_ref[rows, :] = (_silu(a) * b).astype(BF16)


def _ffn_in(u, w_in, layer, n_rows, tm):
    d = u.shape[1]
    f = w_in.shape[2] // 2
    tn = 512
    nb = f // tn
    return pl.pallas_call(
        functools.partial(_ffn_in_kernel, n_split=2),
        grid=(nb, n_rows // tm),
        in_specs=[
            pl.BlockSpec((tm, d), lambda j, i: (i, 0)),
            pl.BlockSpec((None, d, tn), lambda j, i: (layer, 0, j)),
            pl.BlockSpec((None, d, tn), lambda j, i: (layer, 0, nb + j)),
        ],
        out_specs=pl.BlockSpec((tm, tn), lambda j, i: (i, j)),
        out_shape=jax.ShapeDtypeStruct((n_rows, f), BF16),
        scratch_shapes=[pltpu.VMEM((d, tn), BF16), pltpu.VMEM((d, tn), BF16)],
        compiler_params=_params("parallel", "arbitrary"),
        name="ffn_in",
    )(u, w_in, w_in)


def _rope_lanes(x, c_ref, s_up_ref, s_dn_ref):
    half = MLA_D_ROPE // 4
    up = pltpu.roll(x, LANES - half, 1)
    dn = pltpu.roll(x, half, 1)
    return x * c_ref[...] + up * s_up_ref[...] + dn * s_dn_ref[...]


def _rms(x, g):
    return x * lax.rsqrt(jnp.mean(x * x, axis=-1, keepdims=True) + RMS_EPS) * g


def _mla_down_kernel(x_ref, w_ref, gq_ref, gkv_ref, c_ref, su_ref, sd_ref,
                     cq_ref, ckv_ref, kr_ref, *, rq, rkv):
    for rows in _row_groups(x_ref, 2):
        acc = jnp.dot(x_ref[rows, :], w_ref[...], preferred_element_type=F32)
        cq_ref[rows, :] = _rms(acc[:, :rq], gq_ref[...]).astype(BF16)
        ckv_ref[rows, :] = _rms(acc[:, rq:rq + rkv], gkv_ref[...]).astype(BF16)
        kr_ref[rows, :] = _rope_lanes(acc[:, rq + rkv:], c_ref.at[rows, :], su_ref.at[rows, :],
                                      sd_ref.at[rows, :]).astype(BF16)


def _mla_down(u, w_down, g_q, g_kv, tabs, tm):
    t, d = u.shape
    rq, rkv = g_q.shape[1], g_kv.shape[1]
    n = w_down.shape[1]
    tab = pl.BlockSpec((tm, LANES), lambda i: (i, 0))
    kern = functools.partial(_mla_down_kernel, rq=rq, rkv=rkv)
    return pl.pallas_call(
        kern,
        grid=(t // tm,),
        in_specs=[
            pl.BlockSpec((tm, d), lambda i: (i, 0)),
            pl.BlockSpec((d, n), lambda i: (0, 0)),
            pl.BlockSpec((1, rq), lambda i: (0, 0)),
            pl.BlockSpec((1, rkv), lambda i: (0, 0)),
            tab, tab, tab,
        ],
        out_specs=[
            pl.BlockSpec((tm, rq), lambda i: (i, 0)),
            pl.BlockSpec((tm, rkv), lambda i: (i, 0)),
            pl.BlockSpec((tm, LANES), lambda i: (i, 0)),
        ],
        out_shape=[
            jax.ShapeDtypeStruct((t, rq), BF16),
            jax.ShapeDtypeStruct((t, rkv), BF16),
            jax.ShapeDtypeStruct((t, LANES), BF16),
        ],
        compiler_params=_params("parallel"),
        name="mla_down",
    )(u, w_down, g_q, g_kv, *tabs)


def _mla_up_kernel(cq_ref, ckv_ref, kr_ref, wq_ref, wkv_ref, c_ref, su_ref, sd_ref,
                   q_ref, k_ref, v_ref):
    cq = cq_ref[...]
    ckv = ckv_ref[...]
    kr = kr_ref[...]
    hp = MLA_HEAD_PAD
    for h in range(MLA_HEADS):
        qh = jnp.dot(cq, wq_ref[:, h * hp:(h + 1) * hp], preferred_element_type=F32) * MLA_Q_SCALE
        q_ref[:, h * hp:h * hp + LANES] = qh[:, :LANES].astype(BF16)
        q_ref[:, h * hp + LANES:(h + 1) * hp] = _rope_lanes(qh[:, LANES:], c_ref, su_ref, sd_ref).astype(BF16)
        kvh = jnp.dot(ckv, wkv_ref[:, h * hp:(h + 1) * hp], preferred_element_type=F32)
        k_ref[:, h * hp:h * hp + LANES] = kvh[:, :LANES].astype(BF16)
        k_ref[:, h * hp + LANES:(h + 1) * hp] = kr
        v_ref[:, h * MLA_D_V:(h + 1) * MLA_D_V] = kvh[:, LANES:].astype(BF16)


def _mla_up(cq, ckv, kr, wq_pad, wkv, tabs, tm):
    t, rq = cq.shape
    rkv = ckv.shape[1]
    nq = wq_pad.shape[1]
    nkv = wkv.shape[1]
    tab = pl.BlockSpec((tm, LANES), lambda i: (i, 0))
    return pl.pallas_call(
        _mla_up_kernel,
        grid=(t // tm,),
        in_specs=[
            pl.BlockSpec((tm, rq), lambda i: (i, 0)),
            pl.BlockSpec((tm, rkv), lambda i: (i, 0)),
            pl.BlockSpec((tm, LANES), lambda i: (i, 0)),
            pl.BlockSpec((rq, nq), lambda i: (0, 0)),
            pl.BlockSpec((rkv, nkv), lambda i: (0, 0)),
            tab, tab, tab,
        ],
        out_specs=[
            pl.BlockSpec((tm, nq), lambda i: (i, 0)),
            pl.BlockSpec((tm, nq), lambda i: (i, 0)),
            pl.BlockSpec((tm, MLA_HEADS * MLA_D_V), lambda i: (i, 0)),
        ],
        out_shape=[
            jax.ShapeDtypeStruct((t, nq), BF16),
            jax.ShapeDtypeStruct((t, nq), BF16),
            jax.ShapeDtypeStruct((t, MLA_HEADS * MLA_D_V), BF16),
        ],
        compiler_params=_params("parallel"),
        name="mla_up",
    )(cq, ckv, kr, wq_pad, wkv, *tabs)


def _scores(q, k):
    return lax.dot_general(q, k, (((1,), (1,)), ((), ())), preferred_element_type=F32)


def _attn_lat_kernel(q_ref, kl_ref, kc_ref, vl_ref, vc_ref, o_ref, *, key_chunk):
    q = q_ref[...]
    n_lat = kl_ref.shape[0] // key_chunk
    chunks = [(kl_ref, vl_ref, c * key_chunk, key_chunk) for c in range(n_lat)]
    chunks.append((kc_ref, vc_ref, 0, kc_ref.shape[0]))

    def chunk_scores(c):
        k_ref, _, lo, size = chunks[c]
        return _scores(q, k_ref[lo:lo + size, :])

    s = chunk_scores(0)
    m = l = acc = None
    for c, (_, v_ref, lo, size) in enumerate(chunks):
        s_next = chunk_scores(c + 1) if c + 1 < len(chunks) else None
        m_c = jnp.max(s, axis=-1, keepdims=True)
        v = v_ref[lo:lo + size, :]
        if c == 0:
            m = m_c
            p = jnp.exp2(s - m)
            l = jnp.sum(p, axis=-1, keepdims=True)
            acc = jnp.dot(p.astype(BF16), v, preferred_element_type=F32)
        else:
            m_new = jnp.maximum(m, m_c)
            alpha = jnp.exp2(m - m_new)
            p = jnp.exp2(s - m_new)
            l = alpha * l + jnp.sum(p, axis=-1, keepdims=True)
            acc = alpha * acc + jnp.dot(p.astype(BF16), v, preferred_element_type=F32)
            m = m_new
        s = s_next
    o_ref[...] = (acc / l).astype(BF16)


def _attn_ctx_kernel(q_ref, kc_ref, vc_ref, aliased_ref, o_ref):
    del aliased_ref
    s_c = _scores(q_ref[...], kc_ref[...])
    p_c = jnp.exp2(s_c - jnp.max(s_c, axis=-1, keepdims=True))
    denom = jnp.sum(p_c, axis=-1, keepdims=True)
    o = jnp.dot(p_c.astype(BF16), vc_ref[...], preferred_element_type=F32)
    o_ref[...] = (o / denom).astype(BF16)


def _attention(q, k, v, batch, seq, ctx_len, with_ctx_queries):
    t = q.shape[0]
    hp, dv = MLA_HEAD_PAD, MLA_D_V
    tq = 512
    nq = seq // tq
    ctx0 = batch * seq // ctx_len
    n_rows = t if with_ctx_queries else batch * seq
    o_lat = pl.pallas_call(
        functools.partial(_attn_lat_kernel, key_chunk=min(seq, 512)),
        grid=(batch, MLA_HEADS, nq),
        in_specs=[
            pl.BlockSpec((tq, hp), lambda b, h, i: (b * nq + i, h)),
            pl.BlockSpec((seq, hp), lambda b, h, i: (b, h)),
            pl.BlockSpec((ctx_len, hp), lambda b, h, i: (ctx0 + b, h)),
            pl.BlockSpec((seq, dv), lambda b, h, i: (b, h)),
            pl.BlockSpec((ctx_len, dv), lambda b, h, i: (ctx0 + b, h)),
        ],
        out_specs=pl.BlockSpec((tq, dv), lambda b, h, i: (b * nq + i, h)),
        out_shape=jax.ShapeDtypeStruct((n_rows, MLA_HEADS * dv), BF16),
        compiler_params=_params("parallel", "parallel", "arbitrary"),
        name="attn_lat",
    )(q, k, k, v, v)
    if not with_ctx_queries:
        return o_lat
    return pl.pallas_call(
        _attn_ctx_kernel,
        grid=(batch, MLA_HEADS),
        in_specs=[
            pl.BlockSpec((ctx_len, hp), lambda b, h: (ctx0 + b, h)),
            pl.BlockSpec((ctx_len, hp), lambda b, h: (ctx0 + b, h)),
            pl.BlockSpec((ctx_len, dv), lambda b, h: (ctx0 + b, h)),
            pl.BlockSpec(memory_space=pl.ANY),
        ],
        out_specs=pl.BlockSpec((ctx_len, dv), lambda b, h: (ctx0 + b, h)),
        out_shape=jax.ShapeDtypeStruct((n_rows, MLA_HEADS * dv), BF16),
        input_output_aliases={3: 0},
        compiler_params=_params("parallel", "parallel"),
        name="attn_ctx",
    )(q, k, v, o_lat)


def _retention_tables(seq, n_ctx, dk):
    inv = RET_ROPE_BASE ** (-jnp.linspace(0.0, 1.0, dk // 2, dtype=F32))
    ang = jnp.arange(seq, dtype=F32)[:, None] * inv[None, :]
    return jnp.cos(ang), jnp.sin(ang), jnp.ones((n_ctx, dk // 2), F32), jnp.zeros((n_ctx, dk // 2), F32)


def _axial_tables(seq):
    n_f = MLA_D_ROPE // 4
    inv = AXIAL_ROPE_BASE ** (-jnp.arange(n_f, dtype=F32) * 2.0 / (MLA_D_ROPE // 2))
    tpos = jnp.arange(seq)
    row = (tpos // GRID_W).astype(F32)[:, None] * inv[None, :]
    col = (tpos % GRID_W).astype(F32)[:, None] * inv[None, :]
    z = jnp.zeros((seq, n_f), F32)
    tail = LANES - MLA_D_ROPE
    cos = jnp.concatenate([jnp.cos(row), jnp.cos(row), jnp.cos(col), jnp.cos(col), jnp.ones((seq, tail), F32)], 1)
    s_up = jnp.concatenate([-jnp.sin(row), z, -jnp.sin(col), z, jnp.zeros((seq, tail), F32)], 1)
    s_dn = jnp.concatenate([z, jnp.sin(row), z, jnp.sin(col), jnp.zeros((seq, tail), F32)], 1)
    return cos, s_up, s_dn


def _token_table(lat, batch, ctx_rows, fill):
    return jnp.concatenate([jnp.tile(lat, (batch, 1)), jnp.full((ctx_rows, lat.shape[1]), fill, F32)], 0)


def kernel(x, c, ctx, c_ctx, ada_w, ada_b, ln_g, ln_b, ret_w_qkv, ret_w_g, ret_decay_logit, ret_w_o,
           mla_w_dq, mla_g_q, mla_w_uq, mla_w_dkv, mla_g_kv, mla_w_ukv, mla_w_o, ffn_w_in, ffn_w_out):
    batch, seq, d = x.shape
    ctx_len = ctx.shape[1]
    depth = ada_w.shape[0]
    n_lat, n_ctx = batch * seq, batch * ctx_len
    n_tok = n_lat + n_ctx
    ret_heads = ret_decay_logit.shape[-1]
    assert ctx_len == RET_CHUNK and seq % RET_CHUNK == 0 and batch + 1 <= 8
    tm = _row_tile(n_lat, n_ctx)
    alpha = (2 * depth) ** 0.25

    cond8 = jnp.concatenate([c, c_ctx[None, :], jnp.zeros((8 - batch - 1, d), F32)], 0)
    mod = _modulation(cond8, ada_w, ada_b).reshape(depth * 8 * 6, 1, d)

    def mod_idx(layer, chunk, tile):
        def f(i):
            r = jnp.where(i * tile >= n_lat, batch, (i * tile) // seq)
            return (layer * 8 + r) * 6 + chunk
        return f

    ln_g3 = ln_g.reshape(depth * 2, 1, d)
    ln_b3 = ln_b.reshape(depth * 2, 1, d)

    ret_cos, ret_sin, one_c, zero_c = _retention_tables(seq, n_ctx, d // ret_heads)
    ret_cos = jnp.concatenate([jnp.tile(ret_cos, (batch, 1)), one_c], 0)
    ret_sin = jnp.concatenate([jnp.tile(ret_sin, (batch, 1)), zero_c], 0)
    ax_cos, ax_up, ax_dn = _axial_tables(seq)
    ax_tabs = (_token_table(ax_cos, batch, n_ctx, 1.0), _token_table(ax_up, batch, n_ctx, 0.0),
               _token_table(ax_dn, batch, n_ctx, 0.0))

    tm_ln = 512
    h, u = _modulate(x.reshape(n_lat, d), ctx.reshape(n_ctx, d), mod,
                     mod_idx(0, 1, tm_ln), mod_idx(0, 0, tm_ln), tm_ln)

    ret_w_o_b, mla_w_o_b, ffn_w_out_b = (w.astype(BF16) for w in (ret_w_o, mla_w_o, ffn_w_out))

    for i in range(depth):
        last = i == depth - 1
        n_rows = n_lat if last else n_tok
        j = i // 2
        if i % 2 == 0:
            qkv = _ret_qkv(u, ret_w_qkv, j, ret_cos, ret_sin, ret_heads, tm)
            decay_tile = jnp.broadcast_to(
                ret_decay_logit[j].astype(F32).reshape(2 * ret_heads, 1, 1), (2 * ret_heads, 8, LANES))
            o_f, o_b = _retention(qkv, decay_tile, batch, seq, ret_heads)
            y = _ret_gate(u, ret_w_g, j, o_f, o_b, tm)
            w_o = ret_w_o_b
        else:
            rq, rkv = mla_g_q.shape[1], mla_g_kv.shape[1]
            w_down = jnp.concatenate(
                [mla_w_dq[j], mla_w_dkv[j], jnp.zeros((d, LANES - MLA_D_ROPE), F32)], 1).astype(BF16)
            cq, ckv, kr = _mla_down(u, w_down, mla_g_q[j][None, :], mla_g_kv[j][None, :], ax_tabs, 512)
            wq_pad = jnp.pad(
                mla_w_uq[j].reshape(rq, MLA_HEADS, MLA_D_NOPE + MLA_D_ROPE),
                ((0, 0), (0, 0), (0, MLA_HEAD_PAD - MLA_D_NOPE - MLA_D_ROPE)),
            ).reshape(rq, MLA_HEADS * MLA_HEAD_PAD).astype(BF16)
            q, k, v = _mla_up(cq, ckv, kr, wq_pad, mla_w_ukv[j].astype(BF16), ax_tabs, 512)
            y = _attention(q, k, v, batch, seq, ctx_len, not last)
            w_o = mla_w_o_b

        h, u = _out_ln(y, w_o, j, h, mod, ln_g3, ln_b3, gate_idx=mod_idx(i, 2, tm_ln), ln_idx=2 * i,
                       sc_idx=mod_idx(i, 4, tm_ln), sh_idx=mod_idx(i, 3, tm_ln), alpha=alpha,
                       n_rows=n_rows, tm=tm_ln, tk=2048)
        hid = _ffn_in(u, ffn_w_in, i, n_rows, tm)
        nxt = min(i + 1, depth - 1)
        h, u = _out_ln(hid, ffn_w_out_b, i, h, mod, ln_g3, ln_b3, gate_idx=mod_idx(i, 5, tm_ln),
                       ln_idx=2 * i + 1, sc_idx=mod_idx(nxt, 1, tm_ln), sh_idx=mod_idx(nxt, 0, tm_ln),
                       alpha=alpha, n_rows=n_rows, tm=tm_ln, tk=ffn_w_out.shape[1] // 4)
    return h.reshape(batch, seq, d)
```

```python
import functools

import jax
import jax.numpy as jnp
from jax import lax
from jax.experimental import pallas as pl
from jax.experimental.pallas import tpu as pltpu

F32 = jnp.float32
BF16 = jnp.bfloat16

GRID_W = 64
RET_ROPE_BASE = 10000.0
GN_EPS = 1e-6
MLA_HEADS = 16
MLA_D_NOPE = 128
MLA_D_ROPE = 64
MLA_D_V = 128
MLA_SCALE = (MLA_D_NOPE + MLA_D_ROPE) ** -0.5
MLA_Q_SCALE = MLA_SCALE * 1.4426950408889634
AXIAL_ROPE_BASE = 10000.0
RMS_EPS = 1e-6
LN_EPS = 1e-5

LANES = 128
MXU_DIM = 256
VMEM_LIMIT_BYTES = 56 * 1024 * 1024

RET_CHUNK = 256
MLA_HEAD_PAD = 2 * LANES


def _params(*sem):
    return pltpu.CompilerParams(dimension_semantics=sem, vmem_limit_bytes=VMEM_LIMIT_BYTES)


def _silu(x):
    return 0.5 * x * (1.0 + jnp.tanh(0.5 * x))


def _row_tile(n_lat, n_ctx):
    for t in (1024, 512, 256):
        if n_lat % t == 0 and n_ctx % t == 0:
            return t
    raise ValueError("token counts must be multiples of 256")


def _mod_kernel(cond_ref, w_ref, b_ref, o_ref):
    s = _silu(cond_ref[...]).astype(BF16)
    acc = jnp.dot(s, w_ref[0].astype(BF16), preferred_element_type=F32)
    o_ref[0] = acc + b_ref[0]


def _modulation(cond8, ada_w, ada_b):
    depth, d, n = ada_w.shape
    tn = 1024
    return pl.pallas_call(
        _mod_kernel,
        grid=(depth, n // tn),
        in_specs=[
            pl.BlockSpec((8, d), lambda l, j: (0, 0)),
            pl.BlockSpec((1, d, tn), lambda l, j: (l, 0, j)),
            pl.BlockSpec((1, 1, tn), lambda l, j: (l, 0, j)),
        ],
        out_specs=pl.BlockSpec((1, 8, tn), lambda l, j: (l, 0, j)),
        out_shape=jax.ShapeDtypeStruct((depth, 8, n), F32),
        compiler_params=_params("parallel", "parallel"),
        name="adaln_mod",
    )(cond8, ada_w, ada_b.reshape(depth, 1, n))


def _modulate_kernel(x_ref, ctx_ref, sc_ref, sh_ref, h_ref, u_ref, *, n_lat_tiles):
    def emit(src_ref):
        hv = src_ref[...]
        h_ref[...] = hv
        u_ref[...] = (hv * (1.0 + sc_ref[0]) + sh_ref[0]).astype(BF16)

    is_lat = pl.program_id(0) < n_lat_tiles
    pl.when(is_lat)(lambda: emit(x_ref))
    pl.when(jnp.logical_not(is_lat))(lambda: emit(ctx_ref))


def _modulate(x2d, ctx2d, mod, sc_idx, sh_idx, tm):
    n_lat, d = x2d.shape
    n_ctx = ctx2d.shape[0]
    t = n_lat + n_ctx
    n_lat_tiles = n_lat // tm
    vec = lambda f: pl.BlockSpec((1, 1, d), lambda i: (f(i), 0, 0))
    row = pl.BlockSpec((tm, d), lambda i: (i, 0))
    return pl.pallas_call(
        functools.partial(_modulate_kernel, n_lat_tiles=n_lat_tiles),
        grid=(t // tm,),
        in_specs=[
            pl.BlockSpec((tm, d), lambda i: (jnp.minimum(i, n_lat_tiles - 1), 0)),
            pl.BlockSpec((tm, d), lambda i: (jnp.maximum(i - n_lat_tiles, 0), 0)),
            vec(sc_idx), vec(sh_idx),
        ],
        out_specs=[row, row],
        out_shape=[jax.ShapeDtypeStruct((t, d), F32), jax.ShapeDtypeStruct((t, d), BF16)],
        compiler_params=_params("arbitrary"),
        name="modulate_in",
    )(x2d, ctx2d, mod, mod)


def _cast_weights_once(pairs):
    @pl.when(pl.program_id(1) == 0)
    def _():
        for src_ref, dst_ref in pairs:
            dst_ref[...] = src_ref[...].astype(BF16)


def _row_groups(ref, n_split):
    rows_per = ref.shape[0] // n_split
    return [pl.ds(r * rows_per, rows_per) for r in range(n_split)]


def _ret_qkv_kernel(x_ref, w_ref, cos_ref, sin_ref, o_ref, wb_ref, *, n_q, n_k, k_scale, n_split):
    j = pl.program_id(0)
    tn = o_ref.shape[1]
    _cast_weights_once([(w_ref, wb_ref)])

    def product(rows):
        return jnp.dot(x_ref[rows, :], wb_ref[...], preferred_element_type=F32)

    @pl.when(j < n_q + n_k)
    def _():
        scale = jnp.where(j < n_q, 1.0, k_scale).astype(F32)
        for rows in _row_groups(o_ref, n_split):
            acc = product(rows)
            c = cos_ref[rows, :]
            s = sin_ref[rows, :]
            for g in range(tn // MXU_DIM):
                lo = g * MXU_DIM
                x1 = acc[:, lo:lo + LANES] * scale
                x2 = acc[:, lo + LANES:lo + MXU_DIM] * scale
                o_ref[rows, lo:lo + LANES] = (x1 * c - x2 * s).astype(BF16)
                o_ref[rows, lo + LANES:lo + MXU_DIM] = (x1 * s + x2 * c).astype(BF16)

    @pl.when(j >= n_q + n_k)
    def _():
        for rows in _row_groups(o_ref, n_split):
            o_ref[rows, :] = product(rows).astype(BF16)


def _ret_qkv(u, w, layer, cos, sin, n_heads, tm):
    t, d = u.shape
    n = w.shape[2]
    dk = d // n_heads
    tn = 1024
    n_q = n_heads * dk // tn
    kern = functools.partial(_ret_qkv_kernel, n_q=n_q, n_k=n_q, k_scale=dk ** -0.5, n_split=2)
    return pl.pallas_call(
        kern,
        grid=(n // tn, t // tm),
        in_specs=[
            pl.BlockSpec((tm, d), lambda j, i: (i, 0)),
            pl.BlockSpec((None, d, tn), lambda j, i: (layer, 0, j)),
            pl.BlockSpec((tm, LANES), lambda j, i: (i, 0)),
            pl.BlockSpec((tm, LANES), lambda j, i: (i, 0)),
        ],
        out_specs=pl.BlockSpec((tm, tn), lambda j, i: (i, j)),
        out_shape=jax.ShapeDtypeStruct((t, n), BF16),
        scratch_shapes=[pltpu.VMEM((d, tn), BF16)],
        compiler_params=_params("parallel", "arbitrary"),
        name="ret_qkv",
    )(u, w, cos, sin)


def _retention_kernel(dec_ref, qf_ref, qb_ref, of_ref, ob_ref, s_ref, intra_ref, qdec_ref, kdec_ref,
                      *, n_heads, dk, dv):
    c = RET_CHUNK
    step = pl.program_id(1)

    def log_gamma(idx):
        x = dec_ref[idx][0:1, 0:1]
        return jnp.minimum(x, 0.0) - jnp.log1p(jnp.exp(-jnp.abs(x)))

    @pl.when(step == 0)
    def _():
        s_ref[...] = jnp.zeros_like(s_ref)
        row = lax.broadcasted_iota(jnp.int32, (c, c), 0)
        col = lax.broadcasted_iota(jnp.int32, (c, c), 1)
        pos = lax.broadcasted_iota(jnp.int32, (c, dk), 0).astype(F32)
        for d in range(2):
            diff = (row - col) if d == 0 else (col - row)
            dist = jnp.maximum(diff, 0).astype(F32)
            ahead = pos + 1.0 if d == 0 else float(c) - pos
            behind = float(c - 1) - pos if d == 0 else pos
            for h in range(n_heads):
                idx = d * n_heads + h
                lg = log_gamma(idx)
                intra_ref[idx] = jnp.where(diff >= 0, jnp.exp(lg * dist), 0.0)
                qdec_ref[idx] = jnp.exp(lg * ahead).astype(BF16)
                kdec_ref[idx] = jnp.exp(lg * behind).astype(BF16)

    for d, (x_ref, o_ref) in enumerate(((qf_ref, of_ref), (qb_ref, ob_ref))):
        for h in range(n_heads):
            idx = d * n_heads + h
            q = x_ref[:, h * dk:(h + 1) * dk]
            k = x_ref[:, n_heads * dk + h * dk:n_heads * dk + (h + 1) * dk]
            v = x_ref[:, 2 * n_heads * dk + h * dv:2 * n_heads * dk + (h + 1) * dv]
            scores = lax.dot_general(q, k, (((1,), (1,)), ((), ())), preferred_element_type=F32)
            p = (scores * intra_ref[idx]).astype(BF16)
            state = s_ref[idx]
            o = (jnp.dot(p, v, preferred_element_type=F32)
                 + jnp.dot(q * qdec_ref[idx], state.astype(BF16), preferred_element_type=F32))
            s_ref[idx] = state * jnp.exp(log_gamma(idx) * float(c)) + lax.dot_general(
                k * kdec_ref[idx], v, (((0,), (0,)), ((), ())), preferred_element_type=F32)
            mu = jnp.mean(o, axis=-1, keepdims=True)
            ctr = o - mu
            var = jnp.mean(ctr * ctr, axis=-1, keepdims=True)
            o_ref[:, h * dv:(h + 1) * dv] = (ctr * lax.rsqrt(var + GN_EPS)).astype(BF16)


def _retention(qkv, decay_tile, batch, seq, n_heads):
    t, n = qkv.shape
    d = n // 4
    dk, dv = d // n_heads, 2 * d // n_heads
    c = RET_CHUNK
    n_lat = seq // c
    ctx_blk = batch * n_lat

    def fwd_blk(b, s):
        return jnp.where(s == 0, ctx_blk + b, b * n_lat + s - 1)

    def bwd_blk(b, s):
        return jnp.where(s == 0, ctx_blk + b, b * n_lat + n_lat - s)

    kern = functools.partial(_retention_kernel, n_heads=n_heads, dk=dk, dv=dv)
    out = jax.ShapeDtypeStruct((t, n_heads * dv), BF16)
    return pl.pallas_call(
        kern,
        grid=(batch, n_lat + 1),
        in_specs=[
            pl.BlockSpec((2 * n_heads, 8, LANES), lambda b, s: (0, 0, 0)),
            pl.BlockSpec((c, n), lambda b, s: (fwd_blk(b, s), 0)),
            pl.BlockSpec((c, n), lambda b, s: (bwd_blk(b, s), 0)),
        ],
        out_specs=[
            pl.BlockSpec((c, n_heads * dv), lambda b, s: (fwd_blk(b, s), 0)),
            pl.BlockSpec((c, n_heads * dv), lambda b, s: (bwd_blk(b, s), 0)),
        ],
        out_shape=[out, out],
        scratch_shapes=[
            pltpu.VMEM((2 * n_heads, dk, dv), F32),
            pltpu.VMEM((2 * n_heads, c, c), F32),
            pltpu.VMEM((2 * n_heads, c, dk), BF16),
            pltpu.VMEM((2 * n_heads, c, dk), BF16),
        ],
        compiler_params=_params("parallel", "arbitrary"),
        name="retention_scan",
    )(decay_tile, qkv, qkv)


def _ret_gate_kernel(x_ref, wf_ref, wb_ref, of_ref, ob_ref, y_ref, wf_bf, wb_bf, *, n_split):
    _cast_weights_once([(wf_ref, wf_bf), (wb_ref, wb_bf)])
    for rows in _row_groups(y_ref, n_split):
        x = x_ref[rows, :]
        gf = jnp.dot(x, wf_bf[...], preferred_element_type=F32)
        gb = jnp.dot(x, wb_bf[...], preferred_element_type=F32)
        y = _silu(gf) * of_ref[rows, :].astype(F32) + _silu(gb) * ob_ref[rows, :].astype(F32)
        y_ref[rows, :] = y.astype(BF16)


def _ret_gate(u, w_g, layer, o_f, o_b, tm):
    t, d = u.shape
    n = w_g.shape[2] // 2
    tn = 512
    nb = n // tn
    return pl.pallas_call(
        functools.partial(_ret_gate_kernel, n_split=2),
        grid=(nb, t // tm),
        in_specs=[
            pl.BlockSpec((tm, d), lambda j, i: (i, 0)),
            pl.BlockSpec((None, d, tn), lambda j, i: (layer, 0, j)),
            pl.BlockSpec((None, d, tn), lambda j, i: (layer, 0, nb + j)),
            pl.BlockSpec((tm, tn), lambda j, i: (i, j)),
            pl.BlockSpec((tm, tn), lambda j, i: (i, j)),
        ],
        out_specs=pl.BlockSpec((tm, tn), lambda j, i: (i, j)),
        out_shape=jax.ShapeDtypeStruct((t, n), BF16),
        scratch_shapes=[pltpu.VMEM((d, tn), BF16), pltpu.VMEM((d, tn), BF16)],
        compiler_params=_params("parallel", "arbitrary"),
        name="ret_gate",
    )(u, w_g, w_g, o_f, o_b)


def _out_ln_kernel(x_ref, w_ref, h_ref, gate_ref, lng_ref, lnb_ref, sc_ref, sh_ref,
                   hout_ref, u_ref, *acc, alpha, n_k, n_split):
    k = pl.program_id(1)

    def partial_product(rows):
        return jnp.dot(x_ref[rows, :], w_ref[...], preferred_element_type=F32)

    def finish(rows, y):
        z = alpha * h_ref[rows, :] + gate_ref[0] * y
        mu = jnp.mean(z, axis=-1, keepdims=True)
        ctr = z - mu
        var = jnp.mean(ctr * ctr, axis=-1, keepdims=True)
        hn = ctr * lax.rsqrt(var + LN_EPS) * lng_ref[0] + lnb_ref[0]
        hout_ref[rows, :] = hn
        u_ref[rows, :] = (hn * (1.0 + sc_ref[0]) + sh_ref[0]).astype(BF16)

    def last_step(prev):
        for rows in _row_groups(x_ref, n_split):
            y = partial_product(rows)
            finish(rows, y if prev is None else prev[rows, :] + y)

    if n_k == 1:
        last_step(None)
        return
    acc_ref = acc[0].at[pl.program_id(2)]

    @pl.when(k == 0)
    def _():
        acc_ref[...] = partial_product(slice(None))

    @pl.when(jnp.logical_and(k > 0, k < n_k - 1))
    def _():
        acc_ref[...] += partial_product(slice(None))

    @pl.when(k == n_k - 1)
    def _():
        last_step(acc_ref)


def _out_ln(x, w, layer, h, mod, ln_g, ln_b, *, gate_idx, ln_idx, sc_idx, sh_idx, alpha, n_rows, tm, tk,
            n_split=2):
    kdim = x.shape[1]
    d = w.shape[2]
    n_k = kdim // tk
    pair = 2 if n_rows % (2 * tm) == 0 else 1
    assert n_rows % (pair * tm) == 0 and kdim % tk == 0

    def row_blk(i, k, m):
        return pair * i + jnp.where(k == n_k - 1, m, 0)

    vec = lambda arr_idx: pl.BlockSpec((1, 1, d), lambda i, k, m: (arr_idx(row_blk(i, k, m)), 0, 0))
    row = pl.BlockSpec((tm, d), lambda i, k, m: (row_blk(i, k, m), 0))
    kern = functools.partial(_out_ln_kernel, alpha=alpha, n_k=n_k, n_split=n_split)
    return pl.pallas_call(
        kern,
        grid=(n_rows // (pair * tm), n_k, pair),
        in_specs=[
            pl.BlockSpec((tm, tk), lambda i, k, m: (pair * i + m, k)),
            pl.BlockSpec((None, tk, d), lambda i, k, m: (layer, k, 0)),
            row,
            vec(gate_idx),
            vec(lambda r: ln_idx),
            vec(lambda r: ln_idx),
            vec(sc_idx),
            vec(sh_idx),
        ],
        out_specs=[row, row],
        out_shape=[
            jax.ShapeDtypeStruct((n_rows, d), F32),
            jax.ShapeDtypeStruct((n_rows, d), BF16),
        ],
        scratch_shapes=[pltpu.VMEM((pair, tm, d), F32)] if n_k > 1 else [],
        compiler_params=_params("parallel", "arbitrary", "arbitrary"),
        name="out_ln",
    )(x, w, h, mod, ln_g, ln_b, mod, mod)


def _ffn_in_kernel(x_ref, wa_ref, wb_ref, o_ref, wa_bf, wb_bf, *, n_split):
    _cast_weights_once([(wa_ref, wa_bf), (wb_ref, wb_bf)])
    for rows in _row_groups(o_ref, n_split):
        x = x_ref[rows, :]
        a = jnp.dot(x, wa_bf[...], preferred_element_type=F32)
        b = jnp.dot(x, wb_bf[...], preferred_element_type=F32)
        o_ref[rows, :] = (_silu(a) * b).astype(BF16)


def _ffn_in(u, w_in, layer, n_rows, tm):
    d = u.shape[1]
    f = w_in.shape[2] // 2
    tn = 512
    nb = f // tn
    return pl.pallas_call(
        functools.partial(_ffn_in_kernel, n_split=2),
        grid=(nb, n_rows // tm),
        in_specs=[
            pl.BlockSpec((tm, d), lambda j, i: (i, 0)),
            pl.BlockSpec((None, d, tn), lambda j, i: (layer, 0, j)),
            pl.BlockSpec((None, d, tn), lambda j, i: (layer, 0, nb + j)),
        ],
        out_specs=pl.BlockSpec((tm, tn), lambda j, i: (i, j)),
        out_shape=jax.ShapeDtypeStruct((n_rows, f), BF16),
        scratch_shapes=[pltpu.VMEM((d, tn), BF16), pltpu.VMEM((d, tn), BF16)],
        compiler_params=_params("parallel", "arbitrary"),
        name="ffn_in",
    )(u, w_in, w_in)


def _rope_lanes(x, c_ref, s_up_ref, s_dn_ref):
    half = MLA_D_ROPE // 4
    up = pltpu.roll(x, LANES - half, 1)
    dn = pltpu.roll(x, half, 1)
    return x * c_ref[...] + up * s_up_ref[...] + dn * s_dn_ref[...]


def _rms(x, g):
    return x * lax.rsqrt(jnp.mean(x * x, axis=-1, keepdims=True) + RMS_EPS) * g


def _mla_down_kernel(x_ref, w_ref, gq_ref, gkv_ref, c_ref, su_ref, sd_ref,
                     cq_ref, ckv_ref, kr_ref, *, rq, rkv):
    for rows in _row_groups(x_ref, 2):
        acc = jnp.dot(x_ref[rows, :], w_ref[...], preferred_element_type=F32)
        cq_ref[rows, :] = _rms(acc[:, :rq], gq_ref[...]).astype(BF16)
        ckv_ref[rows, :] = _rms(acc[:, rq:rq + rkv], gkv_ref[...]).astype(BF16)
        kr_ref[rows, :] = _rope_lanes(acc[:, rq + rkv:], c_ref.at[rows, :], su_ref.at[rows, :],
                                      sd_ref.at[rows, :]).astype(BF16)


def _mla_down(u, w_down, g_q, g_kv, tabs, tm):
    t, d = u.shape
    rq, rkv = g_q.shape[1], g_kv.shape[1]
    n = w_down.shape[1]
    tab = pl.BlockSpec((tm, LANES), lambda i: (i, 0))
    kern = functools.partial(_mla_down_kernel, rq=rq, rkv=rkv)
    return pl.pallas_call(
        kern,
        grid=(t // tm,),
        in_specs=[
            pl.BlockSpec((tm, d), lambda i: (i, 0)),
            pl.BlockSpec((d, n), lambda i: (0, 0)),
            pl.BlockSpec((1, rq), lambda i: (0, 0)),
            pl.BlockSpec((1, rkv), lambda i: (0, 0)),
            tab, tab, tab,
        ],
        out_specs=[
            pl.BlockSpec((tm, rq), lambda i: (i, 0)),
            pl.BlockSpec((tm, rkv), lambda i: (i, 0)),
            pl.BlockSpec((tm, LANES), lambda i: (i, 0)),
        ],
        out_shape=[
            jax.ShapeDtypeStruct((t, rq), BF16),
            jax.ShapeDtypeStruct((t, rkv), BF16),
            jax.ShapeDtypeStruct((t, LANES), BF16),
        ],
        compiler_params=_params("parallel"),
        name="mla_down",
    )(u, w_down, g_q, g_kv, *tabs)


def _mla_up_kernel(cq_ref, ckv_ref, kr_ref, wq_ref, wkv_ref, c_ref, su_ref, sd_ref,
                   q_ref, k_ref, v_ref):
    cq = cq_ref[...]
    ckv = ckv_ref[...]
    kr = kr_ref[...]
    hp = MLA_HEAD_PAD
    for h in range(MLA_HEADS):
        qh = jnp.dot(cq, wq_ref[:, h * hp:(h + 1) * hp], preferred_element_type=F32) * MLA_Q_SCALE
        q_ref[:, h * hp:h * hp + LANES] = qh[:, :LANES].astype(BF16)
        q_ref[:, h * hp + LANES:(h + 1) * hp] = _rope_lanes(qh[:, LANES:], c_ref, su_ref, sd_ref).astype(BF16)
        kvh = jnp.dot(ckv, wkv_ref[:, h * hp:(h + 1) * hp], preferred_element_type=F32)
        k_ref[:, h * hp:h * hp + LANES] = kvh[:, :LANES].astype(BF16)
        k_ref[:, h * hp + LANES:(h + 1) * hp] = kr
        v_ref[:, h * MLA_D_V:(h + 1) * MLA_D_V] = kvh[:, LANES:].astype(BF16)


def _mla_up(cq, ckv, kr, wq_pad, wkv, tabs, tm):
    t, rq = cq.shape
    rkv = ckv.shape[1]
    nq = wq_pad.shape[1]
    nkv = wkv.shape[1]
    tab = pl.BlockSpec((tm, LANES), lambda i: (i, 0))
    return pl.pallas_call(
        _mla_up_kernel,
        grid=(t // tm,),
        in_specs=[
            pl.BlockSpec((tm, rq), lambda i: (i, 0)),
            pl.BlockSpec((tm, rkv), lambda i: (i, 0)),
            pl.BlockSpec((tm, LANES), lambda i: (i, 0)),
            pl.BlockSpec((rq, nq), lambda i: (0, 0)),
            pl.BlockSpec((rkv, nkv), lambda i: (0, 0)),
            tab, tab, tab,
        ],
        out_specs=[
            pl.BlockSpec((tm, nq), lambda i: (i, 0)),
            pl.BlockSpec((tm, nq), lambda i: (i, 0)),
            pl.BlockSpec((tm, MLA_HEADS * MLA_D_V), lambda i: (i, 0)),
        ],
        out_shape=[
            jax.ShapeDtypeStruct((t, nq), BF16),
            jax.ShapeDtypeStruct((t, nq), BF16),
            jax.ShapeDtypeStruct((t, MLA_HEADS * MLA_D_V), BF16),
        ],
        compiler_params=_params("parallel"),
        name="mla_up",
    )(cq, ckv, kr, wq_pad, wkv, *tabs)


def _scores(q, k):
    return lax.dot_general(q, k, (((1,), (1,)), ((), ())), preferred_element_type=F32)


def _attn_lat_kernel(q_ref, kl_ref, kc_ref, vl_ref, vc_ref, o_ref, *, key_chunk):
    q = q_ref[...]
    n_lat = kl_ref.shape[0] // key_chunk
    chunks = [(kl_ref, vl_ref, c * key_chunk, key_chunk) for c in range(n_lat)]
    chunks.append((kc_ref, vc_ref, 0, kc_ref.shape[0]))

    def chunk_scores(c):
        k_ref, _, lo, size = chunks[c]
        return _scores(q, k_ref[lo:lo + size, :])

    s = chunk_scores(0)
    m = l = acc = None
    for c, (_, v_ref, lo, size) in enumerate(chunks):
        s_next = chunk_scores(c + 1) if c + 1 < len(chunks) else None
        m_c = jnp.max(s, axis=-1, keepdims=True)
        v = v_ref[lo:lo + size, :]
        if c == 0:
            m = m_c
            p = jnp.exp2(s - m)
            l = jnp.sum(p, axis=-1, keepdims=True)
            acc = jnp.dot(p.astype(BF16), v, preferred_element_type=F32)
        else:
            m_new = jnp.maximum(m, m_c)
            alpha = jnp.exp2(m - m_new)
            p = jnp.exp2(s - m_new)
            l = alpha * l + jnp.sum(p, axis=-1, keepdims=True)
            acc = alpha * acc + jnp.dot(p.astype(BF16), v, preferred_element_type=F32)
            m = m_new
        s = s_next
    o_ref[...] = (acc / l).astype(BF16)


def _attn_ctx_kernel(q_ref, kc_ref, vc_ref, o_ref):
    s_c = _scores(q_ref[...], kc_ref[...])
    p_c = jnp.exp2(s_c - jnp.max(s_c, axis=-1, keepdims=True))
    denom = jnp.sum(p_c, axis=-1, keepdims=True)
    o = jnp.dot(p_c.astype(BF16), vc_ref[...], preferred_element_type=F32)
    o_ref[...] = (o / denom).astype(BF16)


def _attention(q, k, v, batch, seq, ctx_len, with_ctx_queries):
    hp, dv = MLA_HEAD_PAD, MLA_D_V
    tq = min(seq, 1024)
    nq = seq // tq
    ctx0 = batch * seq // ctx_len
    o_lat = pl.pallas_call(
        functools.partial(_attn_lat_kernel, key_chunk=min(seq, 1024)),
        grid=(batch, MLA_HEADS, nq),
        in_specs=[
            pl.BlockSpec((tq, hp), lambda b, h, i: (b * nq + i, h)),
            pl.BlockSpec((seq, hp), lambda b, h, i: (b, h)),
            pl.BlockSpec((ctx_len, hp), lambda b, h, i: (ctx0 + b, h)),
            pl.BlockSpec((seq, dv), lambda b, h, i: (b, h)),
            pl.BlockSpec((ctx_len, dv), lambda b, h, i: (ctx0 + b, h)),
        ],
        out_specs=pl.BlockSpec((tq, dv), lambda b, h, i: (b * nq + i, h)),
        out_shape=jax.ShapeDtypeStruct((batch * seq, MLA_HEADS * dv), BF16),
        compiler_params=_params("parallel", "parallel", "arbitrary"),
        name="attn_lat",
    )(q, k, k, v, v)
    if not with_ctx_queries:
        return o_lat
    o_ctx = pl.pallas_call(
        _attn_ctx_kernel,
        grid=(batch, MLA_HEADS),
        in_specs=[
            pl.BlockSpec((ctx_len, hp), lambda b, h: (ctx0 + b, h)),
            pl.BlockSpec((ctx_len, hp), lambda b, h: (ctx0 + b, h)),
            pl.BlockSpec((ctx_len, dv), lambda b, h: (ctx0 + b, h)),
        ],
        out_specs=pl.BlockSpec((ctx_len, dv), lambda b, h: (b, h)),
        out_shape=jax.ShapeDtypeStruct((batch * ctx_len, MLA_HEADS * dv), BF16),
        compiler_params=_params("parallel", "parallel"),
        name="attn_ctx",
    )(q, k, v)
    return jnp.concatenate([o_lat, o_ctx], 0)


def _retention_tables(seq, n_ctx, dk):
    inv = RET_ROPE_BASE ** (-jnp.linspace(0.0, 1.0, dk // 2, dtype=F32))
    ang = jnp.arange(seq, dtype=F32)[:, None] * inv[None, :]
    return jnp.cos(ang), jnp.sin(ang), jnp.ones((n_ctx, dk // 2), F32), jnp.zeros((n_ctx, dk // 2), F32)


def _axial_tables(seq):
    n_f = MLA_D_ROPE // 4
    inv = AXIAL_ROPE_BASE ** (-jnp.arange(n_f, dtype=F32) * 2.0 / (MLA_D_ROPE // 2))
    tpos = jnp.arange(seq)
    row = (tpos // GRID_W).astype(F32)[:, None] * inv[None, :]
    col = (tpos % GRID_W).astype(F32)[:, None] * inv[None, :]
    z = jnp.zeros((seq, n_f), F32)
    tail = LANES - MLA_D_ROPE
    cos = jnp.concatenate([jnp.cos(row), jnp.cos(row), jnp.cos(col), jnp.cos(col), jnp.ones((seq, tail), F32)], 1)
    s_up = jnp.concatenate([-jnp.sin(row), z, -jnp.sin(col), z, jnp.zeros((seq, tail), F32)], 1)
    s_dn = jnp.concatenate([z, jnp.sin(row), z, jnp.sin(col), jnp.zeros((seq, tail), F32)], 1)
    return cos, s_up, s_dn


def _token_table(lat, batch, ctx_rows, fill):
    return jnp.concatenate([jnp.tile(lat, (batch, 1)), jnp.full((ctx_rows, lat.shape[1]), fill, F32)], 0)


def kernel(x, c, ctx, c_ctx, ada_w, ada_b, ln_g, ln_b, ret_w_qkv, ret_w_g, ret_decay_logit, ret_w_o,
           mla_w_dq, mla_g_q, mla_w_uq, mla_w_dkv, mla_g_kv, mla_w_ukv, mla_w_o, ffn_w_in, ffn_w_out):
    batch, seq, d = x.shape
    ctx_len = ctx.shape[1]
    depth = ada_w.shape[0]
    n_lat, n_ctx = batch * seq, batch * ctx_len
    n_tok = n_lat + n_ctx
    ret_heads = ret_decay_logit.shape[-1]
    assert ctx_len == RET_CHUNK and seq % RET_CHUNK == 0 and batch + 1 <= 8
    tm = _row_tile(n_lat, n_ctx)
    alpha = (2 * depth) ** 0.25

    cond8 = jnp.concatenate([c, c_ctx[None, :], jnp.zeros((8 - batch - 1, d), F32)], 0)
    mod = _modulation(cond8, ada_w, ada_b).reshape(depth * 8 * 6, 1, d)

    def mod_idx(layer, chunk, tile):
        def f(i):
            r = jnp.where(i * tile >= n_lat, batch, (i * tile) // seq)
            return (layer * 8 + r) * 6 + chunk
        return f

    ln_g3 = ln_g.reshape(depth * 2, 1, d)
    ln_b3 = ln_b.reshape(depth * 2, 1, d)

    ret_cos, ret_sin, one_c, zero_c = _retention_tables(seq, n_ctx, d // ret_heads)
    ret_cos = jnp.concatenate([jnp.tile(ret_cos, (batch, 1)), one_c], 0)
    ret_sin = jnp.concatenate([jnp.tile(ret_sin, (batch, 1)), zero_c], 0)
    ax_cos, ax_up, ax_dn = _axial_tables(seq)
    ax_tabs = (_token_table(ax_cos, batch, n_ctx, 1.0), _token_table(ax_up, batch, n_ctx, 0.0),
               _token_table(ax_dn, batch, n_ctx, 0.0))

    tm_ln = 512
    h, u = _modulate(x.reshape(n_lat, d), ctx.reshape(n_ctx, d), mod,
                     mod_idx(0, 1, tm_ln), mod_idx(0, 0, tm_ln), tm_ln)

    ret_w_o_b, mla_w_o_b, ffn_w_out_b = (w.astype(BF16) for w in (ret_w_o, mla_w_o, ffn_w_out))

    for i in range(depth):
        last = i == depth - 1
        n_rows = n_lat if last else n_tok
        j = i // 2
        if i % 2 == 0:
            qkv = _ret_qkv(u, ret_w_qkv, j, ret_cos, ret_sin, ret_heads, tm)
            decay_tile = jnp.broadcast_to(
                ret_decay_logit[j].astype(F32).reshape(2 * ret_heads, 1, 1), (2 * ret_heads, 8, LANES))
            o_f, o_b = _retention(qkv, decay_tile, batch, seq, ret_heads)
            y = _ret_gate(u, ret_w_g, j, o_f, o_b, tm)
            w_o = ret_w_o_b
        else:
            rq, rkv = mla_g_q.shape[1], mla_g_kv.shape[1]
            w_down = jnp.concatenate(
                [mla_w_dq[j], mla_w_dkv[j], jnp.zeros((d, LANES - MLA_D_ROPE), F32)], 1).astype(BF16)
            cq, ckv, kr = _mla_down(u, w_down, mla_g_q[j][None, :], mla_g_kv[j][None, :], ax_tabs, 512)
            wq_pad = jnp.pad(
                mla_w_uq[j].reshape(rq, MLA_HEADS, MLA_D_NOPE + MLA_D_ROPE),
                ((0, 0), (0, 0), (0, MLA_HEAD_PAD - MLA_D_NOPE - MLA_D_ROPE)),
            ).reshape(rq, MLA_HEADS * MLA_HEAD_PAD).astype(BF16)
            q, k, v = _mla_up(cq, ckv, kr, wq_pad, mla_w_ukv[j].astype(BF16), ax_tabs, 512)
            y = _attention(q, k, v, batch, seq, ctx_len, not last)
            w_o = mla_w_o_b

        h, u = _out_ln(y, w_o, j, h, mod, ln_g3, ln_b3, gate_idx=mod_idx(i, 2, tm_ln), ln_idx=2 * i,
                       sc_idx=mod_idx(i, 4, tm_ln), sh_idx=mod_idx(i, 3, tm_ln), alpha=alpha,
                       n_rows=n_rows, tm=tm_ln, tk=2048)
        hid = _ffn_in(u, ffn_w_in, i, n_rows, tm)
        nxt = min(i + 1, depth - 1)
        h, u = _out_ln(hid, ffn_w_out_b, i, h, mod, ln_g3, ln_b3, gate_idx=mod_idx(i, 5, tm_ln),
                       ln_idx=2 * i + 1, sc_idx=mod_idx(nxt, 1, tm_ln), sh_idx=mod_idx(nxt, 0, tm_ln),
                       alpha=alpha, n_rows=n_rows, tm=tm_ln, tk=ffn_w_out.shape[1] // 4)
    return h.reshape(batch, seq, d)
```

```python
import functools

import jax
import jax.numpy as jnp
from jax import lax
from jax.experimental import pallas as pl
from jax.experimental.pallas import tpu as pltpu

F32 = jnp.float32
BF16 = jnp.bfloat16

GRID_W = 64
RET_ROPE_BASE = 10000.0
GN_EPS = 1e-6
MLA_HEADS = 16
MLA_D_NOPE = 128
MLA_D_ROPE = 64
MLA_D_V = 128
MLA_SCALE = (MLA_D_NOPE + MLA_D_ROPE) ** -0.5
MLA_Q_SCALE = MLA_SCALE * 1.4426950408889634
AXIAL_ROPE_BASE = 10000.0
RMS_EPS = 1e-6
LN_EPS = 1e-5

LANES = 128
MXU_DIM = 256
VMEM_LIMIT_BYTES = 56 * 1024 * 1024

RET_CHUNK = 256
MLA_HEAD_PAD = 2 * LANES


def _params(*sem):
    return pltpu.CompilerParams(dimension_semantics=sem, vmem_limit_bytes=VMEM_LIMIT_BYTES)


def _silu(x):
    return 0.5 * x * (1.0 + jnp.tanh(0.5 * x))


def _row_tile(n_lat, n_ctx):
    for t in (1024, 512, 256):
        if n_lat % t == 0 and n_ctx % t == 0:
            return t
    raise ValueError("token counts must be multiples of 256")


def _mod_kernel(cond_ref, w_ref, b_ref, o_ref):
    s = _silu(cond_ref[...]).astype(BF16)
    acc = jnp.dot(s, w_ref[0].astype(BF16), preferred_element_type=F32)
    o_ref[0] = acc + b_ref[0]


def _modulation(cond8, ada_w, ada_b):
    depth, d, n = ada_w.shape
    tn = 1024
    return pl.pallas_call(
        _mod_kernel,
        grid=(depth, n // tn),
        in_specs=[
            pl.BlockSpec((8, d), lambda l, j: (0, 0)),
            pl.BlockSpec((1, d, tn), lambda l, j: (l, 0, j)),
            pl.BlockSpec((1, 1, tn), lambda l, j: (l, 0, j)),
        ],
        out_specs=pl.BlockSpec((1, 8, tn), lambda l, j: (l, 0, j)),
        out_shape=jax.ShapeDtypeStruct((depth, 8, n), F32),
        compiler_params=_params("parallel", "parallel"),
        name="adaln_mod",
    )(cond8, ada_w, ada_b.reshape(depth, 1, n))


def _modulate_kernel(x_ref, ctx_ref, sc_ref, sh_ref, h_ref, u_ref, *, n_lat_tiles):
    def emit(src_ref):
        hv = src_ref[...]
        h_ref[...] = hv
        u_ref[...] = (hv * (1.0 + sc_ref[0]) + sh_ref[0]).astype(BF16)

    is_lat = pl.program_id(0) < n_lat_tiles
    pl.when(is_lat)(lambda: emit(x_ref))
    pl.when(jnp.logical_not(is_lat))(lambda: emit(ctx_ref))


def _modulate(x2d, ctx2d, mod, sc_idx, sh_idx, tm):
    n_lat, d = x2d.shape
    n_ctx = ctx2d.shape[0]
    t = n_lat + n_ctx
    n_lat_tiles = n_lat // tm
    vec = lambda f: pl.BlockSpec((1, 1, d), lambda i: (f(i), 0, 0))
    row = pl.BlockSpec((tm, d), lambda i: (i, 0))
    return pl.pallas_call(
        functools.partial(_modulate_kernel, n_lat_tiles=n_lat_tiles),
        grid=(t // tm,),
        in_specs=[
            pl.BlockSpec((tm, d), lambda i: (jnp.minimum(i, n_lat_tiles - 1), 0)),
            pl.BlockSpec((tm, d), lambda i: (jnp.maximum(i - n_lat_tiles, 0), 0)),
            vec(sc_idx), vec(sh_idx),
        ],
        out_specs=[row, row],
        out_shape=[jax.ShapeDtypeStruct((t, d), F32), jax.ShapeDtypeStruct((t, d), BF16)],
        compiler_params=_params("arbitrary"),
        name="modulate_in",
    )(x2d, ctx2d, mod, mod)


def _cast_weights_once(pairs):
    @pl.when(pl.program_id(1) == 0)
    def _():
        for src_ref, dst_ref in pairs:
            dst_ref[...] = src_ref[...].astype(BF16)


def _row_groups(ref, n_split):
    rows_per = ref.shape[0] // n_split
    return [pl.ds(r * rows_per, rows_per) for r in range(n_split)]


def _ret_qkv_kernel(x_ref, w_ref, cos_ref, sin_ref, o_ref, wb_ref, *, n_q, n_k, k_scale, n_split):
    j = pl.program_id(0)
    tn = o_ref.shape[1]
    _cast_weights_once([(w_ref, wb_ref)])

    def product(rows):
        return jnp.dot(x_ref[rows, :], wb_ref[...], preferred_element_type=F32)

    @pl.when(j < n_q + n_k)
    def _():
        scale = jnp.where(j < n_q, 1.0, k_scale).astype(F32)
        for rows in _row_groups(o_ref, n_split):
            acc = product(rows)
            c = cos_ref[rows, :]
            s = sin_ref[rows, :]
            for g in range(tn // MXU_DIM):
                lo = g * MXU_DIM
                x1 = acc[:, lo:lo + LANES] * scale
                x2 = acc[:, lo + LANES:lo + MXU_DIM] * scale
                o_ref[rows, lo:lo + LANES] = (x1 * c - x2 * s).astype(BF16)
                o_ref[rows, lo + LANES:lo + MXU_DIM] = (x1 * s + x2 * c).astype(BF16)

    @pl.when(j >= n_q + n_k)
    def _():
        for rows in _row_groups(o_ref, n_split):
            o_ref[rows, :] = product(rows).astype(BF16)


def _ret_qkv(u, w, layer, cos, sin, n_heads, tm):
    t, d = u.shape
    n = w.shape[2]
    dk = d // n_heads
    tn = 1024
    n_q = n_heads * dk // tn
    kern = functools.partial(_ret_qkv_kernel, n_q=n_q, n_k=n_q, k_scale=dk ** -0.5, n_split=2)
    return pl.pallas_call(
        kern,
        grid=(n // tn, t // tm),
        in_specs=[
            pl.BlockSpec((tm, d), lambda j, i: (i, 0)),
            pl.BlockSpec((None, d, tn), lambda j, i: (layer, 0, j)),
            pl.BlockSpec((tm, LANES), lambda j, i: (i, 0)),
            pl.BlockSpec((tm, LANES), lambda j, i: (i, 0)),
        ],
        out_specs=pl.BlockSpec((tm, tn), lambda j, i: (i, j)),
        out_shape=jax.ShapeDtypeStruct((t, n), BF16),
        scratch_shapes=[pltpu.VMEM((d, tn), BF16)],
        compiler_params=_params("parallel", "arbitrary"),
        name="ret_qkv",
    )(u, w, cos, sin)


def _retention_kernel(dec_ref, qf_ref, qb_ref, of_ref, ob_ref, s_ref, intra_ref, qdec_ref, kdec_ref,
                      *, n_heads, dk, dv):
    c = RET_CHUNK
    step = pl.program_id(1)

    def log_gamma(idx):
        x = dec_ref[idx][0:1, 0:1]
        return jnp.minimum(x, 0.0) - jnp.log1p(jnp.exp(-jnp.abs(x)))

    @pl.when(step == 0)
    def _():
        s_ref[...] = jnp.zeros_like(s_ref)
        row = lax.broadcasted_iota(jnp.int32, (c, c), 0)
        col = lax.broadcasted_iota(jnp.int32, (c, c), 1)
        pos = lax.broadcasted_iota(jnp.int32, (c, dk), 0).astype(F32)
        for d in range(2):
            diff = (row - col) if d == 0 else (col - row)
            dist = jnp.maximum(diff, 0).astype(F32)
            ahead = pos + 1.0 if d == 0 else float(c) - pos
            behind = float(c - 1) - pos if d == 0 else pos
            for h in range(n_heads):
                idx = d * n_heads + h
                lg = log_gamma(idx)
                intra_ref[idx] = jnp.where(diff >= 0, jnp.exp(lg * dist), 0.0)
                qdec_ref[idx] = jnp.exp(lg * ahead).astype(BF16)
                kdec_ref[idx] = jnp.exp(lg * behind).astype(BF16)

    for d, (x_ref, o_ref) in enumerate(((qf_ref, of_ref), (qb_ref, ob_ref))):
        for h in range(n_heads):
            idx = d * n_heads + h
            q = x_ref[:, h * dk:(h + 1) * dk]
            k = x_ref[:, n_heads * dk + h * dk:n_heads * dk + (h + 1) * dk]
            v = x_ref[:, 2 * n_heads * dk + h * dv:2 * n_heads * dk + (h + 1) * dv]
            scores = lax.dot_general(q, k, (((1,), (1,)), ((), ())), preferred_element_type=F32)
            p = (scores * intra_ref[idx]).astype(BF16)
            state = s_ref[idx]
            o = (jnp.dot(p, v, preferred_element_type=F32)
                 + jnp.dot(q * qdec_ref[idx], state.astype(BF16), preferred_element_type=F32))
            s_ref[idx] = state * jnp.exp(log_gamma(idx) * float(c)) + lax.dot_general(
                k * kdec_ref[idx], v, (((0,), (0,)), ((), ())), preferred_element_type=F32)
            mu = jnp.mean(o, axis=-1, keepdims=True)
            ctr = o - mu
            var = jnp.mean(ctr * ctr, axis=-1, keepdims=True)
            o_ref[:, h * dv:(h + 1) * dv] = (ctr * lax.rsqrt(var + GN_EPS)).astype(BF16)


def _retention(qkv, decay_tile, batch, seq, n_heads):
    t, n = qkv.shape
    d = n // 4
    dk, dv = d // n_heads, 2 * d // n_heads
    c = RET_CHUNK
    n_lat = seq // c
    ctx_blk = batch * n_lat

    def fwd_blk(b, s):
        return jnp.where(s == 0, ctx_blk + b, b * n_lat + s - 1)

    def bwd_blk(b, s):
        return jnp.where(s == 0, ctx_blk + b, b * n_lat + n_lat - s)

    kern = functools.partial(_retention_kernel, n_heads=n_heads, dk=dk, dv=dv)
    out = jax.ShapeDtypeStruct((t, n_heads * dv), BF16)
    return pl.pallas_call(
        kern,
        grid=(batch, n_lat + 1),
        in_specs=[
            pl.BlockSpec((2 * n_heads, 8, LANES), lambda b, s: (0, 0, 0)),
            pl.BlockSpec((c, n), lambda b, s: (fwd_blk(b, s), 0)),
            pl.BlockSpec((c, n), lambda b, s: (bwd_blk(b, s), 0)),
        ],
        out_specs=[
            pl.BlockSpec((c, n_heads * dv), lambda b, s: (fwd_blk(b, s), 0)),
            pl.BlockSpec((c, n_heads * dv), lambda b, s: (bwd_blk(b, s), 0)),
        ],
        out_shape=[out, out],
        scratch_shapes=[
            pltpu.VMEM((2 * n_heads, dk, dv), F32),
            pltpu.VMEM((2 * n_heads, c, c), F32),
            pltpu.VMEM((2 * n_heads, c, dk), BF16),
            pltpu.VMEM((2 * n_heads, c, dk), BF16),
        ],
        compiler_params=_params("parallel", "arbitrary"),
        name="retention_scan",
    )(decay_tile, qkv, qkv)


def _ret_gate_kernel(x_ref, wf_ref, wb_ref, of_ref, ob_ref, y_ref, wf_bf, wb_bf, *, n_split):
    _cast_weights_once([(wf_ref, wf_bf), (wb_ref, wb_bf)])
    for rows in _row_groups(y_ref, n_split):
        x = x_ref[rows, :]
        gf = jnp.dot(x, wf_bf[...], preferred_element_type=F32)
        gb = jnp.dot(x, wb_bf[...], preferred_element_type=F32)
        y = _silu(gf) * of_ref[rows, :].astype(F32) + _silu(gb) * ob_ref[rows, :].astype(F32)
        y_ref[rows, :] = y.astype(BF16)


def _ret_gate(u, w_g, layer, o_f, o_b, tm):
    t, d = u.shape
    n = w_g.shape[2] // 2
    tn = 512
    nb = n // tn
    return pl.pallas_call(
        functools.partial(_ret_gate_kernel, n_split=2),
        grid=(nb, t // tm),
        in_specs=[
            pl.BlockSpec((tm, d), lambda j, i: (i, 0)),
            pl.BlockSpec((None, d, tn), lambda j, i: (layer, 0, j)),
            pl.BlockSpec((None, d, tn), lambda j, i: (layer, 0, nb + j)),
            pl.BlockSpec((tm, tn), lambda j, i: (i, j)),
            pl.BlockSpec((tm, tn), lambda j, i: (i, j)),
        ],
        out_specs=pl.BlockSpec((tm, tn), lambda j, i: (i, j)),
        out_shape=jax.ShapeDtypeStruct((t, n), BF16),
        scratch_shapes=[pltpu.VMEM((d, tn), BF16), pltpu.VMEM((d, tn), BF16)],
        compiler_params=_params("parallel", "arbitrary"),
        name="ret_gate",
    )(u, w_g, w_g, o_f, o_b)


def _out_ln_kernel(x_ref, w_ref, h_ref, gate_ref, lng_ref, lnb_ref, sc_ref, sh_ref,
                   hout_ref, u_ref, *acc, alpha, n_k, n_split):
    k = pl.program_id(1)

    def partial_product(rows):
        return jnp.dot(x_ref[rows, :], w_ref[...], preferred_element_type=F32)

    def finish(rows, y):
        z = alpha * h_ref[rows, :] + gate_ref[0] * y
        mu = jnp.mean(z, axis=-1, keepdims=True)
        ctr = z - mu
        var = jnp.mean(ctr * ctr, axis=-1, keepdims=True)
        hn = ctr * lax.rsqrt(var + LN_EPS) * lng_ref[0] + lnb_ref[0]
        hout_ref[rows, :] = hn
        u_ref[rows, :] = (hn * (1.0 + sc_ref[0]) + sh_ref[0]).astype(BF16)

    def last_step(prev):
        for rows in _row_groups(x_ref, n_split):
            y = partial_product(rows)
            finish(rows, y if prev is None else prev[rows, :] + y)

    if n_k == 1:
        last_step(None)
        return
    acc_ref = acc[0].at[pl.program_id(2)]

    @pl.when(k == 0)
    def _():
        acc_ref[...] = partial_product(slice(None))

    @pl.when(jnp.logical_and(k > 0, k < n_k - 1))
    def _():
        acc_ref[...] += partial_product(slice(None))

    @pl.when(k == n_k - 1)
    def _():
        last_step(acc_ref)


def _out_ln(x, w, layer, h, mod, ln_g, ln_b, *, gate_idx, ln_idx, sc_idx, sh_idx, alpha, n_rows, tm, tk,
            n_split=2):
    kdim = x.shape[1]
    d = w.shape[2]
    n_k = kdim // tk
    pair = 2 if n_rows % (2 * tm) == 0 else 1
    assert n_rows % (pair * tm) == 0 and kdim % tk == 0

    def row_blk(i, k, m):
        return pair * i + jnp.where(k == n_k - 1, m, 0)

    vec = lambda arr_idx: pl.BlockSpec((1, 1, d), lambda i, k, m: (arr_idx(row_blk(i, k, m)), 0, 0))
    row = pl.BlockSpec((tm, d), lambda i, k, m: (row_blk(i, k, m), 0))
    kern = functools.partial(_out_ln_kernel, alpha=alpha, n_k=n_k, n_split=n_split)
    return pl.pallas_call(
        kern,
        grid=(n_rows // (pair * tm), n_k, pair),
        in_specs=[
            pl.BlockSpec((tm, tk), lambda i, k, m: (pair * i + m, k)),
            pl.BlockSpec((None, tk, d), lambda i, k, m: (layer, k, 0)),
            row,
            vec(gate_idx),
            vec(lambda r: ln_idx),
            vec(lambda r: ln_idx),
            vec(sc_idx),
            vec(sh_idx),
        ],
        out_specs=[row, row],
        out_shape=[
            jax.ShapeDtypeStruct((n_rows, d), F32),
            jax.ShapeDtypeStruct((n_rows, d), BF16),
        ],
        scratch_shapes=[pltpu.VMEM((pair, tm, d), F32)] if n_k > 1 else [],
        compiler_params=_params("parallel", "arbitrary", "arbitrary"),
        name="out_ln",
    )(x, w, h, mod, ln_g, ln_b, mod, mod)


def _ffn_in_kernel(x_ref, wa_ref, wb_ref, o_ref, wa_bf, wb_bf, *, n_split):
    _cast_weights_once([(wa_ref, wa_bf), (wb_ref, wb_bf)])
    for rows in _row_groups(o_ref, n_split):
        x = x_ref[rows, :]
        a = jnp.dot(x, wa_bf[...], preferred_element_type=F32)
        b = jnp.dot(x, wb_bf[...], preferred_element_type=F32)
        o_ref[rows, :] = (_silu(a) * b).astype(BF16)


def _ffn_in(u, w_in, layer, n_rows, tm):
    d = u.shape[1]
    f = w_in.shape[2] // 2
    tn = 512
    nb = f // tn
    return pl.pallas_call(
        functools.partial(_ffn_in_kernel, n_split=2),
        grid=(nb, n_rows // tm),
        in_specs=[
            pl.BlockSpec((tm, d), lambda j, i: (i, 0)),
            pl.BlockSpec((None, d, tn), lambda j, i: (layer, 0, j)),
            pl.BlockSpec((None, d, tn), lambda j, i: (layer, 0, nb + j)),
        ],
        out_specs=pl.BlockSpec((tm, tn), lambda j, i: (i, j)),
        out_shape=jax.ShapeDtypeStruct((n_rows, f), BF16),
        scratch_shapes=[pltpu.VMEM((d, tn), BF16), pltpu.VMEM((d, tn), BF16)],
        compiler_params=_params("parallel", "arbitrary"),
        name="ffn_in",
    )(u, w_in, w_in)


def _rope_lanes(x, c_ref, s_up_ref, s_dn_ref):
    half = MLA_D_ROPE // 4
    up = pltpu.roll(x, LANES - half, 1)
    dn = pltpu.roll(x, half, 1)
    return x * c_ref[...] + up * s_up_ref[...] + dn * s_dn_ref[...]


def _rms(x, g):
    return x * lax.rsqrt(jnp.mean(x * x, axis=-1, keepdims=True) + RMS_EPS) * g


def _mla_down_kernel(x_ref, w_ref, gq_ref, gkv_ref, c_ref, su_ref, sd_ref,
                     cq_ref, ckv_ref, kr_ref, *, rq, rkv):
    for rows in _row_groups(x_ref, 2):
        acc = jnp.dot(x_ref[rows, :], w_ref[...], preferred_element_type=F32)
        cq_ref[rows, :] = _rms(acc[:, :rq], gq_ref[...]).astype(BF16)
        ckv_ref[rows, :] = _rms(acc[:, rq:rq + rkv], gkv_ref[...]).astype(BF16)
        kr_ref[rows, :] = _rope_lanes(acc[:, rq + rkv:], c_ref.at[rows, :], su_ref.at[rows, :],
                                      sd_ref.at[rows, :]).astype(BF16)


def _mla_down(u, w_down, g_q, g_kv, tabs, tm):
    t, d = u.shape
    rq, rkv = g_q.shape[1], g_kv.shape[1]
    n = w_down.shape[1]
    tab = pl.BlockSpec((tm, LANES), lambda i: (i, 0))
    kern = functools.partial(_mla_down_kernel, rq=rq, rkv=rkv)
    return pl.pallas_call(
        kern,
        grid=(t // tm,),
        in_specs=[
            pl.BlockSpec((tm, d), lambda i: (i, 0)),
            pl.BlockSpec((d, n), lambda i: (0, 0)),
            pl.BlockSpec((1, rq), lambda i: (0, 0)),
            pl.BlockSpec((1, rkv), lambda i: (0, 0)),
            tab, tab, tab,
        ],
        out_specs=[
            pl.BlockSpec((tm, rq), lambda i: (i, 0)),
            pl.BlockSpec((tm, rkv), lambda i: (i, 0)),
            pl.BlockSpec((tm, LANES), lambda i: (i, 0)),
        ],
        out_shape=[
            jax.ShapeDtypeStruct((t, rq), BF16),
            jax.ShapeDtypeStruct((t, rkv), BF16),
            jax.ShapeDtypeStruct((t, LANES), BF16),
        ],
        compiler_params=_params("parallel"),
        name="mla_down",
    )(u, w_down, g_q, g_kv, *tabs)


def _mla_up_kernel(cq_ref, ckv_ref, kr_ref, wq_ref, wkv_ref, c_ref, su_ref, sd_ref,
                   q_ref, k_ref, v_ref):
    cq = cq_ref[...]
    ckv = ckv_ref[...]
    kr = kr_ref[...]
    hp = MLA_HEAD_PAD
    for h in range(MLA_HEADS):
        qh = jnp.dot(cq, wq_ref[:, h * hp:(h + 1) * hp], preferred_element_type=F32) * MLA_Q_SCALE
        q_ref[:, h * hp:h * hp + LANES] = qh[:, :LANES].astype(BF16)
        q_ref[:, h * hp + LANES:(h + 1) * hp] = _rope_lanes(qh[:, LANES:], c_ref, su_ref, sd_ref).astype(BF16)
        kvh = jnp.dot(ckv, wkv_ref[:, h * hp:(h + 1) * hp], preferred_element_type=F32)
        k_ref[:, h * hp:h * hp + LANES] = kvh[:, :LANES].astype(BF16)
        k_ref[:, h * hp + LANES:(h + 1) * hp] = kr
        v_ref[:, h * hp:h * hp + MLA_D_V] = kvh[:, LANES:].astype(BF16)
        v_ref[:, h * hp + MLA_D_V:(h + 1) * hp] = jnp.ones((kvh.shape[0], hp - MLA_D_V), BF16)


def _mla_up(cq, ckv, kr, wq_pad, wkv, tabs, tm):
    t, rq = cq.shape
    rkv = ckv.shape[1]
    nq = wq_pad.shape[1]
    nkv = wkv.shape[1]
    tab = pl.BlockSpec((tm, LANES), lambda i: (i, 0))
    return pl.pallas_call(
        _mla_up_kernel,
        grid=(t // tm,),
        in_specs=[
            pl.BlockSpec((tm, rq), lambda i: (i, 0)),
            pl.BlockSpec((tm, rkv), lambda i: (i, 0)),
            pl.BlockSpec((tm, LANES), lambda i: (i, 0)),
            pl.BlockSpec((rq, nq), lambda i: (0, 0)),
            pl.BlockSpec((rkv, nkv), lambda i: (0, 0)),
            tab, tab, tab,
        ],
        out_specs=[
            pl.BlockSpec((tm, nq), lambda i: (i, 0)),
            pl.BlockSpec((tm, nq), lambda i: (i, 0)),
            pl.BlockSpec((tm, nq), lambda i: (i, 0)),
        ],
        out_shape=[
            jax.ShapeDtypeStruct((t, nq), BF16),
            jax.ShapeDtypeStruct((t, nq), BF16),
            jax.ShapeDtypeStruct((t, nq), BF16),
        ],
        compiler_params=_params("parallel"),
        name="mla_up",
    )(cq, ckv, kr, wq_pad, wkv, *tabs)


def _scores(q, k):
    return lax.dot_general(q, k, (((1,), (1,)), ((), ())), preferred_element_type=F32)


def _attn_lat_kernel(q_ref, kl_ref, kc_ref, vl_ref, vc_ref, o_ref, *, key_chunk):
    q = q_ref[...]
    n_lat = kl_ref.shape[0] // key_chunk
    chunks = [(kl_ref, vl_ref, c * key_chunk, key_chunk) for c in range(n_lat)]
    chunks.append((kc_ref, vc_ref, 0, kc_ref.shape[0]))

    def chunk_scores(c):
        k_ref, _, lo, size = chunks[c]
        return _scores(q, k_ref[lo:lo + size, :])

    s = chunk_scores(0)
    m = acc = None
    for c, (_, v_ref, lo, size) in enumerate(chunks):
        s_next = chunk_scores(c + 1) if c + 1 < len(chunks) else None
        m_c = jnp.max(s, axis=-1, keepdims=True)
        v = v_ref[lo:lo + size, :]
        if c == 0:
            m = m_c
            acc = jnp.dot(jnp.exp2(s - m).astype(BF16), v, preferred_element_type=F32)
        else:
            m_new = jnp.maximum(m, m_c)
            acc = (jnp.exp2(m - m_new) * acc
                   + jnp.dot(jnp.exp2(s - m_new).astype(BF16), v, preferred_element_type=F32))
            m = m_new
        s = s_next
    _store_normalised(o_ref, acc)


def _store_normalised(o_ref, acc):
    o_ref[...] = (acc[:, :MLA_D_V] / acc[:, MLA_D_V:MLA_D_V + 1]).astype(BF16)


def _attn_ctx_kernel(q_ref, kc_ref, vc_ref, o_ref):
    s_c = _scores(q_ref[...], kc_ref[...])
    p_c = jnp.exp2(s_c - jnp.max(s_c, axis=-1, keepdims=True))
    _store_normalised(o_ref, jnp.dot(p_c.astype(BF16), vc_ref[...], preferred_element_type=F32))


def _attention(q, k, v, batch, seq, ctx_len, with_ctx_queries):
    hp, dv = MLA_HEAD_PAD, MLA_D_V
    tq = min(seq, 1024)
    nq = seq // tq
    ctx0 = batch * seq // ctx_len
    o_lat = pl.pallas_call(
        functools.partial(_attn_lat_kernel, key_chunk=min(seq, 1024)),
        grid=(batch, MLA_HEADS, nq),
        in_specs=[
            pl.BlockSpec((tq, hp), lambda b, h, i: (b * nq + i, h)),
            pl.BlockSpec((seq, hp), lambda b, h, i: (b, h)),
            pl.BlockSpec((ctx_len, hp), lambda b, h, i: (ctx0 + b, h)),
            pl.BlockSpec((seq, hp), lambda b, h, i: (b, h)),
            pl.BlockSpec((ctx_len, hp), lambda b, h, i: (ctx0 + b, h)),
        ],
        out_specs=pl.BlockSpec((tq, dv), lambda b, h, i: (b * nq + i, h)),
        out_shape=jax.ShapeDtypeStruct((batch * seq, MLA_HEADS * dv), BF16),
        compiler_params=_params("parallel", "parallel", "arbitrary"),
        name="attn_lat",
    )(q, k, k, v, v)
    if not with_ctx_queries:
        return o_lat
    o_ctx = pl.pallas_call(
        _attn_ctx_kernel,
        grid=(batch, MLA_HEADS),
        in_specs=[
            pl.BlockSpec((ctx_len, hp), lambda b, h: (ctx0 + b, h)),
            pl.BlockSpec((ctx_len, hp), lambda b, h: (ctx0 + b, h)),
            pl.BlockSpec((ctx_len, hp), lambda b, h: (ctx0 + b, h)),
        ],
        out_specs=pl.BlockSpec((ctx_len, dv), lambda b, h: (b, h)),
        out_shape=jax.ShapeDtypeStruct((batch * ctx_len, MLA_HEADS * dv), BF16),
        compiler_params=_params("parallel", "parallel"),
        name="attn_ctx",
    )(q, k, v)
    return jnp.concatenate([o_lat, o_ctx], 0)


def _retention_tables(seq, n_ctx, dk):
    inv = RET_ROPE_BASE ** (-jnp.linspace(0.0, 1.0, dk // 2, dtype=F32))
    ang = jnp.arange(seq, dtype=F32)[:, None] * inv[None, :]
    return jnp.cos(ang), jnp.sin(ang), jnp.ones((n_ctx, dk // 2), F32), jnp.zeros((n_ctx, dk // 2), F32)


def _axial_tables(seq):
    n_f = MLA_D_ROPE // 4
    inv = AXIAL_ROPE_BASE ** (-jnp.arange(n_f, dtype=F32) * 2.0 / (MLA_D_ROPE // 2))
    tpos = jnp.arange(seq)
    row = (tpos // GRID_W).astype(F32)[:, None] * inv[None, :]
    col = (tpos % GRID_W).astype(F32)[:, None] * inv[None, :]
    z = jnp.zeros((seq, n_f), F32)
    tail = LANES - MLA_D_ROPE
    cos = jnp.concatenate([jnp.cos(row), jnp.cos(row), jnp.cos(col), jnp.cos(col), jnp.ones((seq, tail), F32)], 1)
    s_up = jnp.concatenate([-jnp.sin(row), z, -jnp.sin(col), z, jnp.zeros((seq, tail), F32)], 1)
    s_dn = jnp.concatenate([z, jnp.sin(row), z, jnp.sin(col), jnp.zeros((seq, tail), F32)], 1)
    return cos, s_up, s_dn


def _token_table(lat, batch, ctx_rows, fill):
    return jnp.concatenate([jnp.tile(lat, (batch, 1)), jnp.full((ctx_rows, lat.shape[1]), fill, F32)], 0)


def kernel(x, c, ctx, c_ctx, ada_w, ada_b, ln_g, ln_b, ret_w_qkv, ret_w_g, ret_decay_logit, ret_w_o,
           mla_w_dq, mla_g_q, mla_w_uq, mla_w_dkv, mla_g_kv, mla_w_ukv, mla_w_o, ffn_w_in, ffn_w_out):
    batch, seq, d = x.shape
    ctx_len = ctx.shape[1]
    depth = ada_w.shape[0]
    n_lat, n_ctx = batch * seq, batch * ctx_len
    n_tok = n_lat + n_ctx
    ret_heads = ret_decay_logit.shape[-1]
    assert ctx_len == RET_CHUNK and seq % RET_CHUNK == 0 and batch + 1 <= 8
    tm = _row_tile(n_lat, n_ctx)
    alpha = (2 * depth) ** 0.25

    cond8 = jnp.concatenate([c, c_ctx[None, :], jnp.zeros((8 - batch - 1, d), F32)], 0)
    mod = _modulation(cond8, ada_w, ada_b).reshape(depth * 8 * 6, 1, d)

    def mod_idx(layer, chunk, tile):
        def f(i):
            r = jnp.where(i * tile >= n_lat, batch, (i * tile) // seq)
            return (layer * 8 + r) * 6 + chunk
        return f

    ln_g3 = ln_g.reshape(depth * 2, 1, d)
    ln_b3 = ln_b.reshape(depth * 2, 1, d)

    ret_cos, ret_sin, one_c, zero_c = _retention_tables(seq, n_ctx, d // ret_heads)
    ret_cos = jnp.concatenate([jnp.tile(ret_cos, (batch, 1)), one_c], 0)
    ret_sin = jnp.concatenate([jnp.tile(ret_sin, (batch, 1)), zero_c], 0)
    ax_cos, ax_up, ax_dn = _axial_tables(seq)
    ax_tabs = (_token_table(ax_cos, batch, n_ctx, 1.0), _token_table(ax_up, batch, n_ctx, 0.0),
               _token_table(ax_dn, batch, n_ctx, 0.0))

    tm_ln = 512
    h, u = _modulate(x.reshape(n_lat, d), ctx.reshape(n_ctx, d), mod,
                     mod_idx(0, 1, tm_ln), mod_idx(0, 0, tm_ln), tm_ln)

    ret_w_o_b, mla_w_o_b, ffn_w_out_b = (w.astype(BF16) for w in (ret_w_o, mla_w_o, ffn_w_out))

    for i in range(depth):
        last = i == depth - 1
        n_rows = n_lat if last else n_tok
        j = i // 2
        if i % 2 == 0:
            qkv = _ret_qkv(u, ret_w_qkv, j, ret_cos, ret_sin, ret_heads, tm)
            decay_tile = jnp.broadcast_to(
                ret_decay_logit[j].astype(F32).reshape(2 * ret_heads, 1, 1), (2 * ret_heads, 8, LANES))
            o_f, o_b = _retention(qkv, decay_tile, batch, seq, ret_heads)
            y = _ret_gate(u, ret_w_g, j, o_f, o_b, tm)
            w_o = ret_w_o_b
        else:
            rq, rkv = mla_g_q.shape[1], mla_g_kv.shape[1]
            w_down = jnp.concatenate(
                [mla_w_dq[j], mla_w_dkv[j], jnp.zeros((d, LANES - MLA_D_ROPE), F32)], 1).astype(BF16)
            cq, ckv, kr = _mla_down(u, w_down, mla_g_q[j][None, :], mla_g_kv[j][None, :], ax_tabs, 512)
            wq_pad = jnp.pad(
                mla_w_uq[j].reshape(rq, MLA_HEADS, MLA_D_NOPE + MLA_D_ROPE),
                ((0, 0), (0, 0), (0, MLA_HEAD_PAD - MLA_D_NOPE - MLA_D_ROPE)),
            ).reshape(rq, MLA_HEADS * MLA_HEAD_PAD).astype(BF16)
            q, k, v = _mla_up(cq, ckv, kr, wq_pad, mla_w_ukv[j].astype(BF16), ax_tabs, 512)
            y = _attention(q, k, v, batch, seq, ctx_len, not last)
            w_o = mla_w_o_b

        h, u = _out_ln(y, w_o, j, h, mod, ln_g3, ln_b3, gate_idx=mod_idx(i, 2, tm_ln), ln_idx=2 * i,
                       sc_idx=mod_idx(i, 4, tm_ln), sh_idx=mod_idx(i, 3, tm_ln), alpha=alpha,
                       n_rows=n_rows, tm=tm_ln, tk=2048)
        hid = _ffn_in(u, ffn_w_in, i, n_rows, tm)
        nxt = min(i + 1, depth - 1)
        h, u = _out_ln(hid, ffn_w_out_b, i, h, mod, ln_g3, ln_b3, gate_idx=mod_idx(i, 5, tm_ln),
                       ln_idx=2 * i + 1, sc_idx=mod_idx(nxt, 1, tm_ln), sh_idx=mod_idx(nxt, 0, tm_ln),
                       alpha=alpha, n_rows=n_rows, tm=tm_ln, tk=ffn_w_out.shape[1] // 4)
    return h.reshape(batch, seq, d)
```

```python
import functools

import jax
import jax.numpy as jnp
from jax import lax
from jax.experimental import pallas as pl
from jax.experimental.pallas import tpu as pltpu

F32 = jnp.float32
BF16 = jnp.bfloat16

GRID_W = 64
RET_ROPE_BASE = 10000.0
GN_EPS = 1e-6
MLA_HEADS = 16
MLA_D_NOPE = 128
MLA_D_ROPE = 64
MLA_D_V = 128
MLA_SCALE = (MLA_D_NOPE + MLA_D_ROPE) ** -0.5
MLA_Q_SCALE = MLA_SCALE * 1.4426950408889634
AXIAL_ROPE_BASE = 10000.0
RMS_EPS = 1e-6
LN_EPS = 1e-5

LANES = 128
MXU_DIM = 256
VMEM_LIMIT_BYTES = 56 * 1024 * 1024

RET_CHUNK = 256
MLA_HEAD_PAD = 2 * LANES


def _params(*sem):
    return pltpu.CompilerParams(dimension_semantics=sem, vmem_limit_bytes=VMEM_LIMIT_BYTES)


def _silu(x):
    return 0.5 * x * (1.0 + jnp.tanh(0.5 * x))


def _row_tile(n_lat, n_ctx):
    for t in (1024, 512, 256):
        if n_lat % t == 0 and n_ctx % t == 0:
            return t
    raise ValueError("token counts must be multiples of 256")


def _mod_kernel(cond_ref, w_ref, b_ref, o_ref):
    s = _silu(cond_ref[...]).astype(BF16)
    acc = jnp.dot(s, w_ref[0].astype(BF16), preferred_element_type=F32)
    o_ref[0] = acc + b_ref[0]


def _modulation(cond8, ada_w, ada_b):
    depth, d, n = ada_w.shape
    tn = 1024
    return pl.pallas_call(
        _mod_kernel,
        grid=(depth, n // tn),
        in_specs=[
            pl.BlockSpec((8, d), lambda l, j: (0, 0)),
            pl.BlockSpec((1, d, tn), lambda l, j: (l, 0, j)),
            pl.BlockSpec((1, 1, tn), lambda l, j: (l, 0, j)),
        ],
        out_specs=pl.BlockSpec((1, 8, tn), lambda l, j: (l, 0, j)),
        out_shape=jax.ShapeDtypeStruct((depth, 8, n), F32),
        compiler_params=_params("parallel", "parallel"),
        name="adaln_mod",
    )(cond8, ada_w, ada_b.reshape(depth, 1, n))


def _modulate_kernel(x_ref, ctx_ref, sc_ref, sh_ref, u_ref, *, n_lat_tiles):
    def emit(src_ref):
        u_ref[...] = (src_ref[...] * (1.0 + sc_ref[0]) + sh_ref[0]).astype(BF16)

    is_lat = pl.program_id(0) < n_lat_tiles
    pl.when(is_lat)(lambda: emit(x_ref))
    pl.when(jnp.logical_not(is_lat))(lambda: emit(ctx_ref))


def _modulate(x2d, ctx2d, mod, sc_idx, sh_idx, tm):
    n_lat, d = x2d.shape
    n_ctx = ctx2d.shape[0]
    t = n_lat + n_ctx
    n_lat_tiles = n_lat // tm
    vec = lambda f: pl.BlockSpec((1, 1, d), lambda i: (f(i), 0, 0))
    return pl.pallas_call(
        functools.partial(_modulate_kernel, n_lat_tiles=n_lat_tiles),
        grid=(t // tm,),
        in_specs=[
            pl.BlockSpec((tm, d), lambda i: (jnp.minimum(i, n_lat_tiles - 1), 0)),
            pl.BlockSpec((tm, d), lambda i: (jnp.maximum(i - n_lat_tiles, 0), 0)),
            vec(sc_idx), vec(sh_idx),
        ],
        out_specs=pl.BlockSpec((tm, d), lambda i: (i, 0)),
        out_shape=jax.ShapeDtypeStruct((t, d), BF16),
        compiler_params=_params("arbitrary"),
        name="modulate_in",
    )(x2d, ctx2d, mod, mod)


def _cast_weights_once(pairs):
    @pl.when(pl.program_id(1) == 0)
    def _():
        for src_ref, dst_ref in pairs:
            dst_ref[...] = src_ref[...].astype(BF16)


def _row_groups(ref, n_split):
    rows_per = ref.shape[0] // n_split
    return [pl.ds(r * rows_per, rows_per) for r in range(n_split)]


def _ret_qkv_kernel(x_ref, w_ref, cos_ref, sin_ref, o_ref, wb_ref, *, n_q, n_k, k_scale, n_split):
    j = pl.program_id(0)
    tn = o_ref.shape[1]
    _cast_weights_once([(w_ref, wb_ref)])

    def product(rows):
        return jnp.dot(x_ref[rows, :], wb_ref[...], preferred_element_type=F32)

    @pl.when(j < n_q + n_k)
    def _():
        scale = jnp.where(j < n_q, 1.0, k_scale).astype(F32)
        for rows in _row_groups(o_ref, n_split):
            acc = product(rows)
            c = cos_ref[rows, :]
            s = sin_ref[rows, :]
            for g in range(tn // MXU_DIM):
                lo = g * MXU_DIM
                x1 = acc[:, lo:lo + LANES] * scale
                x2 = acc[:, lo + LANES:lo + MXU_DIM] * scale
                o_ref[rows, lo:lo + LANES] = (x1 * c - x2 * s).astype(BF16)
                o_ref[rows, lo + LANES:lo + MXU_DIM] = (x1 * s + x2 * c).astype(BF16)

    @pl.when(j >= n_q + n_k)
    def _():
        for rows in _row_groups(o_ref, n_split):
            o_ref[rows, :] = product(rows).astype(BF16)


def _ret_qkv(u, w, layer, cos, sin, n_heads, tm):
    t, d = u.shape
    n = w.shape[2]
    dk = d // n_heads
    tn = 1024
    n_q = n_heads * dk // tn
    kern = functools.partial(_ret_qkv_kernel, n_q=n_q, n_k=n_q, k_scale=dk ** -0.5, n_split=2)
    return pl.pallas_call(
        kern,
        grid=(n // tn, t // tm),
        in_specs=[
            pl.BlockSpec((tm, d), lambda j, i: (i, 0)),
            pl.BlockSpec((None, d, tn), lambda j, i: (layer, 0, j)),
            pl.BlockSpec((tm, LANES), lambda j, i: (i, 0)),
            pl.BlockSpec((tm, LANES), lambda j, i: (i, 0)),
        ],
        out_specs=pl.BlockSpec((tm, tn), lambda j, i: (i, j)),
        out_shape=jax.ShapeDtypeStruct((t, n), BF16),
        scratch_shapes=[pltpu.VMEM((d, tn), BF16)],
        compiler_params=_params("parallel", "arbitrary"),
        name="ret_qkv",
    )(u, w, cos, sin)


def _retention_kernel(dec_ref, qf_ref, qb_ref, of_ref, ob_ref, s_ref, intra_ref, qdec_ref, kdec_ref,
                      *, n_heads, dk, dv):
    c = RET_CHUNK
    step = pl.program_id(1)

    def log_gamma(idx):
        x = dec_ref[idx][0:1, 0:1]
        return jnp.minimum(x, 0.0) - jnp.log1p(jnp.exp(-jnp.abs(x)))

    @pl.when(step == 0)
    def _():
        s_ref[...] = jnp.zeros_like(s_ref)
        row = lax.broadcasted_iota(jnp.int32, (c, c), 0)
        col = lax.broadcasted_iota(jnp.int32, (c, c), 1)
        pos = lax.broadcasted_iota(jnp.int32, (c, dk), 0).astype(F32)
        for d in range(2):
            diff = (row - col) if d == 0 else (col - row)
            dist = jnp.maximum(diff, 0).astype(F32)
            ahead = pos + 1.0 if d == 0 else float(c) - pos
            behind = float(c - 1) - pos if d == 0 else pos
            for h in range(n_heads):
                idx = d * n_heads + h
                lg = log_gamma(idx)
                intra_ref[idx] = jnp.where(diff >= 0, jnp.exp(lg * dist), 0.0)
                qdec_ref[idx] = jnp.exp(lg * ahead).astype(BF16)
                kdec_ref[idx] = jnp.exp(lg * behind).astype(BF16)

    for d, (x_ref, o_ref) in enumerate(((qf_ref, of_ref), (qb_ref, ob_ref))):
        for h in range(n_heads):
            idx = d * n_heads + h
            q = x_ref[:, h * dk:(h + 1) * dk]
            k = x_ref[:, n_heads * dk + h * dk:n_heads * dk + (h + 1) * dk]
            v = x_ref[:, 2 * n_heads * dk + h * dv:2 * n_heads * dk + (h + 1) * dv]
            scores = lax.dot_general(q, k, (((1,), (1,)), ((), ())), preferred_element_type=F32)
            p = (scores * intra_ref[idx]).astype(BF16)
            state = s_ref[idx]
            o = (jnp.dot(p, v, preferred_element_type=F32)
                 + jnp.dot(q * qdec_ref[idx], state.astype(BF16), preferred_element_type=F32))
            s_ref[idx] = state * jnp.exp(log_gamma(idx) * float(c)) + lax.dot_general(
                k * kdec_ref[idx], v, (((0,), (0,)), ((), ())), preferred_element_type=F32)
            mu = jnp.mean(o, axis=-1, keepdims=True)
            ctr = o - mu
            var = jnp.mean(ctr * ctr, axis=-1, keepdims=True)
            o_ref[:, h * dv:(h + 1) * dv] = (ctr * lax.rsqrt(var + GN_EPS)).astype(BF16)


def _retention(qkv, decay_tile, batch, seq, n_heads):
    t, n = qkv.shape
    d = n // 4
    dk, dv = d // n_heads, 2 * d // n_heads
    c = RET_CHUNK
    n_lat = seq // c
    ctx_blk = batch * n_lat

    def fwd_blk(b, s):
        return jnp.where(s == 0, ctx_blk + b, b * n_lat + s - 1)

    def bwd_blk(b, s):
        return jnp.where(s == 0, ctx_blk + b, b * n_lat + n_lat - s)

    kern = functools.partial(_retention_kernel, n_heads=n_heads, dk=dk, dv=dv)
    out = jax.ShapeDtypeStruct((t, n_heads * dv), BF16)
    return pl.pallas_call(
        kern,
        grid=(batch, n_lat + 1),
        in_specs=[
            pl.BlockSpec((2 * n_heads, 8, LANES), lambda b, s: (0, 0, 0)),
            pl.BlockSpec((c, n), lambda b, s: (fwd_blk(b, s), 0)),
            pl.BlockSpec((c, n), lambda b, s: (bwd_blk(b, s), 0)),
        ],
        out_specs=[
            pl.BlockSpec((c, n_heads * dv), lambda b, s: (fwd_blk(b, s), 0)),
            pl.BlockSpec((c, n_heads * dv), lambda b, s: (bwd_blk(b, s), 0)),
        ],
        out_shape=[out, out],
        scratch_shapes=[
            pltpu.VMEM((2 * n_heads, dk, dv), F32),
            pltpu.VMEM((2 * n_heads, c, c), F32),
            pltpu.VMEM((2 * n_heads, c, dk), BF16),
            pltpu.VMEM((2 * n_heads, c, dk), BF16),
        ],
        compiler_params=_params("parallel", "arbitrary"),
        name="retention_scan",
    )(decay_tile, qkv, qkv)


def _ret_gate_kernel(x_ref, wf_ref, wb_ref, of_ref, ob_ref, y_ref, wf_bf, wb_bf, *, n_split):
    _cast_weights_once([(wf_ref, wf_bf), (wb_ref, wb_bf)])
    for rows in _row_groups(y_ref, n_split):
        x = x_ref[rows, :]
        gf = jnp.dot(x, wf_bf[...], preferred_element_type=F32)
        gb = jnp.dot(x, wb_bf[...], preferred_element_type=F32)
        y = _silu(gf) * of_ref[rows, :].astype(F32) + _silu(gb) * ob_ref[rows, :].astype(F32)
        y_ref[rows, :] = y.astype(BF16)


def _ret_gate(u, w_g, layer, o_f, o_b, tm):
    t, d = u.shape
    n = w_g.shape[2] // 2
    tn = 512
    nb = n // tn
    return pl.pallas_call(
        functools.partial(_ret_gate_kernel, n_split=2),
        grid=(nb, t // tm),
        in_specs=[
            pl.BlockSpec((tm, d), lambda j, i: (i, 0)),
            pl.BlockSpec((None, d, tn), lambda j, i: (layer, 0, j)),
            pl.BlockSpec((None, d, tn), lambda j, i: (layer, 0, nb + j)),
            pl.BlockSpec((tm, tn), lambda j, i: (i, j)),
            pl.BlockSpec((tm, tn), lambda j, i: (i, j)),
        ],
        out_specs=pl.BlockSpec((tm, tn), lambda j, i: (i, j)),
        out_shape=jax.ShapeDtypeStruct((t, n), BF16),
        scratch_shapes=[pltpu.VMEM((d, tn), BF16), pltpu.VMEM((d, tn), BF16)],
        compiler_params=_params("parallel", "arbitrary"),
        name="ret_gate",
    )(u, w_g, w_g, o_f, o_b)


def _out_ln_kernel(*refs, alpha, n_k, n_split, pair, n_x, n_h, n_lat_tiles):
    x_refs, refs = refs[:n_x], refs[n_x:]
    w_ref, refs = refs[0], refs[1:]
    h_refs, refs = refs[:n_h], refs[n_h:]
    gate_ref, lng_ref, lnb_ref, sc_ref, sh_ref, hout_ref, u_ref, *acc = refs
    k = pl.program_id(1)
    is_lat = pair * pl.program_id(0) + pl.program_id(2) < n_lat_tiles

    def select(parts, rows):
        if len(parts) == 1:
            return parts[0][rows, :]
        return jnp.where(is_lat, parts[0][rows, :], parts[1][rows, :])

    def partial_product(rows):
        return jnp.dot(select(x_refs, rows), w_ref[...], preferred_element_type=F32)

    def finish(rows, y):
        z = alpha * select(h_refs, rows) + gate_ref[0] * y
        mu = jnp.mean(z, axis=-1, keepdims=True)
        ctr = z - mu
        var = jnp.mean(ctr * ctr, axis=-1, keepdims=True)
        hn = ctr * lax.rsqrt(var + LN_EPS) * lng_ref[0] + lnb_ref[0]
        hout_ref[rows, :] = hn
        u_ref[rows, :] = (hn * (1.0 + sc_ref[0]) + sh_ref[0]).astype(BF16)

    def last_step(prev):
        for rows in _row_groups(hout_ref, n_split):
            y = partial_product(rows)
            finish(rows, y if prev is None else prev[rows, :] + y)

    if n_k == 1:
        last_step(None)
        return
    acc_ref = acc[0].at[pl.program_id(2)]

    @pl.when(k == 0)
    def _():
        acc_ref[...] = partial_product(slice(None))

    @pl.when(jnp.logical_and(k > 0, k < n_k - 1))
    def _():
        acc_ref[...] += partial_product(slice(None))

    @pl.when(k == n_k - 1)
    def _():
        last_step(acc_ref)


def _out_ln(x, w, layer, h, mod, ln_g, ln_b, *, gate_idx, ln_idx, sc_idx, sh_idx, alpha, n_rows, tm, tk,
            n_split=2):
    x_parts = x if isinstance(x, tuple) else (x,)
    h_parts = h if isinstance(h, tuple) else (h,)
    kdim = x_parts[0].shape[1]
    d = w.shape[2]
    n_k = kdim // tk
    pair = 2 if n_rows % (2 * tm) == 0 else 1
    assert n_rows % (pair * tm) == 0 and kdim % tk == 0
    split = [p for p in (x_parts, h_parts) if len(p) == 2]
    n_lat_tiles = split[0][0].shape[0] // tm if split else 0

    def row_blk(i, k, m):
        return pair * i + jnp.where(k == n_k - 1, m, 0)

    def part_specs(parts, width, tile_fn, col_fn):
        if len(parts) == 1:
            return [pl.BlockSpec((tm, width), lambda i, k, m: (tile_fn(i, k, m), col_fn(k)))]
        assert parts[0].shape[0] == n_lat_tiles * tm
        return [
            pl.BlockSpec((tm, width), lambda i, k, m: (jnp.minimum(tile_fn(i, k, m), n_lat_tiles - 1), col_fn(k))),
            pl.BlockSpec((tm, width), lambda i, k, m: (jnp.maximum(tile_fn(i, k, m) - n_lat_tiles, 0), col_fn(k))),
        ]

    vec = lambda arr_idx: pl.BlockSpec((1, 1, d), lambda i, k, m: (arr_idx(row_blk(i, k, m)), 0, 0))
    row = pl.BlockSpec((tm, d), lambda i, k, m: (row_blk(i, k, m), 0))
    kern = functools.partial(_out_ln_kernel, alpha=alpha, n_k=n_k, n_split=n_split, pair=pair,
                             n_x=len(x_parts), n_h=len(h_parts), n_lat_tiles=n_lat_tiles)
    return pl.pallas_call(
        kern,
        grid=(n_rows // (pair * tm), n_k, pair),
        in_specs=[
            *part_specs(x_parts, tk, lambda i, k, m: pair * i + m, lambda k: k),
            pl.BlockSpec((None, tk, d), lambda i, k, m: (layer, k, 0)),
            *part_specs(h_parts, d, row_blk, lambda k: 0),
            vec(gate_idx),
            vec(lambda r: ln_idx),
            vec(lambda r: ln_idx),
            vec(sc_idx),
            vec(sh_idx),
        ],
        out_specs=[row, row],
        out_shape=[
            jax.ShapeDtypeStruct((n_rows, d), F32),
            jax.ShapeDtypeStruct((n_rows, d), BF16),
        ],
        scratch_shapes=[pltpu.VMEM((pair, tm, d), F32)] if n_k > 1 else [],
        compiler_params=_params("parallel", "arbitrary", "arbitrary"),
        name="out_ln",
    )(*x_parts, w, *h_parts, mod, ln_g, ln_b, mod, mod)


def _ffn_in_kernel(x_ref, wa_ref, wb_ref, o_ref, wa_bf, wb_bf, *, n_split):
    _cast_weights_once([(wa_ref, wa_bf), (wb_ref, wb_bf)])
    for rows in _row_groups(o_ref, n_split):
        x = x_ref[rows, :]
        a = jnp.dot(x, wa_bf[...], preferred_element_type=F32)
        b = jnp.dot(x, wb_bf[...], preferred_element_type=F32)
        o_ref[rows, :] = (_silu(a) * b).astype(BF16)


def _ffn_in(u, w_in, layer, n_rows, tm):
    d = u.shape[1]
    f = w_in.shape[2] // 2
    tn = 512
    nb = f // tn
    return pl.pallas_call(
        functools.partial(_ffn_in_kernel, n_split=2),
        grid=(nb, n_rows // tm),
        in_specs=[
            pl.BlockSpec((tm, d), lambda j, i: (i, 0)),
            pl.BlockSpec((None, d, tn), lambda j, i: (layer, 0, j)),
            pl.BlockSpec((None, d, tn), lambda j, i: (layer, 0, nb + j)),
        ],
        out_specs=pl.BlockSpec((tm, tn), lambda j, i: (i, j)),
        out_shape=jax.ShapeDtypeStruct((n_rows, f), BF16),
        scratch_shapes=[pltpu.VMEM((d, tn), BF16), pltpu.VMEM((d, tn), BF16)],
        compiler_params=_params("parallel", "arbitrary"),
        name="ffn_in",
    )(u, w_in, w_in)


def _rope_lanes(x, c_ref, s_up_ref, s_dn_ref):
    half = MLA_D_ROPE // 4
    up = pltpu.roll(x, LANES - half, 1)
    dn = pltpu.roll(x, half, 1)
    return x * c_ref[...] + up * s_up_ref[...] + dn * s_dn_ref[...]


def _rms(x, g):
    return x * lax.rsqrt(jnp.mean(x * x, axis=-1, keepdims=True) + RMS_EPS) * g


def _mla_down_kernel(x_ref, w_ref, gq_ref, gkv_ref, c_ref, su_ref, sd_ref,
                     cq_ref, ckv_ref, kr_ref, *, rq, rkv):
    for rows in _row_groups(x_ref, 2):
        acc = jnp.dot(x_ref[rows, :], w_ref[...], preferred_element_type=F32)
        cq_ref[rows, :] = _rms(acc[:, :rq], gq_ref[...]).astype(BF16)
        ckv_ref[rows, :] = _rms(acc[:, rq:rq + rkv], gkv_ref[...]).astype(BF16)
        kr_ref[rows, :] = _rope_lanes(acc[:, rq + rkv:], c_ref.at[rows, :], su_ref.at[rows, :],
                                      sd_ref.at[rows, :]).astype(BF16)


def _mla_down(u, w_down, g_q, g_kv, tabs, tm):
    t, d = u.shape
    rq, rkv = g_q.shape[1], g_kv.shape[1]
    n = w_down.shape[1]
    tab = pl.BlockSpec((tm, LANES), lambda i: (i, 0))
    kern = functools.partial(_mla_down_kernel, rq=rq, rkv=rkv)
    return pl.pallas_call(
        kern,
        grid=(t // tm,),
        in_specs=[
            pl.BlockSpec((tm, d), lambda i: (i, 0)),
            pl.BlockSpec((d, n), lambda i: (0, 0)),
            pl.BlockSpec((1, rq), lambda i: (0, 0)),
            pl.BlockSpec((1, rkv), lambda i: (0, 0)),
            tab, tab, tab,
        ],
        out_specs=[
            pl.BlockSpec((tm, rq), lambda i: (i, 0)),
            pl.BlockSpec((tm, rkv), lambda i: (i, 0)),
            pl.BlockSpec((tm, LANES), lambda i: (i, 0)),
        ],
        out_shape=[
            jax.ShapeDtypeStruct((t, rq), BF16),
            jax.ShapeDtypeStruct((t, rkv), BF16),
            jax.ShapeDtypeStruct((t, LANES), BF16),
        ],
        compiler_params=_params("parallel"),
        name="mla_down",
    )(u, w_down, g_q, g_kv, *tabs)


def _mla_up_kernel(cq_ref, ckv_ref, kr_ref, wq_ref, wkv_ref, c_ref, su_ref, sd_ref,
                   q_ref, k_ref, v_ref):
    cq = cq_ref[...]
    ckv = ckv_ref[...]
    kr = kr_ref[...]
    hp = MLA_HEAD_PAD
    for h in range(MLA_HEADS):
        qh = jnp.dot(cq, wq_ref[:, h * hp:(h + 1) * hp], preferred_element_type=F32) * MLA_Q_SCALE
        q_ref[:, h * hp:h * hp + LANES] = qh[:, :LANES].astype(BF16)
        q_ref[:, h * hp + LANES:(h + 1) * hp] = _rope_lanes(qh[:, LANES:], c_ref, su_ref, sd_ref).astype(BF16)
        kvh = jnp.dot(ckv, wkv_ref[:, h * hp:(h + 1) * hp], preferred_element_type=F32)
        k_ref[:, h * hp:h * hp + LANES] = kvh[:, :LANES].astype(BF16)
        k_ref[:, h * hp + LANES:(h + 1) * hp] = kr
        v_ref[:, h * hp:h * hp + MLA_D_V] = kvh[:, LANES:].astype(BF16)
        v_ref[:, h * hp + MLA_D_V:(h + 1) * hp] = jnp.ones((kvh.shape[0], hp - MLA_D_V), BF16)


def _mla_up(cq, ckv, kr, wq_pad, wkv, tabs, tm):
    t, rq = cq.shape
    rkv = ckv.shape[1]
    nq = wq_pad.shape[1]
    nkv = wkv.shape[1]
    tab = pl.BlockSpec((tm, LANES), lambda i: (i, 0))
    return pl.pallas_call(
        _mla_up_kernel,
        grid=(t // tm,),
        in_specs=[
            pl.BlockSpec((tm, rq), lambda i: (i, 0)),
            pl.BlockSpec((tm, rkv), lambda i: (i, 0)),
            pl.BlockSpec((tm, LANES), lambda i: (i, 0)),
            pl.BlockSpec((rq, nq), lambda i: (0, 0)),
            pl.BlockSpec((rkv, nkv), lambda i: (0, 0)),
            tab, tab, tab,
        ],
        out_specs=[
            pl.BlockSpec((tm, nq), lambda i: (i, 0)),
            pl.BlockSpec((tm, nq), lambda i: (i, 0)),
            pl.BlockSpec((tm, nq), lambda i: (i, 0)),
        ],
        out_shape=[
            jax.ShapeDtypeStruct((t, nq), BF16),
            jax.ShapeDtypeStruct((t, nq), BF16),
            jax.ShapeDtypeStruct((t, nq), BF16),
        ],
        compiler_params=_params("parallel"),
        name="mla_up",
    )(cq, ckv, kr, wq_pad, wkv, *tabs)


def _scores(q, k):
    return lax.dot_general(q, k, (((1,), (1,)), ((), ())), preferred_element_type=F32)


def _attn_lat_kernel(q_ref, kl_ref, kc_ref, vl_ref, vc_ref, o_ref, *, key_chunk):
    q = q_ref[...]
    n_lat = kl_ref.shape[0] // key_chunk
    chunks = [(kl_ref, vl_ref, c * key_chunk, key_chunk) for c in range(n_lat)]
    chunks.append((kc_ref, vc_ref, 0, kc_ref.shape[0]))

    def chunk_scores(c):
        k_ref, _, lo, size = chunks[c]
        return _scores(q, k_ref[lo:lo + size, :])

    s = chunk_scores(0)
    m = acc = None
    for c, (_, v_ref, lo, size) in enumerate(chunks):
        s_next = chunk_scores(c + 1) if c + 1 < len(chunks) else None
        m_c = jnp.max(s, axis=-1, keepdims=True)
        v = v_ref[lo:lo + size, :]
        if c == 0:
            m = m_c
            acc = jnp.dot(jnp.exp2(s - m).astype(BF16), v, preferred_element_type=F32)
        else:
            m_new = jnp.maximum(m, m_c)
            acc = (jnp.exp2(m - m_new) * acc
                   + jnp.dot(jnp.exp2(s - m_new).astype(BF16), v, preferred_element_type=F32))
            m = m_new
        s = s_next
    _store_normalised(o_ref, acc)


def _store_normalised(o_ref, acc):
    o_ref[...] = (acc[:, :MLA_D_V] / acc[:, MLA_D_V:MLA_D_V + 1]).astype(BF16)


def _attn_ctx_kernel(q_ref, kc_ref, vc_ref, o_ref):
    s_c = _scores(q_ref[...], kc_ref[...])
    p_c = jnp.exp2(s_c - jnp.max(s_c, axis=-1, keepdims=True))
    _store_normalised(o_ref, jnp.dot(p_c.astype(BF16), vc_ref[...], preferred_element_type=F32))


def _attention(q, k, v, batch, seq, ctx_len, with_ctx_queries):
    hp, dv = MLA_HEAD_PAD, MLA_D_V
    tq = min(seq, 1024)
    nq = seq // tq
    ctx0 = batch * seq // ctx_len
    o_lat = pl.pallas_call(
        functools.partial(_attn_lat_kernel, key_chunk=min(seq, 2048)),
        grid=(batch, MLA_HEADS, nq),
        in_specs=[
            pl.BlockSpec((tq, hp), lambda b, h, i: (b * nq + i, h)),
            pl.BlockSpec((seq, hp), lambda b, h, i: (b, h)),
            pl.BlockSpec((ctx_len, hp), lambda b, h, i: (ctx0 + b, h)),
            pl.BlockSpec((seq, hp), lambda b, h, i: (b, h)),
            pl.BlockSpec((ctx_len, hp), lambda b, h, i: (ctx0 + b, h)),
        ],
        out_specs=pl.BlockSpec((tq, dv), lambda b, h, i: (b * nq + i, h)),
        out_shape=jax.ShapeDtypeStruct((batch * seq, MLA_HEADS * dv), BF16),
        compiler_params=_params("parallel", "parallel", "arbitrary"),
        name="attn_lat",
    )(q, k, k, v, v)
    if not with_ctx_queries:
        return o_lat
    o_ctx = pl.pallas_call(
        _attn_ctx_kernel,
        grid=(batch, MLA_HEADS),
        in_specs=[
            pl.BlockSpec((ctx_len, hp), lambda b, h: (ctx0 + b, h)),
            pl.BlockSpec((ctx_len, hp), lambda b, h: (ctx0 + b, h)),
            pl.BlockSpec((ctx_len, hp), lambda b, h: (ctx0 + b, h)),
        ],
        out_specs=pl.BlockSpec((ctx_len, dv), lambda b, h: (b, h)),
        out_shape=jax.ShapeDtypeStruct((batch * ctx_len, MLA_HEADS * dv), BF16),
        compiler_params=_params("parallel", "parallel"),
        name="attn_ctx",
    )(q, k, v)
    return o_lat, o_ctx


def _retention_tables(seq, n_ctx, dk):
    inv = RET_ROPE_BASE ** (-jnp.linspace(0.0, 1.0, dk // 2, dtype=F32))
    ang = jnp.arange(seq, dtype=F32)[:, None] * inv[None, :]
    return jnp.cos(ang), jnp.sin(ang), jnp.ones((n_ctx, dk // 2), F32), jnp.zeros((n_ctx, dk // 2), F32)


def _axial_tables(seq):
    n_f = MLA_D_ROPE // 4
    inv = AXIAL_ROPE_BASE ** (-jnp.arange(n_f, dtype=F32) * 2.0 / (MLA_D_ROPE // 2))
    tpos = jnp.arange(seq)
    row = (tpos // GRID_W).astype(F32)[:, None] * inv[None, :]
    col = (tpos % GRID_W).astype(F32)[:, None] * inv[None, :]
    z = jnp.zeros((seq, n_f), F32)
    tail = LANES - MLA_D_ROPE
    cos = jnp.concatenate([jnp.cos(row), jnp.cos(row), jnp.cos(col), jnp.cos(col), jnp.ones((seq, tail), F32)], 1)
    s_up = jnp.concatenate([-jnp.sin(row), z, -jnp.sin(col), z, jnp.zeros((seq, tail), F32)], 1)
    s_dn = jnp.concatenate([z, jnp.sin(row), z, jnp.sin(col), jnp.zeros((seq, tail), F32)], 1)
    return cos, s_up, s_dn


def _token_table(lat, batch, ctx_rows, fill):
    return jnp.concatenate([jnp.tile(lat, (batch, 1)), jnp.full((ctx_rows, lat.shape[1]), fill, F32)], 0)


def kernel(x, c, ctx, c_ctx, ada_w, ada_b, ln_g, ln_b, ret_w_qkv, ret_w_g, ret_decay_logit, ret_w_o,
           mla_w_dq, mla_g_q, mla_w_uq, mla_w_dkv, mla_g_kv, mla_w_ukv, mla_w_o, ffn_w_in, ffn_w_out):
    batch, seq, d = x.shape
    ctx_len = ctx.shape[1]
    depth = ada_w.shape[0]
    n_lat, n_ctx = batch * seq, batch * ctx_len
    n_tok = n_lat + n_ctx
    ret_heads = ret_decay_logit.shape[-1]
    assert ctx_len == RET_CHUNK and seq % RET_CHUNK == 0 and batch + 1 <= 8
    tm = _row_tile(n_lat, n_ctx)
    alpha = (2 * depth) ** 0.25

    cond8 = jnp.concatenate([c, c_ctx[None, :], jnp.zeros((8 - batch - 1, d), F32)], 0)
    mod = _modulation(cond8, ada_w, ada_b).reshape(depth * 8 * 6, 1, d)

    def mod_idx(layer, chunk, tile):
        def f(i):
            r = jnp.where(i * tile >= n_lat, batch, (i * tile) // seq)
            return (layer * 8 + r) * 6 + chunk
        return f

    ln_g3 = ln_g.reshape(depth * 2, 1, d)
    ln_b3 = ln_b.reshape(depth * 2, 1, d)

    ret_cos, ret_sin, one_c, zero_c = _retention_tables(seq, n_ctx, d // ret_heads)
    ret_cos = jnp.concatenate([jnp.tile(ret_cos, (batch, 1)), one_c], 0)
    ret_sin = jnp.concatenate([jnp.tile(ret_sin, (batch, 1)), zero_c], 0)
    ax_cos, ax_up, ax_dn = _axial_tables(seq)
    ax_tabs = (_token_table(ax_cos, batch, n_ctx, 1.0), _token_table(ax_up, batch, n_ctx, 0.0),
               _token_table(ax_dn, batch, n_ctx, 0.0))

    tm_ln = 512
    h = (x.reshape(n_lat, d), ctx.reshape(n_ctx, d))
    u = _modulate(*h, mod, mod_idx(0, 1, tm_ln), mod_idx(0, 0, tm_ln), tm_ln)

    ret_w_o_b, mla_w_o_b, ffn_w_out_b = (w.astype(BF16) for w in (ret_w_o, mla_w_o, ffn_w_out))

    for i in range(depth):
        last = i == depth - 1
        n_rows = n_lat if last else n_tok
        j = i // 2
        if i % 2 == 0:
            qkv = _ret_qkv(u, ret_w_qkv, j, ret_cos, ret_sin, ret_heads, tm)
            decay_tile = jnp.broadcast_to(
                ret_decay_logit[j].astype(F32).reshape(2 * ret_heads, 1, 1), (2 * ret_heads, 8, LANES))
            o_f, o_b = _retention(qkv, decay_tile, batch, seq, ret_heads)
            y = _ret_gate(u, ret_w_g, j, o_f, o_b, tm)
            w_o = ret_w_o_b
        else:
            rq, rkv = mla_g_q.shape[1], mla_g_kv.shape[1]
            w_down = jnp.concatenate(
                [mla_w_dq[j], mla_w_dkv[j], jnp.zeros((d, LANES - MLA_D_ROPE), F32)], 1).astype(BF16)
            cq, ckv, kr = _mla_down(u, w_down, mla_g_q[j][None, :], mla_g_kv[j][None, :], ax_tabs, 512)
            wq_pad = jnp.pad(
                mla_w_uq[j].reshape(rq, MLA_HEADS, MLA_D_NOPE + MLA_D_ROPE),
                ((0, 0), (0, 0), (0, MLA_HEAD_PAD - MLA_D_NOPE - MLA_D_ROPE)),
            ).reshape(rq, MLA_HEADS * MLA_HEAD_PAD).astype(BF16)
            q, k, v = _mla_up(cq, ckv, kr, wq_pad, mla_w_ukv[j].astype(BF16), ax_tabs, 512)
            y = _attention(q, k, v, batch, seq, ctx_len, not last)
            w_o = mla_w_o_b

        h, u = _out_ln(y, w_o, j, h, mod, ln_g3, ln_b3, gate_idx=mod_idx(i, 2, tm_ln), ln_idx=2 * i,
                       sc_idx=mod_idx(i, 4, tm_ln), sh_idx=mod_idx(i, 3, tm_ln), alpha=alpha,
                       n_rows=n_rows, tm=tm_ln, tk=1024 if isinstance(h, tuple) else 2048)
        hid = _ffn_in(u, ffn_w_in, i, n_rows, tm)
        nxt = min(i + 1, depth - 1)
        h, u = _out_ln(hid, ffn_w_out_b, i, h, mod, ln_g3, ln_b3, gate_idx=mod_idx(i, 5, tm_ln),
                       ln_idx=2 * i + 1, sc_idx=mod_idx(nxt, 1, tm_ln), sh_idx=mod_idx(nxt, 0, tm_ln),
                       alpha=alpha, n_rows=n_rows, tm=tm_ln, tk=ffn_w_out.shape[1] // 4)
    return h.reshape(batch, seq, d)
```

```python
import functools

import jax
import jax.numpy as jnp
from jax import lax
from jax.experimental import pallas as pl
from jax.experimental.pallas import tpu as pltpu

F32 = jnp.float32
BF16 = jnp.bfloat16

GRID_W = 64
RET_ROPE_BASE = 10000.0
GN_EPS = 1e-6
MLA_HEADS = 16
MLA_D_NOPE = 128
MLA_D_ROPE = 64
MLA_D_V = 128
MLA_SCALE = (MLA_D_NOPE + MLA_D_ROPE) ** -0.5
MLA_Q_SCALE = MLA_SCALE * 1.4426950408889634
AXIAL_ROPE_BASE = 10000.0
RMS_EPS = 1e-6
LN_EPS = 1e-5

LANES = 128
MXU_DIM = 256
VMEM_LIMIT_BYTES = 56 * 1024 * 1024

RET_CHUNK = 256

TM_SMALL = 512
OUT_TILES_MIXER = (512, 2, 2048)
OUT_TILES_FFN = (256, 4, 2)
ATTN_Q_TILE = 1024
ATTN_KEY_CHUNK = 2048
MLA_HEAD_PAD = 2 * LANES


def _params(*sem):
    return pltpu.CompilerParams(dimension_semantics=sem, vmem_limit_bytes=VMEM_LIMIT_BYTES)


def _silu(x):
    return 0.5 * x * (1.0 + jnp.tanh(0.5 * x))


def _row_tile(n_lat, n_ctx):
    for t in (1024, 512, 256):
        if n_lat % t == 0 and n_ctx % t == 0:
            return t
    raise ValueError("token counts must be multiples of 256")


def _mod_kernel(cond_ref, w_ref, b_ref, o_ref):
    s = _silu(cond_ref[...]).astype(BF16)
    acc = jnp.dot(s, w_ref[0].astype(BF16), preferred_element_type=F32)
    o_ref[0] = acc + b_ref[0]


def _modulation(cond8, ada_w, ada_b):
    depth, d, n = ada_w.shape
    tn = 1024
    return pl.pallas_call(
        _mod_kernel,
        grid=(depth, n // tn),
        in_specs=[
            pl.BlockSpec((8, d), lambda l, j: (0, 0)),
            pl.BlockSpec((1, d, tn), lambda l, j: (l, 0, j)),
            pl.BlockSpec((1, 1, tn), lambda l, j: (l, 0, j)),
        ],
        out_specs=pl.BlockSpec((1, 8, tn), lambda l, j: (l, 0, j)),
        out_shape=jax.ShapeDtypeStruct((depth, 8, n), F32),
        compiler_params=_params("parallel", "parallel"),
        name="adaln_mod",
    )(cond8, ada_w, ada_b.reshape(depth, 1, n))


def _modulate_kernel(x_ref, ctx_ref, sc_ref, sh_ref, h_ref, u_ref, *, n_lat_tiles):
    def emit(src_ref):
        hv = src_ref[...]
        h_ref[...] = hv
        u_ref[...] = (hv * (1.0 + sc_ref[0]) + sh_ref[0]).astype(BF16)

    is_lat = pl.program_id(0) < n_lat_tiles
    pl.when(is_lat)(lambda: emit(x_ref))
    pl.when(jnp.logical_not(is_lat))(lambda: emit(ctx_ref))


def _modulate(x2d, ctx2d, mod, sc_idx, sh_idx, tm):
    n_lat, d = x2d.shape
    n_ctx = ctx2d.shape[0]
    t = n_lat + n_ctx
    n_lat_tiles = n_lat // tm
    vec = lambda f: pl.BlockSpec((1, 1, d), lambda i: (f(i), 0, 0))
    row = pl.BlockSpec((tm, d), lambda i: (i, 0))
    return pl.pallas_call(
        functools.partial(_modulate_kernel, n_lat_tiles=n_lat_tiles),
        grid=(t // tm,),
        in_specs=[
            pl.BlockSpec((tm, d), lambda i: (jnp.minimum(i, n_lat_tiles - 1), 0)),
            pl.BlockSpec((tm, d), lambda i: (jnp.maximum(i - n_lat_tiles, 0), 0)),
            vec(sc_idx), vec(sh_idx),
        ],
        out_specs=[row, row],
        out_shape=[jax.ShapeDtypeStruct((t, d), F32), jax.ShapeDtypeStruct((t, d), BF16)],
        compiler_params=_params("arbitrary"),
        name="modulate_in",
    )(x2d, ctx2d, mod, mod)


def _cast_weights_once(pairs):
    @pl.when(pl.program_id(1) == 0)
    def _():
        for src_ref, dst_ref in pairs:
            dst_ref[...] = src_ref[...].astype(BF16)


def _row_groups(ref, n_split):
    rows_per = ref.shape[0] // n_split
    return [pl.ds(r * rows_per, rows_per) for r in range(n_split)]


def _ret_qkv_kernel(x_ref, w_ref, cos_ref, sin_ref, o_ref, wb_ref, *, n_q, n_k, k_scale, n_split):
    j = pl.program_id(0)
    tn = o_ref.shape[1]
    _cast_weights_once([(w_ref, wb_ref)])

    def product(rows):
        return jnp.dot(x_ref[rows, :], wb_ref[...], preferred_element_type=F32)

    @pl.when(j < n_q + n_k)
    def _():
        scale = jnp.where(j < n_q, 1.0, k_scale).astype(F32)
        for rows in _row_groups(o_ref, n_split):
            acc = product(rows)
            c = cos_ref[rows, :]
            s = sin_ref[rows, :]
            for g in range(tn // MXU_DIM):
                lo = g * MXU_DIM
                x1 = acc[:, lo:lo + LANES] * scale
                x2 = acc[:, lo + LANES:lo + MXU_DIM] * scale
                o_ref[rows, lo:lo + LANES] = (x1 * c - x2 * s).astype(BF16)
                o_ref[rows, lo + LANES:lo + MXU_DIM] = (x1 * s + x2 * c).astype(BF16)

    @pl.when(j >= n_q + n_k)
    def _():
        for rows in _row_groups(o_ref, n_split):
            o_ref[rows, :] = product(rows).astype(BF16)


def _ret_qkv(u, w, layer, cos, sin, n_heads, tm):
    t, d = u.shape
    n = w.shape[2]
    dk = d // n_heads
    tn = 1024
    n_q = n_heads * dk // tn
    kern = functools.partial(_ret_qkv_kernel, n_q=n_q, n_k=n_q, k_scale=dk ** -0.5, n_split=2)
    return pl.pallas_call(
        kern,
        grid=(n // tn, t // tm),
        in_specs=[
            pl.BlockSpec((tm, d), lambda j, i: (i, 0)),
            pl.BlockSpec((None, d, tn), lambda j, i: (layer, 0, j)),
            pl.BlockSpec((tm, LANES), lambda j, i: (i, 0)),
            pl.BlockSpec((tm, LANES), lambda j, i: (i, 0)),
        ],
        out_specs=pl.BlockSpec((tm, tn), lambda j, i: (i, j)),
        out_shape=jax.ShapeDtypeStruct((t, n), BF16),
        scratch_shapes=[pltpu.VMEM((d, tn), BF16)],
        compiler_params=_params("parallel", "arbitrary"),
        name="ret_qkv",
    )(u, w, cos, sin)


def _retention_kernel(dec_ref, qf_ref, qb_ref, of_ref, ob_ref, s_ref, intra_ref, qdec_ref, kdec_ref,
                      *, n_heads, dk, dv):
    c = RET_CHUNK
    step = pl.program_id(1)

    def log_gamma(idx):
        x = dec_ref[idx][0:1, 0:1]
        return jnp.minimum(x, 0.0) - jnp.log1p(jnp.exp(-jnp.abs(x)))

    @pl.when(step == 0)
    def _():
        s_ref[...] = jnp.zeros_like(s_ref)
        row = lax.broadcasted_iota(jnp.int32, (c, c), 0)
        col = lax.broadcasted_iota(jnp.int32, (c, c), 1)
        pos = lax.broadcasted_iota(jnp.int32, (c, dk), 0).astype(F32)
        for d in range(2):
            diff = (row - col) if d == 0 else (col - row)
            dist = jnp.maximum(diff, 0).astype(F32)
            ahead = pos + 1.0 if d == 0 else float(c) - pos
            behind = float(c - 1) - pos if d == 0 else pos
            for h in range(n_heads):
                idx = d * n_heads + h
                lg = log_gamma(idx)
                intra_ref[idx] = jnp.where(diff >= 0, jnp.exp(lg * dist), 0.0)
                qdec_ref[idx] = jnp.exp(lg * ahead).astype(BF16)
                kdec_ref[idx] = jnp.exp(lg * behind).astype(BF16)

    for d, (x_ref, o_ref) in enumerate(((qf_ref, of_ref), (qb_ref, ob_ref))):
        for h in range(n_heads):
            idx = d * n_heads + h
            q = x_ref[:, h * dk:(h + 1) * dk]
            k = x_ref[:, n_heads * dk + h * dk:n_heads * dk + (h + 1) * dk]
            v = x_ref[:, 2 * n_heads * dk + h * dv:2 * n_heads * dk + (h + 1) * dv]
            scores = lax.dot_general(q, k, (((1,), (1,)), ((), ())), preferred_element_type=F32)
            p = (scores * intra_ref[idx]).astype(BF16)
            state = s_ref[idx]
            o = (jnp.dot(p, v, preferred_element_type=F32)
                 + jnp.dot(q * qdec_ref[idx], state.astype(BF16), preferred_element_type=F32))
            s_ref[idx] = state * jnp.exp(log_gamma(idx) * float(c)) + lax.dot_general(
                k * kdec_ref[idx], v, (((0,), (0,)), ((), ())), preferred_element_type=F32)
            mu = jnp.mean(o, axis=-1, keepdims=True)
            ctr = o - mu
            var = jnp.mean(ctr * ctr, axis=-1, keepdims=True)
            o_ref[:, h * dv:(h + 1) * dv] = (ctr * lax.rsqrt(var + GN_EPS)).astype(BF16)


def _retention(qkv, decay_tile, batch, seq, n_heads):
    t, n = qkv.shape
    d = n // 4
    dk, dv = d // n_heads, 2 * d // n_heads
    c = RET_CHUNK
    n_lat = seq // c
    ctx_blk = batch * n_lat

    def fwd_blk(b, s):
        return jnp.where(s == 0, ctx_blk + b, b * n_lat + s - 1)

    def bwd_blk(b, s):
        return jnp.where(s == 0, ctx_blk + b, b * n_lat + n_lat - s)

    kern = functools.partial(_retention_kernel, n_heads=n_heads, dk=dk, dv=dv)
    out = jax.ShapeDtypeStruct((t, n_heads * dv), BF16)
    return pl.pallas_call(
        kern,
        grid=(batch, n_lat + 1),
        in_specs=[
            pl.BlockSpec((2 * n_heads, 8, LANES), lambda b, s: (0, 0, 0)),
            pl.BlockSpec((c, n), lambda b, s: (fwd_blk(b, s), 0)),
            pl.BlockSpec((c, n), lambda b, s: (bwd_blk(b, s), 0)),
        ],
        out_specs=[
            pl.BlockSpec((c, n_heads * dv), lambda b, s: (fwd_blk(b, s), 0)),
            pl.BlockSpec((c, n_heads * dv), lambda b, s: (bwd_blk(b, s), 0)),
        ],
        out_shape=[out, out],
        scratch_shapes=[
            pltpu.VMEM((2 * n_heads, dk, dv), F32),
            pltpu.VMEM((2 * n_heads, c, c), F32),
            pltpu.VMEM((2 * n_heads, c, dk), BF16),
            pltpu.VMEM((2 * n_heads, c, dk), BF16),
        ],
        compiler_params=_params("parallel", "arbitrary"),
        name="retention_scan",
    )(decay_tile, qkv, qkv)


def _ret_gate_kernel(x_ref, wf_ref, wb_ref, of_ref, ob_ref, y_ref, wf_bf, wb_bf, *, n_split):
    _cast_weights_once([(wf_ref, wf_bf), (wb_ref, wb_bf)])
    for rows in _row_groups(y_ref, n_split):
        x = x_ref[rows, :]
        gf = jnp.dot(x, wf_bf[...], preferred_element_type=F32)
        gb = jnp.dot(x, wb_bf[...], preferred_element_type=F32)
        y = _silu(gf) * of_ref[rows, :].astype(F32) + _silu(gb) * ob_ref[rows, :].astype(F32)
        y_ref[rows, :] = y.astype(BF16)


def _ret_gate(u, w_g, layer, o_f, o_b, tm):
    t, d = u.shape
    n = w_g.shape[2] // 2
    tn = 512
    nb = n // tn
    return pl.pallas_call(
        functools.partial(_ret_gate_kernel, n_split=2),
        grid=(nb, t // tm),
        in_specs=[
            pl.BlockSpec((tm, d), lambda j, i: (i, 0)),
            pl.BlockSpec((None, d, tn), lambda j, i: (layer, 0, j)),
            pl.BlockSpec((None, d, tn), lambda j, i: (layer, 0, nb + j)),
            pl.BlockSpec((tm, tn), lambda j, i: (i, j)),
            pl.BlockSpec((tm, tn), lambda j, i: (i, j)),
        ],
        out_specs=pl.BlockSpec((tm, tn), lambda j, i: (i, j)),
        out_shape=jax.ShapeDtypeStruct((t, n), BF16),
        scratch_shapes=[pltpu.VMEM((d, tn), BF16), pltpu.VMEM((d, tn), BF16)],
        compiler_params=_params("parallel", "arbitrary"),
        name="ret_gate",
    )(u, w_g, w_g, o_f, o_b)


def _out_ln_kernel(*refs, alpha, n_k, n_split, group, n_x, n_lat_tiles):
    x_refs, refs = refs[:n_x], refs[n_x:]
    w_ref, h_ref, gate_ref, lng_ref, lnb_ref, sc_ref, sh_ref, hout_ref, u_ref, *acc = refs
    k = pl.program_id(1)
    is_lat = group * pl.program_id(0) + pl.program_id(2) < n_lat_tiles

    def x_rows(rows):
        if n_x == 1:
            return x_refs[0][rows, :]
        return jnp.where(is_lat, x_refs[0][rows, :], x_refs[1][rows, :])

    def partial_product(rows):
        return jnp.dot(x_rows(rows), w_ref[...], preferred_element_type=F32)

    def finish(rows, y):
        z = alpha * h_ref[rows, :] + gate_ref[0] * y
        mu = jnp.mean(z, axis=-1, keepdims=True)
        ctr = z - mu
        var = jnp.mean(ctr * ctr, axis=-1, keepdims=True)
        hn = ctr * lax.rsqrt(var + LN_EPS) * lng_ref[0] + lnb_ref[0]
        hout_ref[rows, :] = hn
        u_ref[rows, :] = (hn * (1.0 + sc_ref[0]) + sh_ref[0]).astype(BF16)

    def last_step(prev):
        for rows in _row_groups(hout_ref, n_split):
            y = partial_product(rows)
            finish(rows, y if prev is None else prev[rows, :] + y)

    if n_k == 1:
        last_step(None)
        return
    acc_ref = acc[0].at[pl.program_id(2)]

    @pl.when(k == 0)
    def _():
        acc_ref[...] = partial_product(slice(None))

    @pl.when(jnp.logical_and(k > 0, k < n_k - 1))
    def _():
        acc_ref[...] += partial_product(slice(None))

    @pl.when(k == n_k - 1)
    def _():
        last_step(acc_ref)


def _out_ln(x, w, layer, h, mod, ln_g, ln_b, *, gate_idx, ln_idx, sc_idx, sh_idx, alpha, n_rows, tm, tk,
            group, n_split=2):
    x_parts = x if isinstance(x, tuple) else (x,)
    kdim = x_parts[0].shape[1]
    d = w.shape[2]
    n_k = kdim // tk
    if n_rows % (group * tm):
        group = 1
    assert n_rows % (group * tm) == 0 and kdim % tk == 0
    n_lat_tiles = x_parts[0].shape[0] // tm if len(x_parts) == 2 else 0

    def row_blk(i, k, m):
        return group * i + jnp.where(k == n_k - 1, m, 0)

    if len(x_parts) == 1:
        x_specs = [pl.BlockSpec((tm, tk), lambda i, k, m: (group * i + m, k))]
    else:
        assert x_parts[0].shape[0] == n_lat_tiles * tm
        x_specs = [
            pl.BlockSpec((tm, tk), lambda i, k, m: (jnp.minimum(group * i + m, n_lat_tiles - 1), k)),
            pl.BlockSpec((tm, tk), lambda i, k, m: (jnp.maximum(group * i + m - n_lat_tiles, 0), k)),
        ]

    vec = lambda arr_idx: pl.BlockSpec((1, 1, d), lambda i, k, m: (arr_idx(row_blk(i, k, m)), 0, 0))
    row = pl.BlockSpec((tm, d), lambda i, k, m: (row_blk(i, k, m), 0))
    kern = functools.partial(_out_ln_kernel, alpha=alpha, n_k=n_k, n_split=n_split, group=group,
                             n_x=len(x_parts), n_lat_tiles=n_lat_tiles)
    return pl.pallas_call(
        kern,
        grid=(n_rows // (group * tm), n_k, group),
        in_specs=[
            *x_specs,
            pl.BlockSpec((None, tk, d), lambda i, k, m: (layer, k, 0)),
            row,
            vec(gate_idx),
            vec(lambda r: ln_idx),
            vec(lambda r: ln_idx),
            vec(sc_idx),
            vec(sh_idx),
        ],
        out_specs=[row, row],
        out_shape=[
            jax.ShapeDtypeStruct((n_rows, d), F32),
            jax.ShapeDtypeStruct((n_rows, d), BF16),
        ],
        scratch_shapes=[pltpu.VMEM((group, tm, d), F32)] if n_k > 1 else [],
        compiler_params=_params("parallel", "arbitrary", "arbitrary"),
        name="out_ln",
    )(*x_parts, w, h, mod, ln_g, ln_b, mod, mod)


def _ffn_in_kernel(x_ref, wa_ref, wb_ref, o_ref, wa_bf, wb_bf, *, n_split):
    _cast_weights_once([(wa_ref, wa_bf), (wb_ref, wb_bf)])
    for rows in _row_groups(o_ref, n_split):
        x = x_ref[rows, :]
        a = jnp.dot(x, wa_bf[...], preferred_element_type=F32)
        b = jnp.dot(x, wb_bf[...], preferred_element_type=F32)
        o_ref[rows, :] = (_silu(a) * b).astype(BF16)


def _ffn_in(u, w_in, layer, n_rows, tm):
    d = u.shape[1]
    f = w_in.shape[2] // 2
    tn = 512
    nb = f // tn
    return pl.pallas_call(
        functools.partial(_ffn_in_kernel, n_split=2),
        grid=(nb, n_rows // tm),
        in_specs=[
            pl.BlockSpec((tm, d), lambda j, i: (i, 0)),
            pl.BlockSpec((None, d, tn), lambda j, i: (layer, 0, j)),
            pl.BlockSpec((None, d, tn), lambda j, i: (layer, 0, nb + j)),
        ],
        out_specs=pl.BlockSpec((tm, tn), lambda j, i: (i, j)),
        out_shape=jax.ShapeDtypeStruct((n_rows, f), BF16),
        scratch_shapes=[pltpu.VMEM((d, tn), BF16), pltpu.VMEM((d, tn), BF16)],
        compiler_params=_params("parallel", "arbitrary"),
        name="ffn_in",
    )(u, w_in, w_in)


def _rope_lanes(x, c_ref, s_up_ref, s_dn_ref):
    half = MLA_D_ROPE // 4
    up = pltpu.roll(x, LANES - half, 1)
    dn = pltpu.roll(x, half, 1)
    return x * c_ref[...] + up * s_up_ref[...] + dn * s_dn_ref[...]


def _rms(x, g):
    return x * lax.rsqrt(jnp.mean(x * x, axis=-1, keepdims=True) + RMS_EPS) * g


def _mla_down_kernel(x_ref, w_ref, gq_ref, gkv_ref, c_ref, su_ref, sd_ref,
                     cq_ref, ckv_ref, kr_ref, *, rq, rkv):
    for rows in _row_groups(x_ref, 2):
        acc = jnp.dot(x_ref[rows, :], w_ref[...], preferred_element_type=F32)
        cq_ref[rows, :] = _rms(acc[:, :rq], gq_ref[...]).astype(BF16)
        ckv_ref[rows, :] = _rms(acc[:, rq:rq + rkv], gkv_ref[...]).astype(BF16)
        kr_ref[rows, :] = _rope_lanes(acc[:, rq + rkv:], c_ref.at[rows, :], su_ref.at[rows, :],
                                      sd_ref.at[rows, :]).astype(BF16)


def _mla_down(u, w_down, g_q, g_kv, tabs, tm):
    t, d = u.shape
    rq, rkv = g_q.shape[1], g_kv.shape[1]
    n = w_down.shape[1]
    tab = pl.BlockSpec((tm, LANES), lambda i: (i, 0))
    kern = functools.partial(_mla_down_kernel, rq=rq, rkv=rkv)
    return pl.pallas_call(
        kern,
        grid=(t // tm,),
        in_specs=[
            pl.BlockSpec((tm, d), lambda i: (i, 0)),
            pl.BlockSpec((d, n), lambda i: (0, 0)),
            pl.BlockSpec((1, rq), lambda i: (0, 0)),
            pl.BlockSpec((1, rkv), lambda i: (0, 0)),
            tab, tab, tab,
        ],
        out_specs=[
            pl.BlockSpec((tm, rq), lambda i: (i, 0)),
            pl.BlockSpec((tm, rkv), lambda i: (i, 0)),
            pl.BlockSpec((tm, LANES), lambda i: (i, 0)),
        ],
        out_shape=[
            jax.ShapeDtypeStruct((t, rq), BF16),
            jax.ShapeDtypeStruct((t, rkv), BF16),
            jax.ShapeDtypeStruct((t, LANES), BF16),
        ],
        compiler_params=_params("parallel"),
        name="mla_down",
    )(u, w_down, g_q, g_kv, *tabs)


def _mla_up_kernel(cq_ref, ckv_ref, kr_ref, wq_ref, wkv_ref, c_ref, su_ref, sd_ref,
                   q_ref, k_ref, v_ref):
    cq = cq_ref[...]
    ckv = ckv_ref[...]
    kr = kr_ref[...]
    hp = MLA_HEAD_PAD
    for h in range(MLA_HEADS):
        qh = jnp.dot(cq, wq_ref[:, h * hp:(h + 1) * hp], preferred_element_type=F32) * MLA_Q_SCALE
        q_ref[:, h * hp:h * hp + LANES] = qh[:, :LANES].astype(BF16)
        q_ref[:, h * hp + LANES:(h + 1) * hp] = _rope_lanes(qh[:, LANES:], c_ref, su_ref, sd_ref).astype(BF16)
        kvh = jnp.dot(ckv, wkv_ref[:, h * hp:(h + 1) * hp], preferred_element_type=F32)
        k_ref[:, h * hp:h * hp + LANES] = kvh[:, :LANES].astype(BF16)
        k_ref[:, h * hp + LANES:(h + 1) * hp] = kr
        v_ref[:, h * hp:h * hp + MLA_D_V] = kvh[:, LANES:].astype(BF16)
        v_ref[:, h * hp + MLA_D_V:(h + 1) * hp] = jnp.ones((kvh.shape[0], hp - MLA_D_V), BF16)


def _mla_up(cq, ckv, kr, wq_pad, wkv, tabs, tm):
    t, rq = cq.shape
    rkv = ckv.shape[1]
    nq = wq_pad.shape[1]
    nkv = wkv.shape[1]
    tab = pl.BlockSpec((tm, LANES), lambda i: (i, 0))
    return pl.pallas_call(
        _mla_up_kernel,
        grid=(t // tm,),
        in_specs=[
            pl.BlockSpec((tm, rq), lambda i: (i, 0)),
            pl.BlockSpec((tm, rkv), lambda i: (i, 0)),
            pl.BlockSpec((tm, LANES), lambda i: (i, 0)),
            pl.BlockSpec((rq, nq), lambda i: (0, 0)),
            pl.BlockSpec((rkv, nkv), lambda i: (0, 0)),
            tab, tab, tab,
        ],
        out_specs=[
            pl.BlockSpec((tm, nq), lambda i: (i, 0)),
            pl.BlockSpec((tm, nq), lambda i: (i, 0)),
            pl.BlockSpec((tm, nq), lambda i: (i, 0)),
        ],
        out_shape=[
            jax.ShapeDtypeStruct((t, nq), BF16),
            jax.ShapeDtypeStruct((t, nq), BF16),
            jax.ShapeDtypeStruct((t, nq), BF16),
        ],
        compiler_params=_params("parallel"),
        name="mla_up",
    )(cq, ckv, kr, wq_pad, wkv, *tabs)


def _scores(q, k):
    return lax.dot_general(q, k, (((1,), (1,)), ((), ())), preferred_element_type=F32)


def _attn_lat_kernel(q_ref, kl_ref, kc_ref, vl_ref, vc_ref, o_ref, *, key_chunk):
    q = q_ref[...]
    n_lat = kl_ref.shape[0] // key_chunk
    chunks = [(kl_ref, vl_ref, c * key_chunk, key_chunk) for c in range(n_lat)]
    chunks.append((kc_ref, vc_ref, 0, kc_ref.shape[0]))

    def chunk_scores(c):
        k_ref, _, lo, size = chunks[c]
        return _scores(q, k_ref[lo:lo + size, :])

    s = chunk_scores(0)
    m = acc = None
    for c, (_, v_ref, lo, size) in enumerate(chunks):
        s_next = chunk_scores(c + 1) if c + 1 < len(chunks) else None
        m_c = jnp.max(s, axis=-1, keepdims=True)
        v = v_ref[lo:lo + size, :]
        if c == 0:
            m = m_c
            acc = jnp.dot(jnp.exp2(s - m).astype(BF16), v, preferred_element_type=F32)
        else:
            m_new = jnp.maximum(m, m_c)
            acc = (jnp.exp2(m - m_new) * acc
                   + jnp.dot(jnp.exp2(s - m_new).astype(BF16), v, preferred_element_type=F32))
            m = m_new
        s = s_next
    _store_normalised(o_ref, acc)


def _store_normalised(o_ref, acc):
    o_ref[...] = (acc[:, :MLA_D_V] / acc[:, MLA_D_V:MLA_D_V + 1]).astype(BF16)


def _attn_ctx_kernel(q_ref, kc_ref, vc_ref, o_ref):
    s_c = _scores(q_ref[...], kc_ref[...])
    p_c = jnp.exp2(s_c - jnp.max(s_c, axis=-1, keepdims=True))
    _store_normalised(o_ref, jnp.dot(p_c.astype(BF16), vc_ref[...], preferred_element_type=F32))


def _attention(q, k, v, batch, seq, ctx_len, with_ctx_queries):
    hp, dv = MLA_HEAD_PAD, MLA_D_V
    tq = min(seq, ATTN_Q_TILE)
    nq = seq // tq
    ctx0 = batch * seq // ctx_len
    o_lat = pl.pallas_call(
        functools.partial(_attn_lat_kernel, key_chunk=min(seq, ATTN_KEY_CHUNK)),
        grid=(batch, MLA_HEADS, nq),
        in_specs=[
            pl.BlockSpec((tq, hp), lambda b, h, i: (b * nq + i, h)),
            pl.BlockSpec((seq, hp), lambda b, h, i: (b, h)),
            pl.BlockSpec((ctx_len, hp), lambda b, h, i: (ctx0 + b, h)),
            pl.BlockSpec((seq, hp), lambda b, h, i: (b, h)),
            pl.BlockSpec((ctx_len, hp), lambda b, h, i: (ctx0 + b, h)),
        ],
        out_specs=pl.BlockSpec((tq, dv), lambda b, h, i: (b * nq + i, h)),
        out_shape=jax.ShapeDtypeStruct((batch * seq, MLA_HEADS * dv), BF16),
        compiler_params=_params("parallel", "parallel", "arbitrary"),
        name="attn_lat",
    )(q, k, k, v, v)
    if not with_ctx_queries:
        return o_lat
    o_ctx = pl.pallas_call(
        _attn_ctx_kernel,
        grid=(batch, MLA_HEADS),
        in_specs=[
            pl.BlockSpec((ctx_len, hp), lambda b, h: (ctx0 + b, h)),
            pl.BlockSpec((ctx_len, hp), lambda b, h: (ctx0 + b, h)),
            pl.BlockSpec((ctx_len, hp), lambda b, h: (ctx0 + b, h)),
        ],
        out_specs=pl.BlockSpec((ctx_len, dv), lambda b, h: (b, h)),
        out_shape=jax.ShapeDtypeStruct((batch * ctx_len, MLA_HEADS * dv), BF16),
        compiler_params=_params("parallel", "parallel"),
        name="attn_ctx",
    )(q, k, v)
    return o_lat, o_ctx


def _retention_tables(seq, n_ctx, dk):
    inv = RET_ROPE_BASE ** (-jnp.linspace(0.0, 1.0, dk // 2, dtype=F32))
    ang = jnp.arange(seq, dtype=F32)[:, None] * inv[None, :]
    return jnp.cos(ang), jnp.sin(ang), jnp.ones((n_ctx, dk // 2), F32), jnp.zeros((n_ctx, dk // 2), F32)


def _axial_tables(seq):
    n_f = MLA_D_ROPE // 4
    inv = AXIAL_ROPE_BASE ** (-jnp.arange(n_f, dtype=F32) * 2.0 / (MLA_D_ROPE // 2))
    tpos = jnp.arange(seq)
    row = (tpos // GRID_W).astype(F32)[:, None] * inv[None, :]
    col = (tpos % GRID_W).astype(F32)[:, None] * inv[None, :]
    z = jnp.zeros((seq, n_f), F32)
    tail = LANES - MLA_D_ROPE
    cos = jnp.concatenate([jnp.cos(row), jnp.cos(row), jnp.cos(col), jnp.cos(col), jnp.ones((seq, tail), F32)], 1)
    s_up = jnp.concatenate([-jnp.sin(row), z, -jnp.sin(col), z, jnp.zeros((seq, tail), F32)], 1)
    s_dn = jnp.concatenate([z, jnp.sin(row), z, jnp.sin(col), jnp.zeros((seq, tail), F32)], 1)
    return cos, s_up, s_dn


def _token_table(lat, batch, ctx_rows, fill):
    return jnp.concatenate([jnp.tile(lat, (batch, 1)), jnp.full((ctx_rows, lat.shape[1]), fill, F32)], 0)


def kernel(x, c, ctx, c_ctx, ada_w, ada_b, ln_g, ln_b, ret_w_qkv, ret_w_g, ret_decay_logit, ret_w_o,
           mla_w_dq, mla_g_q, mla_w_uq, mla_w_dkv, mla_g_kv, mla_w_ukv, mla_w_o, ffn_w_in, ffn_w_out):
    batch, seq, d = x.shape
    ctx_len = ctx.shape[1]
    depth = ada_w.shape[0]
    n_lat, n_ctx = batch * seq, batch * ctx_len
    n_tok = n_lat + n_ctx
    ret_heads = ret_decay_logit.shape[-1]
    assert ctx_len == RET_CHUNK and seq % RET_CHUNK == 0 and batch + 1 <= 8
    tm = _row_tile(n_lat, n_ctx)
    alpha = (2 * depth) ** 0.25

    cond8 = jnp.concatenate([c, c_ctx[None, :], jnp.zeros((8 - batch - 1, d), F32)], 0)
    mod = _modulation(cond8, ada_w, ada_b).reshape(depth * 8 * 6, 1, d)

    def mod_idx(layer, chunk, tile):
        def f(i):
            r = jnp.where(i * tile >= n_lat, batch, (i * tile) // seq)
            return (layer * 8 + r) * 6 + chunk
        return f

    ln_g3 = ln_g.reshape(depth * 2, 1, d)
    ln_b3 = ln_b.reshape(depth * 2, 1, d)

    ret_cos, ret_sin, one_c, zero_c = _retention_tables(seq, n_ctx, d // ret_heads)
    ret_cos = jnp.concatenate([jnp.tile(ret_cos, (batch, 1)), one_c], 0)
    ret_sin = jnp.concatenate([jnp.tile(ret_sin, (batch, 1)), zero_c], 0)
    ax_cos, ax_up, ax_dn = _axial_tables(seq)
    ax_tabs = (_token_table(ax_cos, batch, n_ctx, 1.0), _token_table(ax_up, batch, n_ctx, 0.0),
               _token_table(ax_dn, batch, n_ctx, 0.0))

    h, u = _modulate(x.reshape(n_lat, d), ctx.reshape(n_ctx, d), mod,
                     mod_idx(0, 1, TM_SMALL), mod_idx(0, 0, TM_SMALL), TM_SMALL)

    ret_w_o_b, mla_w_o_b, ffn_w_out_b = (w.astype(BF16) for w in (ret_w_o, mla_w_o, ffn_w_out))

    for i in range(depth):
        last = i == depth - 1
        n_rows = n_lat if last else n_tok
        j = i // 2
        if i % 2 == 0:
            qkv = _ret_qkv(u, ret_w_qkv, j, ret_cos, ret_sin, ret_heads, tm)
            decay_tile = jnp.broadcast_to(
                ret_decay_logit[j].astype(F32).reshape(2 * ret_heads, 1, 1), (2 * ret_heads, 8, LANES))
            o_f, o_b = _retention(qkv, decay_tile, batch, seq, ret_heads)
            y = _ret_gate(u, ret_w_g, j, o_f, o_b, tm)
            w_o = ret_w_o_b
        else:
            rq, rkv = mla_g_q.shape[1], mla_g_kv.shape[1]
            w_down = jnp.concatenate(
                [mla_w_dq[j], mla_w_dkv[j], jnp.zeros((d, LANES - MLA_D_ROPE), F32)], 1).astype(BF16)
            cq, ckv, kr = _mla_down(u, w_down, mla_g_q[j][None, :], mla_g_kv[j][None, :], ax_tabs, TM_SMALL)
            wq_pad = jnp.pad(
                mla_w_uq[j].reshape(rq, MLA_HEADS, MLA_D_NOPE + MLA_D_ROPE),
                ((0, 0), (0, 0), (0, MLA_HEAD_PAD - MLA_D_NOPE - MLA_D_ROPE)),
            ).reshape(rq, MLA_HEADS * MLA_HEAD_PAD).astype(BF16)
            q, k, v = _mla_up(cq, ckv, kr, wq_pad, mla_w_ukv[j].astype(BF16), ax_tabs, TM_SMALL)
            y = _attention(q, k, v, batch, seq, ctx_len, not last)
            w_o = mla_w_o_b

        tm_o, grp_o, tk_o = OUT_TILES_MIXER
        h, u = _out_ln(y, w_o, j, h, mod, ln_g3, ln_b3, gate_idx=mod_idx(i, 2, tm_o), ln_idx=2 * i,
                       sc_idx=mod_idx(i, 4, tm_o), sh_idx=mod_idx(i, 3, tm_o), alpha=alpha,
                       n_rows=n_rows, tm=tm_o, tk=min(tk_o, w_o.shape[1]), group=grp_o)
        hid = _ffn_in(u, ffn_w_in, i, n_rows, tm)
        nxt = min(i + 1, depth - 1)
        tm_o, grp_o, k_steps = OUT_TILES_FFN
        h, u = _out_ln(hid, ffn_w_out_b, i, h, mod, ln_g3, ln_b3, gate_idx=mod_idx(i, 5, tm_o),
                       ln_idx=2 * i + 1, sc_idx=mod_idx(nxt, 1, tm_o), sh_idx=mod_idx(nxt, 0, tm_o),
                       alpha=alpha, n_rows=n_rows, tm=tm_o, tk=ffn_w_out.shape[1] // k_steps, group=grp_o)
    return h.reshape(batch, seq, d)
```

```python
import functools

import jax
import jax.numpy as jnp
from jax import lax
from jax.experimental import pallas as pl
from jax.experimental.pallas import tpu as pltpu

F32 = jnp.float32
BF16 = jnp.bfloat16

GRID_W = 64
RET_ROPE_BASE = 10000.0
GN_EPS = 1e-6
MLA_HEADS = 16
MLA_D_NOPE = 128
MLA_D_ROPE = 64
MLA_D_V = 128
MLA_SCALE = (MLA_D_NOPE + MLA_D_ROPE) ** -0.5
MLA_Q_SCALE = MLA_SCALE * 1.4426950408889634
AXIAL_ROPE_BASE = 10000.0
RMS_EPS = 1e-6
LN_EPS = 1e-5

LANES = 128
MXU_DIM = 256
VMEM_LIMIT_BYTES = 56 * 1024 * 1024

RET_CHUNK = 256

TM_SMALL = 512
OUT_TILES_MIXER = (512, 2, 1024)
OUT_K_WHOLE = 2048
OUT_TILES_FFN = (512, 2, 4)
ATTN_Q_TILE = 1024
ATTN_KEY_CHUNK = 2048
MLA_HEAD_PAD = 2 * LANES


def _params(*sem):
    return pltpu.CompilerParams(dimension_semantics=sem, vmem_limit_bytes=VMEM_LIMIT_BYTES)


def _silu(x):
    return 0.5 * x * (1.0 + jnp.tanh(0.5 * x))


def _row_tile(n_lat, n_ctx):
    for t in (1024, 512, 256):
        if n_lat % t == 0 and n_ctx % t == 0:
            return t
    raise ValueError("token counts must be multiples of 256")


def _mod_kernel(cond_ref, w_ref, b_ref, o_ref):
    s = _silu(cond_ref[...]).astype(BF16)
    acc = jnp.dot(s, w_ref[0].astype(BF16), preferred_element_type=F32)
    o_ref[0] = acc + b_ref[0]


def _modulation(cond8, ada_w, ada_b):
    depth, d, n = ada_w.shape
    tn = 1024
    return pl.pallas_call(
        _mod_kernel,
        grid=(depth, n // tn),
        in_specs=[
            pl.BlockSpec((8, d), lambda l, j: (0, 0)),
            pl.BlockSpec((1, d, tn), lambda l, j: (l, 0, j)),
            pl.BlockSpec((1, 1, tn), lambda l, j: (l, 0, j)),
        ],
        out_specs=pl.BlockSpec((1, 8, tn), lambda l, j: (l, 0, j)),
        out_shape=jax.ShapeDtypeStruct((depth, 8, n), F32),
        compiler_params=_params("parallel", "parallel"),
        name="adaln_mod",
    )(cond8, ada_w, ada_b.reshape(depth, 1, n))


def _modulate_kernel(x_ref, ctx_ref, sc_ref, sh_ref, h_ref, u_ref, *, n_lat_tiles):
    def emit(src_ref):
        hv = src_ref[...]
        h_ref[...] = hv
        u_ref[...] = (hv * (1.0 + sc_ref[0]) + sh_ref[0]).astype(BF16)

    is_lat = pl.program_id(0) < n_lat_tiles
    pl.when(is_lat)(lambda: emit(x_ref))
    pl.when(jnp.logical_not(is_lat))(lambda: emit(ctx_ref))


def _modulate(x2d, ctx2d, mod, sc_idx, sh_idx, tm):
    n_lat, d = x2d.shape
    n_ctx = ctx2d.shape[0]
    t = n_lat + n_ctx
    n_lat_tiles = n_lat // tm
    vec = lambda f: pl.BlockSpec((1, 1, d), lambda i: (f(i), 0, 0))
    row = pl.BlockSpec((tm, d), lambda i: (i, 0))
    return pl.pallas_call(
        functools.partial(_modulate_kernel, n_lat_tiles=n_lat_tiles),
        grid=(t // tm,),
        in_specs=[
            pl.BlockSpec((tm, d), lambda i: (jnp.minimum(i, n_lat_tiles - 1), 0)),
            pl.BlockSpec((tm, d), lambda i: (jnp.maximum(i - n_lat_tiles, 0), 0)),
            vec(sc_idx), vec(sh_idx),
        ],
        out_specs=[row, row],
        out_shape=[jax.ShapeDtypeStruct((t, d), F32), jax.ShapeDtypeStruct((t, d), BF16)],
        compiler_params=_params("arbitrary"),
        name="modulate_in",
    )(x2d, ctx2d, mod, mod)


def _cast_weights_once(pairs):
    @pl.when(pl.program_id(1) == 0)
    def _():
        for src_ref, dst_ref in pairs:
            dst_ref[...] = src_ref[...].astype(BF16)


def _row_groups(ref, n_split):
    rows_per = ref.shape[0] // n_split
    return [pl.ds(r * rows_per, rows_per) for r in range(n_split)]


def _ret_qkv_kernel(x_ref, w_ref, cos_ref, sin_ref, o_ref, wb_ref, *, n_q, n_k, k_scale, n_split):
    j = pl.program_id(0)
    tn = o_ref.shape[1]
    _cast_weights_once([(w_ref, wb_ref)])

    def product(rows):
        return jnp.dot(x_ref[rows, :], wb_ref[...], preferred_element_type=F32)

    @pl.when(j < n_q + n_k)
    def _():
        scale = jnp.where(j < n_q, 1.0, k_scale).astype(F32)
        for rows in _row_groups(o_ref, n_split):
            acc = product(rows)
            c = cos_ref[rows, :]
            s = sin_ref[rows, :]
            for g in range(tn // MXU_DIM):
                lo = g * MXU_DIM
                x1 = acc[:, lo:lo + LANES] * scale
                x2 = acc[:, lo + LANES:lo + MXU_DIM] * scale
                o_ref[rows, lo:lo + LANES] = (x1 * c - x2 * s).astype(BF16)
                o_ref[rows, lo + LANES:lo + MXU_DIM] = (x1 * s + x2 * c).astype(BF16)

    @pl.when(j >= n_q + n_k)
    def _():
        for rows in _row_groups(o_ref, n_split):
            o_ref[rows, :] = product(rows).astype(BF16)


def _ret_qkv(u, w, layer, cos, sin, n_heads, tm):
    t, d = u.shape
    n = w.shape[2]
    dk = d // n_heads
    tn = 1024
    n_q = n_heads * dk // tn
    kern = functools.partial(_ret_qkv_kernel, n_q=n_q, n_k=n_q, k_scale=dk ** -0.5, n_split=2)
    return pl.pallas_call(
        kern,
        grid=(n // tn, t // tm),
        in_specs=[
            pl.BlockSpec((tm, d), lambda j, i: (i, 0)),
            pl.BlockSpec((None, d, tn), lambda j, i: (layer, 0, j)),
            pl.BlockSpec((tm, LANES), lambda j, i: (i, 0)),
            pl.BlockSpec((tm, LANES), lambda j, i: (i, 0)),
        ],
        out_specs=pl.BlockSpec((tm, tn), lambda j, i: (i, j)),
        out_shape=jax.ShapeDtypeStruct((t, n), BF16),
        scratch_shapes=[pltpu.VMEM((d, tn), BF16)],
        compiler_params=_params("parallel", "arbitrary"),
        name="ret_qkv",
    )(u, w, cos, sin)


def _retention_kernel(dec_ref, qf_ref, qb_ref, of_ref, ob_ref, s_ref, intra_ref, qdec_ref, kdec_ref,
                      *, n_heads, dk, dv):
    c = RET_CHUNK
    step = pl.program_id(1)

    def log_gamma(idx):
        x = dec_ref[idx][0:1, 0:1]
        return jnp.minimum(x, 0.0) - jnp.log1p(jnp.exp(-jnp.abs(x)))

    @pl.when(step == 0)
    def _():
        s_ref[...] = jnp.zeros_like(s_ref)
        row = lax.broadcasted_iota(jnp.int32, (c, c), 0)
        col = lax.broadcasted_iota(jnp.int32, (c, c), 1)
        pos = lax.broadcasted_iota(jnp.int32, (c, dk), 0).astype(F32)
        for d in range(2):
            diff = (row - col) if d == 0 else (col - row)
            dist = jnp.maximum(diff, 0).astype(F32)
            ahead = pos + 1.0 if d == 0 else float(c) - pos
            behind = float(c - 1) - pos if d == 0 else pos
            for h in range(n_heads):
                idx = d * n_heads + h
                lg = log_gamma(idx)
                intra_ref[idx] = jnp.where(diff >= 0, jnp.exp(lg * dist), 0.0)
                qdec_ref[idx] = jnp.exp(lg * ahead).astype(BF16)
                kdec_ref[idx] = jnp.exp(lg * behind).astype(BF16)

    for d, (x_ref, o_ref) in enumerate(((qf_ref, of_ref), (qb_ref, ob_ref))):
        for h in range(n_heads):
            idx = d * n_heads + h
            q = x_ref[:, h * dk:(h + 1) * dk]
            k = x_ref[:, n_heads * dk + h * dk:n_heads * dk + (h + 1) * dk]
            v = x_ref[:, 2 * n_heads * dk + h * dv:2 * n_heads * dk + (h + 1) * dv]
            scores = lax.dot_general(q, k, (((1,), (1,)), ((), ())), preferred_element_type=F32)
            p = (scores * intra_ref[idx]).astype(BF16)
            state = s_ref[idx]
            o = (jnp.dot(p, v, preferred_element_type=F32)
                 + jnp.dot(q * qdec_ref[idx], state.astype(BF16), preferred_element_type=F32))
            s_ref[idx] = state * jnp.exp(log_gamma(idx) * float(c)) + lax.dot_general(
                k * kdec_ref[idx], v, (((0,), (0,)), ((), ())), preferred_element_type=F32)
            mu = jnp.mean(o, axis=-1, keepdims=True)
            ctr = o - mu
            var = jnp.mean(ctr * ctr, axis=-1, keepdims=True)
            o_ref[:, h * dv:(h + 1) * dv] = (ctr * lax.rsqrt(var + GN_EPS)).astype(BF16)


def _retention(qkv, decay_tile, batch, seq, n_heads):
    t, n = qkv.shape
    d = n // 4
    dk, dv = d // n_heads, 2 * d // n_heads
    c = RET_CHUNK
    n_lat = seq // c
    ctx_blk = batch * n_lat

    def fwd_blk(b, s):
        return jnp.where(s == 0, ctx_blk + b, b * n_lat + s - 1)

    def bwd_blk(b, s):
        return jnp.where(s == 0, ctx_blk + b, b * n_lat + n_lat - s)

    kern = functools.partial(_retention_kernel, n_heads=n_heads, dk=dk, dv=dv)
    out = jax.ShapeDtypeStruct((t, n_heads * dv), BF16)
    return pl.pallas_call(
        kern,
        grid=(batch, n_lat + 1),
        in_specs=[
            pl.BlockSpec((2 * n_heads, 8, LANES), lambda b, s: (0, 0, 0)),
            pl.BlockSpec((c, n), lambda b, s: (fwd_blk(b, s), 0)),
            pl.BlockSpec((c, n), lambda b, s: (bwd_blk(b, s), 0)),
        ],
        out_specs=[
            pl.BlockSpec((c, n_heads * dv), lambda b, s: (fwd_blk(b, s), 0)),
            pl.BlockSpec((c, n_heads * dv), lambda b, s: (bwd_blk(b, s), 0)),
        ],
        out_shape=[out, out],
        scratch_shapes=[
            pltpu.VMEM((2 * n_heads, dk, dv), F32),
            pltpu.VMEM((2 * n_heads, c, c), F32),
            pltpu.VMEM((2 * n_heads, c, dk), BF16),
            pltpu.VMEM((2 * n_heads, c, dk), BF16),
        ],
        compiler_params=_params("parallel", "arbitrary"),
        name="retention_scan",
    )(decay_tile, qkv, qkv)


def _ret_gate_kernel(x_ref, wf_ref, wb_ref, of_ref, ob_ref, y_ref, wf_bf, wb_bf, *, n_split):
    _cast_weights_once([(wf_ref, wf_bf), (wb_ref, wb_bf)])
    for rows in _row_groups(y_ref, n_split):
        x = x_ref[rows, :]
        gf = jnp.dot(x, wf_bf[...], preferred_element_type=F32)
        gb = jnp.dot(x, wb_bf[...], preferred_element_type=F32)
        y = _silu(gf) * of_ref[rows, :].astype(F32) + _silu(gb) * ob_ref[rows, :].astype(F32)
        y_ref[rows, :] = y.astype(BF16)


def _ret_gate(u, w_g, layer, o_f, o_b, tm):
    t, d = u.shape
    n = w_g.shape[2] // 2
    tn = 512
    nb = n // tn
    return pl.pallas_call(
        functools.partial(_ret_gate_kernel, n_split=2),
        grid=(nb, t // tm),
        in_specs=[
            pl.BlockSpec((tm, d), lambda j, i: (i, 0)),
            pl.BlockSpec((None, d, tn), lambda j, i: (layer, 0, j)),
            pl.BlockSpec((None, d, tn), lambda j, i: (layer, 0, nb + j)),
            pl.BlockSpec((tm, tn), lambda j, i: (i, j)),
            pl.BlockSpec((tm, tn), lambda j, i: (i, j)),
        ],
        out_specs=pl.BlockSpec((tm, tn), lambda j, i: (i, j)),
        out_shape=jax.ShapeDtypeStruct((t, n), BF16),
        scratch_shapes=[pltpu.VMEM((d, tn), BF16), pltpu.VMEM((d, tn), BF16)],
        compiler_params=_params("parallel", "arbitrary"),
        name="ret_gate",
    )(u, w_g, w_g, o_f, o_b)


def _out_ln_kernel(*refs, alpha, n_k, n_split, n_x, n_lat_blocks):
    x_refs, refs = refs[:n_x], refs[n_x:]
    w_ref, h_ref, gate_ref, lng_ref, lnb_ref, sc_ref, sh_ref, hout_ref, u_ref, *acc = refs
    s = pl.program_id(1)
    tm = hout_ref.shape[0]
    is_lat = pl.program_id(0) < n_lat_blocks

    def x_rows(rows):
        if n_x == 1:
            return x_refs[0][rows, :]
        return jnp.where(is_lat, x_refs[0][rows, :], x_refs[1][rows, :])

    def partial_product(rows):
        return jnp.dot(x_rows(rows), w_ref[...], preferred_element_type=F32)

    def finish(rows, y):
        z = alpha * h_ref[rows, :] + gate_ref[0] * y
        mu = jnp.mean(z, axis=-1, keepdims=True)
        ctr = z - mu
        var = jnp.mean(ctr * ctr, axis=-1, keepdims=True)
        hn = ctr * lax.rsqrt(var + LN_EPS) * lng_ref[0] + lnb_ref[0]
        hout_ref[rows, :] = hn
        u_ref[rows, :] = (hn * (1.0 + sc_ref[0]) + sh_ref[0]).astype(BF16)

    if n_k > 1:
        acc_ref, = acc

        @pl.when(s == 0)
        def _():
            acc_ref[...] = partial_product(slice(None))

        @pl.when(jnp.logical_and(s > 0, s < n_k - 1))
        def _():
            acc_ref[...] += partial_product(slice(None))

    @pl.when(s >= n_k - 1)
    def _():
        first_row = (s - (n_k - 1)) * tm
        rows_per = tm // n_split
        for r in range(n_split):
            src = pl.ds(pl.multiple_of(first_row + r * rows_per, rows_per), rows_per)
            y = partial_product(src)
            if n_k > 1:
                y = acc_ref[src, :] + y
            finish(pl.ds(r * rows_per, rows_per), y)


def _out_ln(x, w, layer, h, mod, ln_g, ln_b, *, gate_idx, ln_idx, sc_idx, sh_idx, alpha, n_rows, tm, tk,
            group, n_split=2):
    x_parts = x if isinstance(x, tuple) else (x,)
    kdim = x_parts[0].shape[1]
    d = w.shape[2]
    n_k = kdim // tk
    if n_rows % (group * tm):
        group = 1
    rows_blk = group * tm
    assert n_rows % rows_blk == 0 and kdim % tk == 0

    def k_blk(s):
        return jnp.minimum(s, n_k - 1)

    def tile(i, s):
        return group * i + jnp.maximum(s - (n_k - 1), 0)

    if len(x_parts) == 1:
        n_lat_blocks = 0
        x_specs = [pl.BlockSpec((rows_blk, tk), lambda i, s: (i, k_blk(s)))]
    else:
        n_lat_blocks = x_parts[0].shape[0] // rows_blk
        assert x_parts[0].shape[0] == n_lat_blocks * rows_blk and x_parts[1].shape[0] % rows_blk == 0
        x_specs = [
            pl.BlockSpec((rows_blk, tk), lambda i, s: (jnp.minimum(i, n_lat_blocks - 1), k_blk(s))),
            pl.BlockSpec((rows_blk, tk), lambda i, s: (jnp.maximum(i - n_lat_blocks, 0), k_blk(s))),
        ]

    vec = lambda arr_idx: pl.BlockSpec((1, 1, d), lambda i, s: (arr_idx(tile(i, s)), 0, 0))
    row = pl.BlockSpec((tm, d), lambda i, s: (tile(i, s), 0))
    kern = functools.partial(_out_ln_kernel, alpha=alpha, n_k=n_k, n_split=n_split,
                             n_x=len(x_parts), n_lat_blocks=n_lat_blocks)
    return pl.pallas_call(
        kern,
        grid=(n_rows // rows_blk, n_k - 1 + group),
        in_specs=[
            *x_specs,
            pl.BlockSpec((None, tk, d), lambda i, s: (layer, k_blk(s), 0)),
            row,
            vec(gate_idx),
            vec(lambda r: ln_idx),
            vec(lambda r: ln_idx),
            vec(sc_idx),
            vec(sh_idx),
        ],
        out_specs=[row, row],
        out_shape=[
            jax.ShapeDtypeStruct((n_rows, d), F32),
            jax.ShapeDtypeStruct((n_rows, d), BF16),
        ],
        scratch_shapes=[pltpu.VMEM((rows_blk, d), F32)] if n_k > 1 else [],
        compiler_params=_params("parallel", "arbitrary"),
        name="out_ln",
    )(*x_parts, w, h, mod, ln_g, ln_b, mod, mod)


def _ffn_in_kernel(x_ref, wa_ref, wb_ref, o_ref, wa_bf, wb_bf, *, n_split):
    _cast_weights_once([(wa_ref, wa_bf), (wb_ref, wb_bf)])
    for rows in _row_groups(o_ref, n_split):
        x = x_ref[rows, :]
        a = jnp.dot(x, wa_bf[...], preferred_element_type=F32)
        b = jnp.dot(x, wb_bf[...], preferred_element_type=F32)
        o_ref[rows, :] = (_silu(a) * b).astype(BF16)


def _ffn_in(u, w_in, layer, n_rows, tm):
    d = u.shape[1]
    f = w_in.shape[2] // 2
    tn = 512
    nb = f // tn
    return pl.pallas_call(
        functools.partial(_ffn_in_kernel, n_split=2),
        grid=(nb, n_rows // tm),
        in_specs=[
            pl.BlockSpec((tm, d), lambda j, i: (i, 0)),
            pl.BlockSpec((None, d, tn), lambda j, i: (layer, 0, j)),
            pl.BlockSpec((None, d, tn), lambda j, i: (layer, 0, nb + j)),
        ],
        out_specs=pl.BlockSpec((tm, tn), lambda j, i: (i, j)),
        out_shape=jax.ShapeDtypeStruct((n_rows, f), BF16),
        scratch_shapes=[pltpu.VMEM((d, tn), BF16), pltpu.VMEM((d, tn), BF16)],
        compiler_params=_params("parallel", "arbitrary"),
        name="ffn_in",
    )(u, w_in, w_in)


def _rope_lanes(x, c_ref, s_up_ref, s_dn_ref):
    half = MLA_D_ROPE // 4
    up = pltpu.roll(x, LANES - half, 1)
    dn = pltpu.roll(x, half, 1)
    return x * c_ref[...] + up * s_up_ref[...] + dn * s_dn_ref[...]


def _rms(x, g):
    return x * lax.rsqrt(jnp.mean(x * x, axis=-1, keepdims=True) + RMS_EPS) * g


def _mla_down_kernel(x_ref, w_ref, gq_ref, gkv_ref, c_ref, su_ref, sd_ref,
                     cq_ref, ckv_ref, kr_ref, *, rq, rkv):
    for rows in _row_groups(x_ref, 2):
        acc = jnp.dot(x_ref[rows, :], w_ref[...], preferred_element_type=F32)
        cq_ref[rows, :] = _rms(acc[:, :rq], gq_ref[...]).astype(BF16)
        ckv_ref[rows, :] = _rms(acc[:, rq:rq + rkv], gkv_ref[...]).astype(BF16)
        kr_ref[rows, :] = _rope_lanes(acc[:, rq + rkv:], c_ref.at[rows, :], su_ref.at[rows, :],
                                      sd_ref.at[rows, :]).astype(BF16)


def _mla_down(u, w_down, g_q, g_kv, tabs, tm):
    t, d = u.shape
    rq, rkv = g_q.shape[1], g_kv.shape[1]
    n = w_down.shape[1]
    tab = pl.BlockSpec((tm, LANES), lambda i: (i, 0))
    kern = functools.partial(_mla_down_kernel, rq=rq, rkv=rkv)
    return pl.pallas_call(
        kern,
        grid=(t // tm,),
        in_specs=[
            pl.BlockSpec((tm, d), lambda i: (i, 0)),
            pl.BlockSpec((d, n), lambda i: (0, 0)),
            pl.BlockSpec((1, rq), lambda i: (0, 0)),
            pl.BlockSpec((1, rkv), lambda i: (0, 0)),
            tab, tab, tab,
        ],
        out_specs=[
            pl.BlockSpec((tm, rq), lambda i: (i, 0)),
            pl.BlockSpec((tm, rkv), lambda i: (i, 0)),
            pl.BlockSpec((tm, LANES), lambda i: (i, 0)),
        ],
        out_shape=[
            jax.ShapeDtypeStruct((t, rq), BF16),
            jax.ShapeDtypeStruct((t, rkv), BF16),
            jax.ShapeDtypeStruct((t, LANES), BF16),
        ],
        compiler_params=_params("parallel"),
        name="mla_down",
    )(u, w_down, g_q, g_kv, *tabs)


def _mla_up_kernel(cq_ref, ckv_ref, kr_ref, wq_ref, wkv_ref, c_ref, su_ref, sd_ref,
                   q_ref, k_ref, v_ref):
    cq = cq_ref[...]
    ckv = ckv_ref[...]
    kr = kr_ref[...]
    hp = MLA_HEAD_PAD
    for h in range(MLA_HEADS):
        qh = jnp.dot(cq, wq_ref[:, h * hp:(h + 1) * hp], preferred_element_type=F32) * MLA_Q_SCALE
        q_ref[:, h * hp:h * hp + LANES] = qh[:, :LANES].astype(BF16)
        q_ref[:, h * hp + LANES:(h + 1) * hp] = _rope_lanes(qh[:, LANES:], c_ref, su_ref, sd_ref).astype(BF16)
        kvh = jnp.dot(ckv, wkv_ref[:, h * hp:(h + 1) * hp], preferred_element_type=F32)
        k_ref[:, h * hp:h * hp + LANES] = kvh[:, :LANES].astype(BF16)
        k_ref[:, h * hp + LANES:(h + 1) * hp] = kr
        v_ref[:, h * hp:h * hp + MLA_D_V] = kvh[:, LANES:].astype(BF16)
        v_ref[:, h * hp + MLA_D_V:(h + 1) * hp] = jnp.ones((kvh.shape[0], hp - MLA_D_V), BF16)


def _mla_up(cq, ckv, kr, wq_pad, wkv, tabs, tm):
    t, rq = cq.shape
    rkv = ckv.shape[1]
    nq = wq_pad.shape[1]
    nkv = wkv.shape[1]
    tab = pl.BlockSpec((tm, LANES), lambda i: (i, 0))
    return pl.pallas_call(
        _mla_up_kernel,
        grid=(t // tm,),
        in_specs=[
            pl.BlockSpec((tm, rq), lambda i: (i, 0)),
            pl.BlockSpec((tm, rkv), lambda i: (i, 0)),
            pl.BlockSpec((tm, LANES), lambda i: (i, 0)),
            pl.BlockSpec((rq, nq), lambda i: (0, 0)),
            pl.BlockSpec((rkv, nkv), lambda i: (0, 0)),
            tab, tab, tab,
        ],
        out_specs=[
            pl.BlockSpec((tm, nq), lambda i: (i, 0)),
            pl.BlockSpec((tm, nq), lambda i: (i, 0)),
            pl.BlockSpec((tm, nq), lambda i: (i, 0)),
        ],
        out_shape=[
            jax.ShapeDtypeStruct((t, nq), BF16),
            jax.ShapeDtypeStruct((t, nq), BF16),
            jax.ShapeDtypeStruct((t, nq), BF16),
        ],
        compiler_params=_params("parallel"),
        name="mla_up",
    )(cq, ckv, kr, wq_pad, wkv, *tabs)


def _scores(q, k):
    return lax.dot_general(q, k, (((1,), (1,)), ((), ())), preferred_element_type=F32)


def _attn_lat_kernel(q_ref, kl_ref, kc_ref, vl_ref, vc_ref, o_ref, *, key_chunk):
    q = q_ref[...]
    n_lat = kl_ref.shape[0] // key_chunk
    chunks = [(kl_ref, vl_ref, c * key_chunk, key_chunk) for c in range(n_lat)]
    chunks.append((kc_ref, vc_ref, 0, kc_ref.shape[0]))

    def chunk_scores(c):
        k_ref, _, lo, size = chunks[c]
        return _scores(q, k_ref[lo:lo + size, :])

    s = chunk_scores(0)
    m = acc = None
    for c, (_, v_ref, lo, size) in enumerate(chunks):
        s_next = chunk_scores(c + 1) if c + 1 < len(chunks) else None
        m_c = jnp.max(s, axis=-1, keepdims=True)
        v = v_ref[lo:lo + size, :]
        if c == 0:
            m = m_c
            acc = jnp.dot(jnp.exp2(s - m).astype(BF16), v, preferred_element_type=F32)
        else:
            m_new = jnp.maximum(m, m_c)
            acc = (jnp.exp2(m - m_new) * acc
                   + jnp.dot(jnp.exp2(s - m_new).astype(BF16), v, preferred_element_type=F32))
            m = m_new
        s = s_next
    _store_normalised(o_ref, acc)


def _store_normalised(o_ref, acc):
    o_ref[...] = (acc[:, :MLA_D_V] / acc[:, MLA_D_V:MLA_D_V + 1]).astype(BF16)


def _attn_ctx_kernel(q_ref, kc_ref, vc_ref, o_ref):
    s_c = _scores(q_ref[...], kc_ref[...])
    p_c = jnp.exp2(s_c - jnp.max(s_c, axis=-1, keepdims=True))
    _store_normalised(o_ref, jnp.dot(p_c.astype(BF16), vc_ref[...], preferred_element_type=F32))


def _attention(q, k, v, batch, seq, ctx_len, with_ctx_queries):
    hp, dv = MLA_HEAD_PAD, MLA_D_V
    tq = min(seq, ATTN_Q_TILE)
    nq = seq // tq
    ctx0 = batch * seq // ctx_len
    o_lat = pl.pallas_call(
        functools.partial(_attn_lat_kernel, key_chunk=min(seq, ATTN_KEY_CHUNK)),
        grid=(batch, MLA_HEADS, nq),
        in_specs=[
            pl.BlockSpec((tq, hp), lambda b, h, i: (b * nq + i, h)),
            pl.BlockSpec((seq, hp), lambda b, h, i: (b, h)),
            pl.BlockSpec((ctx_len, hp), lambda b, h, i: (ctx0 + b, h)),
            pl.BlockSpec((seq, hp), lambda b, h, i: (b, h)),
            pl.BlockSpec((ctx_len, hp), lambda b, h, i: (ctx0 + b, h)),
        ],
        out_specs=pl.BlockSpec((tq, dv), lambda b, h, i: (b * nq + i, h)),
        out_shape=jax.ShapeDtypeStruct((batch * seq, MLA_HEADS * dv), BF16),
        compiler_params=_params("parallel", "parallel", "arbitrary"),
        name="attn_lat",
    )(q, k, k, v, v)
    if not with_ctx_queries:
        return o_lat
    o_ctx = pl.pallas_call(
        _attn_ctx_kernel,
        grid=(batch, MLA_HEADS),
        in_specs=[
            pl.BlockSpec((ctx_len, hp), lambda b, h: (ctx0 + b, h)),
            pl.BlockSpec((ctx_len, hp), lambda b, h: (ctx0 + b, h)),
            pl.BlockSpec((ctx_len, hp), lambda b, h: (ctx0 + b, h)),
        ],
        out_specs=pl.BlockSpec((ctx_len, dv), lambda b, h: (b, h)),
        out_shape=jax.ShapeDtypeStruct((batch * ctx_len, MLA_HEADS * dv), BF16),
        compiler_params=_params("parallel", "parallel"),
        name="attn_ctx",
    )(q, k, v)
    return o_lat, o_ctx


def _retention_tables(seq, n_ctx, dk):
    inv = RET_ROPE_BASE ** (-jnp.linspace(0.0, 1.0, dk // 2, dtype=F32))
    ang = jnp.arange(seq, dtype=F32)[:, None] * inv[None, :]
    return jnp.cos(ang), jnp.sin(ang), jnp.ones((n_ctx, dk // 2), F32), jnp.zeros((n_ctx, dk // 2), F32)


def _axial_tables(seq):
    n_f = MLA_D_ROPE // 4
    inv = AXIAL_ROPE_BASE ** (-jnp.arange(n_f, dtype=F32) * 2.0 / (MLA_D_ROPE // 2))
    tpos = jnp.arange(seq)
    row = (tpos // GRID_W).astype(F32)[:, None] * inv[None, :]
    col = (tpos % GRID_W).astype(F32)[:, None] * inv[None, :]
    z = jnp.zeros((seq, n_f), F32)
    tail = LANES - MLA_D_ROPE
    cos = jnp.concatenate([jnp.cos(row), jnp.cos(row), jnp.cos(col), jnp.cos(col), jnp.ones((seq, tail), F32)], 1)
    s_up = jnp.concatenate([-jnp.sin(row), z, -jnp.sin(col), z, jnp.zeros((seq, tail), F32)], 1)
    s_dn = jnp.concatenate([z, jnp.sin(row), z, jnp.sin(col), jnp.zeros((seq, tail), F32)], 1)
    return cos, s_up, s_dn


def _token_table(lat, batch, ctx_rows, fill):
    return jnp.concatenate([jnp.tile(lat, (batch, 1)), jnp.full((ctx_rows, lat.shape[1]), fill, F32)], 0)


def kernel(x, c, ctx, c_ctx, ada_w, ada_b, ln_g, ln_b, ret_w_qkv, ret_w_g, ret_decay_logit, ret_w_o,
           mla_w_dq, mla_g_q, mla_w_uq, mla_w_dkv, mla_g_kv, mla_w_ukv, mla_w_o, ffn_w_in, ffn_w_out):
    batch, seq, d = x.shape
    ctx_len = ctx.shape[1]
    depth = ada_w.shape[0]
    n_lat, n_ctx = batch * seq, batch * ctx_len
    n_tok = n_lat + n_ctx
    ret_heads = ret_decay_logit.shape[-1]
    assert ctx_len == RET_CHUNK and seq % RET_CHUNK == 0 and batch + 1 <= 8
    tm = _row_tile(n_lat, n_ctx)
    alpha = (2 * depth) ** 0.25

    cond8 = jnp.concatenate([c, c_ctx[None, :], jnp.zeros((8 - batch - 1, d), F32)], 0)
    mod = _modulation(cond8, ada_w, ada_b).reshape(depth * 8 * 6, 1, d)

    def mod_idx(layer, chunk, tile):
        def f(i):
            r = jnp.where(i * tile >= n_lat, batch, (i * tile) // seq)
            return (layer * 8 + r) * 6 + chunk
        return f

    ln_g3 = ln_g.reshape(depth * 2, 1, d)
    ln_b3 = ln_b.reshape(depth * 2, 1, d)

    ret_cos, ret_sin, one_c, zero_c = _retention_tables(seq, n_ctx, d // ret_heads)
    ret_cos = jnp.concatenate([jnp.tile(ret_cos, (batch, 1)), one_c], 0)
    ret_sin = jnp.concatenate([jnp.tile(ret_sin, (batch, 1)), zero_c], 0)
    ax_cos, ax_up, ax_dn = _axial_tables(seq)
    ax_tabs = (_token_table(ax_cos, batch, n_ctx, 1.0), _token_table(ax_up, batch, n_ctx, 0.0),
               _token_table(ax_dn, batch, n_ctx, 0.0))

    h, u = _modulate(x.reshape(n_lat, d), ctx.reshape(n_ctx, d), mod,
                     mod_idx(0, 1, TM_SMALL), mod_idx(0, 0, TM_SMALL), TM_SMALL)

    ret_w_o_b, mla_w_o_b, ffn_w_out_b = (w.astype(BF16) for w in (ret_w_o, mla_w_o, ffn_w_out))

    for i in range(depth):
        last = i == depth - 1
        n_rows = n_lat if last else n_tok
        j = i // 2
        if i % 2 == 0:
            qkv = _ret_qkv(u, ret_w_qkv, j, ret_cos, ret_sin, ret_heads, tm)
            decay_tile = jnp.broadcast_to(
                ret_decay_logit[j].astype(F32).reshape(2 * ret_heads, 1, 1), (2 * ret_heads, 8, LANES))
            o_f, o_b = _retention(qkv, decay_tile, batch, seq, ret_heads)
            y = _ret_gate(u, ret_w_g, j, o_f, o_b, tm)
            w_o = ret_w_o_b
        else:
            rq, rkv = mla_g_q.shape[1], mla_g_kv.shape[1]
            w_down = jnp.concatenate(
                [mla_w_dq[j], mla_w_dkv[j], jnp.zeros((d, LANES - MLA_D_ROPE), F32)], 1).astype(BF16)
            cq, ckv, kr = _mla_down(u, w_down, mla_g_q[j][None, :], mla_g_kv[j][None, :], ax_tabs, TM_SMALL)
            wq_pad = jnp.pad(
                mla_w_uq[j].reshape(rq, MLA_HEADS, MLA_D_NOPE + MLA_D_ROPE),
                ((0, 0), (0, 0), (0, MLA_HEAD_PAD - MLA_D_NOPE - MLA_D_ROPE)),
            ).reshape(rq, MLA_HEADS * MLA_HEAD_PAD).astype(BF16)
            q, k, v = _mla_up(cq, ckv, kr, wq_pad, mla_w_ukv[j].astype(BF16), ax_tabs, TM_SMALL)
            y = _attention(q, k, v, batch, seq, ctx_len, not last)
            w_o = mla_w_o_b

        tm_o, grp_o, tk_o = OUT_TILES_MIXER
        h, u = _out_ln(y, w_o, j, h, mod, ln_g3, ln_b3, gate_idx=mod_idx(i, 2, tm_o), ln_idx=2 * i,
                       sc_idx=mod_idx(i, 4, tm_o), sh_idx=mod_idx(i, 3, tm_o), alpha=alpha,
                       n_rows=n_rows, tm=tm_o, group=grp_o,
                       tk=w_o.shape[1] if w_o.shape[1] <= OUT_K_WHOLE else tk_o)
        hid = _ffn_in(u, ffn_w_in, i, n_rows, tm)
        nxt = min(i + 1, depth - 1)
        tm_o, grp_o, k_steps = OUT_TILES_FFN
        h, u = _out_ln(hid, ffn_w_out_b, i, h, mod, ln_g3, ln_b3, gate_idx=mod_idx(i, 5, tm_o),
                       ln_idx=2 * i + 1, sc_idx=mod_idx(nxt, 1, tm_o), sh_idx=mod_idx(nxt, 0, tm_o),
                       alpha=alpha, n_rows=n_rows, tm=tm_o, tk=ffn_w_out.shape[1] // k_steps, group=grp_o)
    return h.reshape(batch, seq, d)
```

```python
import functools

import jax
import jax.numpy as jnp
from jax import lax
from jax.experimental import pallas as pl
from jax.experimental.pallas import tpu as pltpu

F32 = jnp.float32
BF16 = jnp.bfloat16

GRID_W = 64
RET_ROPE_BASE = 10000.0
GN_EPS = 1e-6
MLA_HEADS = 16
MLA_D_NOPE = 128
MLA_D_ROPE = 64
MLA_D_V = 128
MLA_SCALE = (MLA_D_NOPE + MLA_D_ROPE) ** -0.5
MLA_Q_SCALE = MLA_SCALE * 1.4426950408889634
AXIAL_ROPE_BASE = 10000.0
RMS_EPS = 1e-6
LN_EPS = 1e-5

LANES = 128
MXU_DIM = 256
VMEM_LIMIT_BYTES = 56 * 1024 * 1024

RET_CHUNK = 256

TM_SMALL = 512
OUT_TILES_MIXER = (512, 2, 2048)
OUT_TILES_FFN = (512, 2, 4)
ATTN_Q_TILE = 1024
ATTN_KEY_CHUNK = 2048
MLA_HEAD_PAD = 2 * LANES


def _params(*sem):
    return pltpu.CompilerParams(dimension_semantics=sem, vmem_limit_bytes=VMEM_LIMIT_BYTES)


def _silu(x):
    return 0.5 * x * (1.0 + jnp.tanh(0.5 * x))


def _row_tile(n_lat, n_ctx):
    for t in (1024, 512, 256):
        if n_lat % t == 0 and n_ctx % t == 0:
            return t
    raise ValueError("token counts must be multiples of 256")


def _mod_kernel(cond_ref, w_ref, b_ref, o_ref):
    s = _silu(cond_ref[...]).astype(BF16)
    acc = jnp.dot(s, w_ref[0].astype(BF16), preferred_element_type=F32)
    o_ref[0] = acc + b_ref[0]


def _modulation(cond8, ada_w, ada_b):
    depth, d, n = ada_w.shape
    tn = 1024
    return pl.pallas_call(
        _mod_kernel,
        grid=(depth, n // tn),
        in_specs=[
            pl.BlockSpec((8, d), lambda l, j: (0, 0)),
            pl.BlockSpec((1, d, tn), lambda l, j: (l, 0, j)),
            pl.BlockSpec((1, 1, tn), lambda l, j: (l, 0, j)),
        ],
        out_specs=pl.BlockSpec((1, 8, tn), lambda l, j: (l, 0, j)),
        out_shape=jax.ShapeDtypeStruct((depth, 8, n), F32),
        compiler_params=_params("parallel", "parallel"),
        name="adaln_mod",
    )(cond8, ada_w, ada_b.reshape(depth, 1, n))


def _modulate_kernel(x_ref, ctx_ref, sc_ref, sh_ref, h_ref, u_ref, *, n_lat_tiles):
    def emit(src_ref):
        hv = src_ref[...]
        h_ref[...] = hv
        u_ref[...] = (hv * (1.0 + sc_ref[0]) + sh_ref[0]).astype(BF16)

    is_lat = pl.program_id(0) < n_lat_tiles
    pl.when(is_lat)(lambda: emit(x_ref))
    pl.when(jnp.logical_not(is_lat))(lambda: emit(ctx_ref))


def _modulate(x2d, ctx2d, mod, sc_idx, sh_idx, tm):
    n_lat, d = x2d.shape
    n_ctx = ctx2d.shape[0]
    t = n_lat + n_ctx
    n_lat_tiles = n_lat // tm
    vec = lambda f: pl.BlockSpec((1, 1, d), lambda i: (f(i), 0, 0))
    row = pl.BlockSpec((tm, d), lambda i: (i, 0))
    return pl.pallas_call(
        functools.partial(_modulate_kernel, n_lat_tiles=n_lat_tiles),
        grid=(t // tm,),
        in_specs=[
            pl.BlockSpec((tm, d), lambda i: (jnp.minimum(i, n_lat_tiles - 1), 0)),
            pl.BlockSpec((tm, d), lambda i: (jnp.maximum(i - n_lat_tiles, 0), 0)),
            vec(sc_idx), vec(sh_idx),
        ],
        out_specs=[row, row],
        out_shape=[jax.ShapeDtypeStruct((t, d), F32), jax.ShapeDtypeStruct((t, d), BF16)],
        compiler_params=_params("arbitrary"),
        name="modulate_in",
    )(x2d, ctx2d, mod, mod)


def _cast_weights_once(pairs):
    @pl.when(pl.program_id(1) == 0)
    def _():
        for src_ref, dst_ref in pairs:
            dst_ref[...] = src_ref[...].astype(BF16)


def _row_groups(ref, n_split):
    rows_per = ref.shape[0] // n_split
    return [pl.ds(r * rows_per, rows_per) for r in range(n_split)]


def _ret_qkv_kernel(x_ref, w_ref, cos_ref, sin_ref, o_ref, wb_ref, *, n_q, n_k, k_scale, n_split):
    j = pl.program_id(0)
    tn = o_ref.shape[1]
    _cast_weights_once([(w_ref, wb_ref)])

    def product(rows):
        return jnp.dot(x_ref[rows, :], wb_ref[...], preferred_element_type=F32)

    @pl.when(j < n_q + n_k)
    def _():
        scale = jnp.where(j < n_q, 1.0, k_scale).astype(F32)
        for rows in _row_groups(o_ref, n_split):
            acc = product(rows)
            c = cos_ref[rows, :]
            s = sin_ref[rows, :]
            for g in range(tn // MXU_DIM):
                lo = g * MXU_DIM
                x1 = acc[:, lo:lo + LANES] * scale
                x2 = acc[:, lo + LANES:lo + MXU_DIM] * scale
                o_ref[rows, lo:lo + LANES] = (x1 * c - x2 * s).astype(BF16)
                o_ref[rows, lo + LANES:lo + MXU_DIM] = (x1 * s + x2 * c).astype(BF16)

    @pl.when(j >= n_q + n_k)
    def _():
        for rows in _row_groups(o_ref, n_split):
            o_ref[rows, :] = product(rows).astype(BF16)


def _ret_qkv(u, w, layer, cos, sin, n_heads, tm):
    t, d = u.shape
    n = w.shape[2]
    dk = d // n_heads
    tn = 1024
    n_q = n_heads * dk // tn
    kern = functools.partial(_ret_qkv_kernel, n_q=n_q, n_k=n_q, k_scale=dk ** -0.5, n_split=2)
    return pl.pallas_call(
        kern,
        grid=(n // tn, t // tm),
        in_specs=[
            pl.BlockSpec((tm, d), lambda j, i: (i, 0)),
            pl.BlockSpec((None, d, tn), lambda j, i: (layer, 0, j)),
            pl.BlockSpec((tm, LANES), lambda j, i: (i, 0)),
            pl.BlockSpec((tm, LANES), lambda j, i: (i, 0)),
        ],
        out_specs=pl.BlockSpec((tm, tn), lambda j, i: (i, j)),
        out_shape=jax.ShapeDtypeStruct((t, n), BF16),
        scratch_shapes=[pltpu.VMEM((d, tn), BF16)],
        compiler_params=_params("parallel", "arbitrary"),
        name="ret_qkv",
    )(u, w, cos, sin)


def _retention_kernel(dec_ref, qf_ref, qb_ref, of_ref, ob_ref, s_ref, intra_ref, qdec_ref, kdec_ref,
                      *, n_heads, dk, dv):
    c = RET_CHUNK
    step = pl.program_id(1)

    def log_gamma(idx):
        x = dec_ref[idx][0:1, 0:1]
        return jnp.minimum(x, 0.0) - jnp.log1p(jnp.exp(-jnp.abs(x)))

    @pl.when(step == 0)
    def _():
        s_ref[...] = jnp.zeros_like(s_ref)
        row = lax.broadcasted_iota(jnp.int32, (c, c), 0)
        col = lax.broadcasted_iota(jnp.int32, (c, c), 1)
        pos = lax.broadcasted_iota(jnp.int32, (c, dk), 0).astype(F32)
        for d in range(2):
            diff = (row - col) if d == 0 else (col - row)
            dist = jnp.maximum(diff, 0).astype(F32)
            ahead = pos + 1.0 if d == 0 else float(c) - pos
            behind = float(c - 1) - pos if d == 0 else pos
            for h in range(n_heads):
                idx = d * n_heads + h
                lg = log_gamma(idx)
                intra_ref[idx] = jnp.where(diff >= 0, jnp.exp(lg * dist), 0.0)
                qdec_ref[idx] = jnp.exp(lg * ahead).astype(BF16)
                kdec_ref[idx] = jnp.exp(lg * behind).astype(BF16)

    for d, (x_ref, o_ref) in enumerate(((qf_ref, of_ref), (qb_ref, ob_ref))):
        for h in range(n_heads):
            idx = d * n_heads + h
            q = x_ref[:, h * dk:(h + 1) * dk]
            k = x_ref[:, n_heads * dk + h * dk:n_heads * dk + (h + 1) * dk]
            v = x_ref[:, 2 * n_heads * dk + h * dv:2 * n_heads * dk + (h + 1) * dv]
            scores = lax.dot_general(q, k, (((1,), (1,)), ((), ())), preferred_element_type=F32)
            p = (scores * intra_ref[idx]).astype(BF16)
            state = s_ref[idx]
            o = (jnp.dot(p, v, preferred_element_type=F32)
                 + jnp.dot(q * qdec_ref[idx], state.astype(BF16), preferred_element_type=F32))
            s_ref[idx] = state * jnp.exp(log_gamma(idx) * float(c)) + lax.dot_general(
                k * kdec_ref[idx], v, (((0,), (0,)), ((), ())), preferred_element_type=F32)
            mu = jnp.mean(o, axis=-1, keepdims=True)
            ctr = o - mu
            var = jnp.mean(ctr * ctr, axis=-1, keepdims=True)
            o_ref[:, h * dv:(h + 1) * dv] = (ctr * lax.rsqrt(var + GN_EPS)).astype(BF16)


def _retention(qkv, decay_tile, batch, seq, n_heads):
    t, n = qkv.shape
    d = n // 4
    dk, dv = d // n_heads, 2 * d // n_heads
    c = RET_CHUNK
    n_lat = seq // c
    ctx_blk = batch * n_lat

    def fwd_blk(b, s):
        return jnp.where(s == 0, ctx_blk + b, b * n_lat + s - 1)

    def bwd_blk(b, s):
        return jnp.where(s == 0, ctx_blk + b, b * n_lat + n_lat - s)

    kern = functools.partial(_retention_kernel, n_heads=n_heads, dk=dk, dv=dv)
    out = jax.ShapeDtypeStruct((t, n_heads * dv), BF16)
    return pl.pallas_call(
        kern,
        grid=(batch, n_lat + 1),
        in_specs=[
            pl.BlockSpec((2 * n_heads, 8, LANES), lambda b, s: (0, 0, 0)),
            pl.BlockSpec((c, n), lambda b, s: (fwd_blk(b, s), 0)),
            pl.BlockSpec((c, n), lambda b, s: (bwd_blk(b, s), 0)),
        ],
        out_specs=[
            pl.BlockSpec((c, n_heads * dv), lambda b, s: (fwd_blk(b, s), 0)),
            pl.BlockSpec((c, n_heads * dv), lambda b, s: (bwd_blk(b, s), 0)),
        ],
        out_shape=[out, out],
        scratch_shapes=[
            pltpu.VMEM((2 * n_heads, dk, dv), F32),
            pltpu.VMEM((2 * n_heads, c, c), F32),
            pltpu.VMEM((2 * n_heads, c, dk), BF16),
            pltpu.VMEM((2 * n_heads, c, dk), BF16),
        ],
        compiler_params=_params("parallel", "arbitrary"),
        name="retention_scan",
    )(decay_tile, qkv, qkv)


def _ret_gate_kernel(x_ref, wf_ref, wb_ref, of_ref, ob_ref, y_ref, wf_bf, wb_bf, *, n_split):
    _cast_weights_once([(wf_ref, wf_bf), (wb_ref, wb_bf)])
    for rows in _row_groups(y_ref, n_split):
        x = x_ref[rows, :]
        gf = jnp.dot(x, wf_bf[...], preferred_element_type=F32)
        gb = jnp.dot(x, wb_bf[...], preferred_element_type=F32)
        y = _silu(gf) * of_ref[rows, :].astype(F32) + _silu(gb) * ob_ref[rows, :].astype(F32)
        y_ref[rows, :] = y.astype(BF16)


def _ret_gate(u, w_g, layer, o_f, o_b, tm):
    t, d = u.shape
    n = w_g.shape[2] // 2
    tn = 512
    nb = n // tn
    return pl.pallas_call(
        functools.partial(_ret_gate_kernel, n_split=2),
        grid=(nb, t // tm),
        in_specs=[
            pl.BlockSpec((tm, d), lambda j, i: (i, 0)),
            pl.BlockSpec((None, d, tn), lambda j, i: (layer, 0, j)),
            pl.BlockSpec((None, d, tn), lambda j, i: (layer, 0, nb + j)),
            pl.BlockSpec((tm, tn), lambda j, i: (i, j)),
            pl.BlockSpec((tm, tn), lambda j, i: (i, j)),
        ],
        out_specs=pl.BlockSpec((tm, tn), lambda j, i: (i, j)),
        out_shape=jax.ShapeDtypeStruct((t, n), BF16),
        scratch_shapes=[pltpu.VMEM((d, tn), BF16), pltpu.VMEM((d, tn), BF16)],
        compiler_params=_params("parallel", "arbitrary"),
        name="ret_gate",
    )(u, w_g, w_g, o_f, o_b)


def _out_ln_kernel(*refs, alpha, n_k, n_split, group, n_x, n_lat_tiles):
    x_refs, refs = refs[:n_x], refs[n_x:]
    w_ref, h_ref, gate_ref, lng_ref, lnb_ref, sc_ref, sh_ref, hout_ref, u_ref, *acc = refs
    k = pl.program_id(1)
    is_lat = group * pl.program_id(0) + pl.program_id(2) < n_lat_tiles

    def x_rows(rows):
        if n_x == 1:
            return x_refs[0][rows, :]
        return jnp.where(is_lat, x_refs[0][rows, :], x_refs[1][rows, :])

    def partial_product(rows):
        return jnp.dot(x_rows(rows), w_ref[...], preferred_element_type=F32)

    def finish(rows, y):
        z = alpha * h_ref[rows, :] + gate_ref[0] * y
        mu = jnp.mean(z, axis=-1, keepdims=True)
        ctr = z - mu
        var = jnp.mean(ctr * ctr, axis=-1, keepdims=True)
        hn = ctr * lax.rsqrt(var + LN_EPS) * lng_ref[0] + lnb_ref[0]
        hout_ref[rows, :] = hn
        u_ref[rows, :] = (hn * (1.0 + sc_ref[0]) + sh_ref[0]).astype(BF16)

    def last_step(prev):
        for rows in _row_groups(hout_ref, n_split):
            y = partial_product(rows)
            finish(rows, y if prev is None else prev[rows, :] + y)

    if n_k == 1:
        last_step(None)
        return
    acc_ref = acc[0].at[pl.program_id(2)]

    @pl.when(k == 0)
    def _():
        acc_ref[...] = partial_product(slice(None))

    @pl.when(jnp.logical_and(k > 0, k < n_k - 1))
    def _():
        acc_ref[...] += partial_product(slice(None))

    @pl.when(k == n_k - 1)
    def _():
        last_step(acc_ref)


def _out_ln(x, w, layer, h, mod, ln_g, ln_b, *, gate_idx, ln_idx, sc_idx, sh_idx, alpha, n_rows, tm, tk,
            group, n_split=2):
    x_parts = x if isinstance(x, tuple) else (x,)
    kdim = x_parts[0].shape[1]
    d = w.shape[2]
    n_k = kdim // tk
    if n_rows % (group * tm):
        group = 1
    assert n_rows % (group * tm) == 0 and kdim % tk == 0
    n_lat_tiles = x_parts[0].shape[0] // tm if len(x_parts) == 2 else 0

    def row_blk(i, k, m):
        return group * i + jnp.where(k == n_k - 1, m, 0)

    if len(x_parts) == 1:
        x_specs = [pl.BlockSpec((tm, tk), lambda i, k, m: (group * i + m, k))]
    else:
        assert x_parts[0].shape[0] == n_lat_tiles * tm
        x_specs = [
            pl.BlockSpec((tm, tk), lambda i, k, m: (jnp.minimum(group * i + m, n_lat_tiles - 1), k)),
            pl.BlockSpec((tm, tk), lambda i, k, m: (jnp.maximum(group * i + m - n_lat_tiles, 0), k)),
        ]

    vec = lambda arr_idx: pl.BlockSpec((1, 1, d), lambda i, k, m: (arr_idx(row_blk(i, k, m)), 0, 0))
    row = pl.BlockSpec((tm, d), lambda i, k, m: (row_blk(i, k, m), 0))
    kern = functools.partial(_out_ln_kernel, alpha=alpha, n_k=n_k, n_split=n_split, group=group,
                             n_x=len(x_parts), n_lat_tiles=n_lat_tiles)
    return pl.pallas_call(
        kern,
        grid=(n_rows // (group * tm), n_k, group),
        in_specs=[
            *x_specs,
            pl.BlockSpec((None, tk, d), lambda i, k, m: (layer, k, 0)),
            row,
            vec(gate_idx),
            vec(lambda r: ln_idx),
            vec(lambda r: ln_idx),
            vec(sc_idx),
            vec(sh_idx),
        ],
        out_specs=[row, row],
        out_shape=[
            jax.ShapeDtypeStruct((n_rows, d), F32),
            jax.ShapeDtypeStruct((n_rows, d), BF16),
        ],
        scratch_shapes=[pltpu.VMEM((group, tm, d), F32)] if n_k > 1 else [],
        compiler_params=_params("parallel", "arbitrary", "arbitrary"),
        name="out_ln",
    )(*x_parts, w, h, mod, ln_g, ln_b, mod, mod)


def _ffn_in_kernel(x_ref, wa_ref, wb_ref, o_ref, wa_bf, wb_bf, *, n_split):
    _cast_weights_once([(wa_ref, wa_bf), (wb_ref, wb_bf)])
    for rows in _row_groups(o_ref, n_split):
        x = x_ref[rows, :]
        a = jnp.dot(x, wa_bf[...], preferred_element_type=F32)
        b = jnp.dot(x, wb_bf[...], preferred_element_type=F32)
        o_ref[rows, :] = (_silu(a) * b).astype(BF16)


def _ffn_in(u, w_in, layer, n_rows, tm):
    d = u.shape[1]
    f = w_in.shape[2] // 2
    tn = 512
    nb = f // tn
    return pl.pallas_call(
        functools.partial(_ffn_in_kernel, n_split=2),
        grid=(nb, n_rows // tm),
        in_specs=[
            pl.BlockSpec((tm, d), lambda j, i: (i, 0)),
            pl.BlockSpec((None, d, tn), lambda j, i: (layer, 0, j)),
            pl.BlockSpec((None, d, tn), lambda j, i: (layer, 0, nb + j)),
        ],
        out_specs=pl.BlockSpec((tm, tn), lambda j, i: (i, j)),
        out_shape=jax.ShapeDtypeStruct((n_rows, f), BF16),
        scratch_shapes=[pltpu.VMEM((d, tn), BF16), pltpu.VMEM((d, tn), BF16)],
        compiler_params=_params("parallel", "arbitrary"),
        name="ffn_in",
    )(u, w_in, w_in)


def _rope_lanes(x, c_ref, s_up_ref, s_dn_ref):
    half = MLA_D_ROPE // 4
    up = pltpu.roll(x, LANES - half, 1)
    dn = pltpu.roll(x, half, 1)
    return x * c_ref[...] + up * s_up_ref[...] + dn * s_dn_ref[...]


def _rms(x, g):
    return x * lax.rsqrt(jnp.mean(x * x, axis=-1, keepdims=True) + RMS_EPS) * g


def _mla_down_kernel(x_ref, w_ref, gq_ref, gkv_ref, c_ref, su_ref, sd_ref,
                     cq_ref, ckv_ref, kr_ref, *, rq, rkv):
    for rows in _row_groups(x_ref, 2):
        acc = jnp.dot(x_ref[rows, :], w_ref[...], preferred_element_type=F32)
        cq_ref[rows, :] = _rms(acc[:, :rq], gq_ref[...]).astype(BF16)
        ckv_ref[rows, :] = _rms(acc[:, rq:rq + rkv], gkv_ref[...]).astype(BF16)
        kr_ref[rows, :] = _rope_lanes(acc[:, rq + rkv:], c_ref.at[rows, :], su_ref.at[rows, :],
                                      sd_ref.at[rows, :]).astype(BF16)


def _mla_down(u, w_down, g_q, g_kv, tabs, tm):
    t, d = u.shape
    rq, rkv = g_q.shape[1], g_kv.shape[1]
    n = w_down.shape[1]
    tab = pl.BlockSpec((tm, LANES), lambda i: (i, 0))
    kern = functools.partial(_mla_down_kernel, rq=rq, rkv=rkv)
    return pl.pallas_call(
        kern,
        grid=(t // tm,),
        in_specs=[
            pl.BlockSpec((tm, d), lambda i: (i, 0)),
            pl.BlockSpec((d, n), lambda i: (0, 0)),
            pl.BlockSpec((1, rq), lambda i: (0, 0)),
            pl.BlockSpec((1, rkv), lambda i: (0, 0)),
            tab, tab, tab,
        ],
        out_specs=[
            pl.BlockSpec((tm, rq), lambda i: (i, 0)),
            pl.BlockSpec((tm, rkv), lambda i: (i, 0)),
            pl.BlockSpec((tm, LANES), lambda i: (i, 0)),
        ],
        out_shape=[
            jax.ShapeDtypeStruct((t, rq), BF16),
            jax.ShapeDtypeStruct((t, rkv), BF16),
            jax.ShapeDtypeStruct((t, LANES), BF16),
        ],
        compiler_params=_params("parallel"),
        name="mla_down",
    )(u, w_down, g_q, g_kv, *tabs)


def _mla_up_kernel(cq_ref, ckv_ref, kr_ref, wq_ref, wkv_ref, c_ref, su_ref, sd_ref,
                   q_ref, k_ref, v_ref):
    cq = cq_ref[...]
    ckv = ckv_ref[...]
    kr = kr_ref[...]
    hp = MLA_HEAD_PAD
    for h in range(MLA_HEADS):
        qh = jnp.dot(cq, wq_ref[:, h * hp:(h + 1) * hp], preferred_element_type=F32) * MLA_Q_SCALE
        q_ref[:, h * hp:h * hp + LANES] = qh[:, :LANES].astype(BF16)
        q_ref[:, h * hp + LANES:(h + 1) * hp] = _rope_lanes(qh[:, LANES:], c_ref, su_ref, sd_ref).astype(BF16)
        kvh = jnp.dot(ckv, wkv_ref[:, h * hp:(h + 1) * hp], preferred_element_type=F32)
        k_ref[:, h * hp:h * hp + LANES] = kvh[:, :LANES].astype(BF16)
        k_ref[:, h * hp + LANES:(h + 1) * hp] = kr
        v_ref[:, h * hp:h * hp + MLA_D_V] = kvh[:, LANES:].astype(BF16)
        v_ref[:, h * hp + MLA_D_V:(h + 1) * hp] = jnp.ones((kvh.shape[0], hp - MLA_D_V), BF16)


def _mla_up(cq, ckv, kr, wq_pad, wkv, tabs, tm):
    t, rq = cq.shape
    rkv = ckv.shape[1]
    nq = wq_pad.shape[1]
    nkv = wkv.shape[1]
    tab = pl.BlockSpec((tm, LANES), lambda i: (i, 0))
    return pl.pallas_call(
        _mla_up_kernel,
        grid=(t // tm,),
        in_specs=[
            pl.BlockSpec((tm, rq), lambda i: (i, 0)),
            pl.BlockSpec((tm, rkv), lambda i: (i, 0)),
            pl.BlockSpec((tm, LANES), lambda i: (i, 0)),
            pl.BlockSpec((rq, nq), lambda i: (0, 0)),
            pl.BlockSpec((rkv, nkv), lambda i: (0, 0)),
            tab, tab, tab,
        ],
        out_specs=[
            pl.BlockSpec((tm, nq), lambda i: (i, 0)),
            pl.BlockSpec((tm, nq), lambda i: (i, 0)),
            pl.BlockSpec((tm, nq), lambda i: (i, 0)),
        ],
        out_shape=[
            jax.ShapeDtypeStruct((t, nq), BF16),
            jax.ShapeDtypeStruct((t, nq), BF16),
            jax.ShapeDtypeStruct((t, nq), BF16),
        ],
        compiler_params=_params("parallel"),
        name="mla_up",
    )(cq, ckv, kr, wq_pad, wkv, *tabs)


def _scores(q, k):
    return lax.dot_general(q, k, (((1,), (1,)), ((), ())), preferred_element_type=F32)


def _attn_lat_kernel(q_ref, kl_ref, kc_ref, vl_ref, vc_ref, o_ref, *, key_chunk):
    q = q_ref[...]
    n_lat = kl_ref.shape[0] // key_chunk
    chunks = [(kl_ref, vl_ref, c * key_chunk, key_chunk) for c in range(n_lat)]
    chunks.append((kc_ref, vc_ref, 0, kc_ref.shape[0]))

    def chunk_scores(c):
        k_ref, _, lo, size = chunks[c]
        return _scores(q, k_ref[lo:lo + size, :])

    s = chunk_scores(0)
    m = acc = None
    for c, (_, v_ref, lo, size) in enumerate(chunks):
        s_next = chunk_scores(c + 1) if c + 1 < len(chunks) else None
        m_c = jnp.max(s, axis=-1, keepdims=True)
        v = v_ref[lo:lo + size, :]
        if c == 0:
            m = m_c
            acc = jnp.dot(jnp.exp2(s - m).astype(BF16), v, preferred_element_type=F32)
        else:
            m_new = jnp.maximum(m, m_c)
            acc = (jnp.exp2(m - m_new) * acc
                   + jnp.dot(jnp.exp2(s - m_new).astype(BF16), v, preferred_element_type=F32))
            m = m_new
        s = s_next
    _store_normalised(o_ref, acc)


def _store_normalised(o_ref, acc):
    o_ref[...] = (acc[:, :MLA_D_V] / acc[:, MLA_D_V:MLA_D_V + 1]).astype(BF16)


def _attn_ctx_kernel(q_ref, kc_ref, vc_ref, o_ref):
    hp, dv = MLA_HEAD_PAD, MLA_D_V
    for h in range(MLA_HEADS):
        cols = slice(h * hp, (h + 1) * hp)
        s_c = _scores(q_ref[:, cols], kc_ref[:, cols])
        p_c = jnp.exp2(s_c - jnp.max(s_c, axis=-1, keepdims=True))
        _store_normalised(o_ref.at[:, h * dv:(h + 1) * dv],
                          jnp.dot(p_c.astype(BF16), vc_ref[:, cols], preferred_element_type=F32))


def _attention(q, k, v, batch, seq, ctx_len, with_ctx_queries):
    hp, dv = MLA_HEAD_PAD, MLA_D_V
    tq = min(seq, ATTN_Q_TILE)
    nq = seq // tq
    ctx0 = batch * seq // ctx_len
    o_lat = pl.pallas_call(
        functools.partial(_attn_lat_kernel, key_chunk=min(seq, ATTN_KEY_CHUNK)),
        grid=(batch, MLA_HEADS, nq),
        in_specs=[
            pl.BlockSpec((tq, hp), lambda b, h, i: (b * nq + i, h)),
            pl.BlockSpec((seq, hp), lambda b, h, i: (b, h)),
            pl.BlockSpec((ctx_len, hp), lambda b, h, i: (ctx0 + b, h)),
            pl.BlockSpec((seq, hp), lambda b, h, i: (b, h)),
            pl.BlockSpec((ctx_len, hp), lambda b, h, i: (ctx0 + b, h)),
        ],
        out_specs=pl.BlockSpec((tq, dv), lambda b, h, i: (b * nq + i, h)),
        out_shape=jax.ShapeDtypeStruct((batch * seq, MLA_HEADS * dv), BF16),
        compiler_params=_params("parallel", "parallel", "arbitrary"),
        name="attn_lat",
    )(q, k, k, v, v)
    if not with_ctx_queries:
        return o_lat
    o_ctx = pl.pallas_call(
        _attn_ctx_kernel,
        grid=(batch,),
        in_specs=[
            pl.BlockSpec((ctx_len, MLA_HEADS * hp), lambda b: (ctx0 + b, 0)),
            pl.BlockSpec((ctx_len, MLA_HEADS * hp), lambda b: (ctx0 + b, 0)),
            pl.BlockSpec((ctx_len, MLA_HEADS * hp), lambda b: (ctx0 + b, 0)),
        ],
        out_specs=pl.BlockSpec((ctx_len, MLA_HEADS * dv), lambda b: (b, 0)),
        out_shape=jax.ShapeDtypeStruct((batch * ctx_len, MLA_HEADS * dv), BF16),
        compiler_params=_params("parallel"),
        name="attn_ctx",
    )(q, k, v)
    return o_lat, o_ctx


def _retention_tables(seq, n_ctx, dk):
    inv = RET_ROPE_BASE ** (-jnp.linspace(0.0, 1.0, dk // 2, dtype=F32))
    ang = jnp.arange(seq, dtype=F32)[:, None] * inv[None, :]
    return jnp.cos(ang), jnp.sin(ang), jnp.ones((n_ctx, dk // 2), F32), jnp.zeros((n_ctx, dk // 2), F32)


def _axial_tables(seq):
    n_f = MLA_D_ROPE // 4
    inv = AXIAL_ROPE_BASE ** (-jnp.arange(n_f, dtype=F32) * 2.0 / (MLA_D_ROPE // 2))
    tpos = jnp.arange(seq)
    row = (tpos // GRID_W).astype(F32)[:, None] * inv[None, :]
    col = (tpos % GRID_W).astype(F32)[:, None] * inv[None, :]
    z = jnp.zeros((seq, n_f), F32)
    tail = LANES - MLA_D_ROPE
    cos = jnp.concatenate([jnp.cos(row), jnp.cos(row), jnp.cos(col), jnp.cos(col), jnp.ones((seq, tail), F32)], 1)
    s_up = jnp.concatenate([-jnp.sin(row), z, -jnp.sin(col), z, jnp.zeros((seq, tail), F32)], 1)
    s_dn = jnp.concatenate([z, jnp.sin(row), z, jnp.sin(col), jnp.zeros((seq, tail), F32)], 1)
    return cos, s_up, s_dn


def _token_table(lat, batch, ctx_rows, fill):
    return jnp.concatenate([jnp.tile(lat, (batch, 1)), jnp.full((ctx_rows, lat.shape[1]), fill, F32)], 0)


def kernel(x, c, ctx, c_ctx, ada_w, ada_b, ln_g, ln_b, ret_w_qkv, ret_w_g, ret_decay_logit, ret_w_o,
           mla_w_dq, mla_g_q, mla_w_uq, mla_w_dkv, mla_g_kv, mla_w_ukv, mla_w_o, ffn_w_in, ffn_w_out):
    batch, seq, d = x.shape
    ctx_len = ctx.shape[1]
    depth = ada_w.shape[0]
    n_lat, n_ctx = batch * seq, batch * ctx_len
    n_tok = n_lat + n_ctx
    ret_heads = ret_decay_logit.shape[-1]
    assert ctx_len == RET_CHUNK and seq % RET_CHUNK == 0 and batch + 1 <= 8
    tm = _row_tile(n_lat, n_ctx)
    alpha = (2 * depth) ** 0.25

    cond8 = jnp.concatenate([c, c_ctx[None, :], jnp.zeros((8 - batch - 1, d), F32)], 0)
    mod = _modulation(cond8, ada_w, ada_b).reshape(depth * 8 * 6, 1, d)

    def mod_idx(layer, chunk, tile):
        def f(i):
            r = jnp.where(i * tile >= n_lat, batch, (i * tile) // seq)
            return (layer * 8 + r) * 6 + chunk
        return f

    ln_g3 = ln_g.reshape(depth * 2, 1, d)
    ln_b3 = ln_b.reshape(depth * 2, 1, d)

    ret_cos, ret_sin, one_c, zero_c = _retention_tables(seq, n_ctx, d // ret_heads)
    ret_cos = jnp.concatenate([jnp.tile(ret_cos, (batch, 1)), one_c], 0)
    ret_sin = jnp.concatenate([jnp.tile(ret_sin, (batch, 1)), zero_c], 0)
    ax_cos, ax_up, ax_dn = _axial_tables(seq)
    ax_tabs = (_token_table(ax_cos, batch, n_ctx, 1.0), _token_table(ax_up, batch, n_ctx, 0.0),
               _token_table(ax_dn, batch, n_ctx, 0.0))

    h, u = _modulate(x.reshape(n_lat, d), ctx.reshape(n_ctx, d), mod,
                     mod_idx(0, 1, TM_SMALL), mod_idx(0, 0, TM_SMALL), TM_SMALL)

    ret_w_o_b, mla_w_o_b, ffn_w_out_b = (w.astype(BF16) for w in (ret_w_o, mla_w_o, ffn_w_out))

    for i in range(depth):
        last = i == depth - 1
        n_rows = n_lat if last else n_tok
        j = i // 2
        if i % 2 == 0:
            qkv = _ret_qkv(u, ret_w_qkv, j, ret_cos, ret_sin, ret_heads, tm)
            decay_tile = jnp.broadcast_to(
                ret_decay_logit[j].astype(F32).reshape(2 * ret_heads, 1, 1), (2 * ret_heads, 8, LANES))
            o_f, o_b = _retention(qkv, decay_tile, batch, seq, ret_heads)
            y = _ret_gate(u, ret_w_g, j, o_f, o_b, tm)
            w_o = ret_w_o_b
        else:
            rq, rkv = mla_g_q.shape[1], mla_g_kv.shape[1]
            w_down = jnp.concatenate(
                [mla_w_dq[j], mla_w_dkv[j], jnp.zeros((d, LANES - MLA_D_ROPE), F32)], 1).astype(BF16)
            cq, ckv, kr = _mla_down(u, w_down, mla_g_q[j][None, :], mla_g_kv[j][None, :], ax_tabs, TM_SMALL)
            wq_pad = jnp.pad(
                mla_w_uq[j].reshape(rq, MLA_HEADS, MLA_D_NOPE + MLA_D_ROPE),
                ((0, 0), (0, 0), (0, MLA_HEAD_PAD - MLA_D_NOPE - MLA_D_ROPE)),
            ).reshape(rq, MLA_HEADS * MLA_HEAD_PAD).astype(BF16)
            q, k, v = _mla_up(cq, ckv, kr, wq_pad, mla_w_ukv[j].astype(BF16), ax_tabs, TM_SMALL)
            y = _attention(q, k, v, batch, seq, ctx_len, not last)
            w_o = mla_w_o_b

        tm_o, grp_o, tk_o = OUT_TILES_MIXER
        h, u = _out_ln(y, w_o, j, h, mod, ln_g3, ln_b3, gate_idx=mod_idx(i, 2, tm_o), ln_idx=2 * i,
                       sc_idx=mod_idx(i, 4, tm_o), sh_idx=mod_idx(i, 3, tm_o), alpha=alpha,
                       n_rows=n_rows, tm=tm_o, tk=min(tk_o, w_o.shape[1]), group=grp_o)
        hid = _ffn_in(u, ffn_w_in, i, n_rows, tm)
        nxt = min(i + 1, depth - 1)
        tm_o, grp_o, k_steps = OUT_TILES_FFN
        h, u = _out_ln(hid, ffn_w_out_b, i, h, mod, ln_g3, ln_b3, gate_idx=mod_idx(i, 5, tm_o),
                       ln_idx=2 * i + 1, sc_idx=mod_idx(nxt, 1, tm_o), sh_idx=mod_idx(nxt, 0, tm_o),
                       alpha=alpha, n_rows=n_rows, tm=tm_o, tk=ffn_w_out.shape[1] // k_steps, group=grp_o)
    return h.reshape(batch, seq, d)
```

```python
import functools

import jax
import jax.numpy as jnp
from jax import lax
from jax.experimental import pallas as pl
from jax.experimental.pallas import tpu as pltpu

F32 = jnp.float32
BF16 = jnp.bfloat16

GRID_W = 64
RET_ROPE_BASE = 10000.0
GN_EPS = 1e-6
MLA_HEADS = 16
MLA_D_NOPE = 128
MLA_D_ROPE = 64
MLA_D_V = 128
MLA_SCALE = (MLA_D_NOPE + MLA_D_ROPE) ** -0.5
MLA_Q_SCALE = MLA_SCALE * 1.4426950408889634
AXIAL_ROPE_BASE = 10000.0
RMS_EPS = 1e-6
LN_EPS = 1e-5

LANES = 128
MXU_DIM = 256
VMEM_LIMIT_BYTES = 56 * 1024 * 1024

RET_CHUNK = 256

TM_SMALL = 512
OUT_TILES_MIXER = (512, 2, 2048)
OUT_TILES_FFN = (512, 2, 4)
ATTN_Q_TILE = 1024
ATTN_KEY_CHUNK = 2048
MLA_HEAD_PAD = 2 * LANES


def _params(*sem):
    return pltpu.CompilerParams(dimension_semantics=sem, vmem_limit_bytes=VMEM_LIMIT_BYTES)


def _silu(x):
    return 0.5 * x * (1.0 + jnp.tanh(0.5 * x))


def _row_tile(n_lat, n_ctx):
    for t in (1024, 512, 256):
        if n_lat % t == 0 and n_ctx % t == 0:
            return t
    raise ValueError("token counts must be multiples of 256")


def _mod_kernel(cond_ref, w_ref, b_ref, o_ref):
    s = _silu(cond_ref[...]).astype(BF16)
    acc = jnp.dot(s, w_ref[0].astype(BF16), preferred_element_type=F32)
    o_ref[0] = acc + b_ref[0]


def _modulation(cond8, ada_w, ada_b):
    depth, d, n = ada_w.shape
    tn = 1024
    return pl.pallas_call(
        _mod_kernel,
        grid=(depth, n // tn),
        in_specs=[
            pl.BlockSpec((8, d), lambda l, j: (0, 0)),
            pl.BlockSpec((1, d, tn), lambda l, j: (l, 0, j)),
            pl.BlockSpec((1, 1, tn), lambda l, j: (l, 0, j)),
        ],
        out_specs=pl.BlockSpec((1, 8, tn), lambda l, j: (l, 0, j)),
        out_shape=jax.ShapeDtypeStruct((depth, 8, n), F32),
        compiler_params=_params("parallel", "parallel"),
        name="adaln_mod",
    )(cond8, ada_w, ada_b.reshape(depth, 1, n))


def _modulate_kernel(x_ref, ctx_ref, sc_ref, sh_ref, h_ref, u_ref, *, n_lat_tiles):
    def emit(src_ref):
        hv = src_ref[...]
        h_ref[...] = hv
        u_ref[...] = (hv * (1.0 + sc_ref[0]) + sh_ref[0]).astype(BF16)

    is_lat = pl.program_id(0) < n_lat_tiles
    pl.when(is_lat)(lambda: emit(x_ref))
    pl.when(jnp.logical_not(is_lat))(lambda: emit(ctx_ref))


def _modulate(x2d, ctx2d, mod, sc_idx, sh_idx, tm):
    n_lat, d = x2d.shape
    n_ctx = ctx2d.shape[0]
    t = n_lat + n_ctx
    n_lat_tiles = n_lat // tm
    vec = lambda f: pl.BlockSpec((1, 1, d), lambda i: (f(i), 0, 0))
    row = pl.BlockSpec((tm, d), lambda i: (i, 0))
    return pl.pallas_call(
        functools.partial(_modulate_kernel, n_lat_tiles=n_lat_tiles),
        grid=(t // tm,),
        in_specs=[
            pl.BlockSpec((tm, d), lambda i: (jnp.minimum(i, n_lat_tiles - 1), 0)),
            pl.BlockSpec((tm, d), lambda i: (jnp.maximum(i - n_lat_tiles, 0), 0)),
            vec(sc_idx), vec(sh_idx),
        ],
        out_specs=[row, row],
        out_shape=[jax.ShapeDtypeStruct((t, d), F32), jax.ShapeDtypeStruct((t, d), BF16)],
        compiler_params=_params("arbitrary"),
        name="modulate_in",
    )(x2d, ctx2d, mod, mod)


def _cast_weights_once(pairs):
    @pl.when(pl.program_id(1) == 0)
    def _():
        for src_ref, dst_ref in pairs:
            dst_ref[...] = src_ref[...].astype(BF16)


def _row_groups(ref, n_split):
    rows_per = ref.shape[0] // n_split
    return [pl.ds(r * rows_per, rows_per) for r in range(n_split)]


def _ret_qkv_kernel(x_ref, w_ref, cos_ref, sin_ref, o_ref, wb_ref, *, n_q, n_k, k_scale, n_split):
    j = pl.program_id(0)
    tn = o_ref.shape[1]
    _cast_weights_once([(w_ref, wb_ref)])

    def product(rows):
        return jnp.dot(x_ref[rows, :], wb_ref[...], preferred_element_type=F32)

    @pl.when(j < n_q + n_k)
    def _():
        scale = jnp.where(j < n_q, 1.0, k_scale).astype(F32)
        for rows in _row_groups(o_ref, n_split):
            acc = product(rows)
            c = cos_ref[rows, :]
            s = sin_ref[rows, :]
            for g in range(tn // MXU_DIM):
                lo = g * MXU_DIM
                x1 = acc[:, lo:lo + LANES] * scale
                x2 = acc[:, lo + LANES:lo + MXU_DIM] * scale
                o_ref[rows, lo:lo + LANES] = (x1 * c - x2 * s).astype(BF16)
                o_ref[rows, lo + LANES:lo + MXU_DIM] = (x1 * s + x2 * c).astype(BF16)

    @pl.when(j >= n_q + n_k)
    def _():
        for rows in _row_groups(o_ref, n_split):
            o_ref[rows, :] = product(rows).astype(BF16)


def _ret_qkv(u, w, layer, cos, sin, n_heads, tm):
    t, d = u.shape
    n = w.shape[2]
    dk = d // n_heads
    tn = 1024
    n_q = n_heads * dk // tn
    kern = functools.partial(_ret_qkv_kernel, n_q=n_q, n_k=n_q, k_scale=dk ** -0.5, n_split=2)
    return pl.pallas_call(
        kern,
        grid=(n // tn, t // tm),
        in_specs=[
            pl.BlockSpec((tm, d), lambda j, i: (i, 0)),
            pl.BlockSpec((None, d, tn), lambda j, i: (layer, 0, j)),
            pl.BlockSpec((tm, LANES), lambda j, i: (i, 0)),
            pl.BlockSpec((tm, LANES), lambda j, i: (i, 0)),
        ],
        out_specs=pl.BlockSpec((tm, tn), lambda j, i: (i, j)),
        out_shape=jax.ShapeDtypeStruct((t, n), BF16),
        scratch_shapes=[pltpu.VMEM((d, tn), BF16)],
        compiler_params=_params("parallel", "arbitrary"),
        name="ret_qkv",
    )(u, w, cos, sin)


def _retention_kernel(dec_ref, qf_ref, qb_ref, of_ref, ob_ref, s_ref, intra_ref, qdec_ref, kdec_ref,
                      *, n_heads, dk, dv):
    c = RET_CHUNK
    step = pl.program_id(1)

    def log_gamma(idx):
        x = dec_ref[idx][0:1, 0:1]
        return jnp.minimum(x, 0.0) - jnp.log1p(jnp.exp(-jnp.abs(x)))

    @pl.when(step == 0)
    def _():
        s_ref[...] = jnp.zeros_like(s_ref)
        row = lax.broadcasted_iota(jnp.int32, (c, c), 0)
        col = lax.broadcasted_iota(jnp.int32, (c, c), 1)
        pos = lax.broadcasted_iota(jnp.int32, (c, dk), 0).astype(F32)
        for d in range(2):
            diff = (row - col) if d == 0 else (col - row)
            dist = jnp.maximum(diff, 0).astype(F32)
            ahead = pos + 1.0 if d == 0 else float(c) - pos
            behind = float(c - 1) - pos if d == 0 else pos
            for h in range(n_heads):
                idx = d * n_heads + h
                lg = log_gamma(idx)
                intra_ref[idx] = jnp.where(diff >= 0, jnp.exp(lg * dist), 0.0)
                qdec_ref[idx] = jnp.exp(lg * ahead).astype(BF16)
                kdec_ref[idx] = jnp.exp(lg * behind).astype(BF16)

    for d, (x_ref, o_ref) in enumerate(((qf_ref, of_ref), (qb_ref, ob_ref))):
        for h in range(n_heads):
            idx = d * n_heads + h
            q = x_ref[:, h * dk:(h + 1) * dk]
            k = x_ref[:, n_heads * dk + h * dk:n_heads * dk + (h + 1) * dk]
            v = x_ref[:, 2 * n_heads * dk + h * dv:2 * n_heads * dk + (h + 1) * dv]
            scores = lax.dot_general(q, k, (((1,), (1,)), ((), ())), preferred_element_type=F32)
            p = (scores * intra_ref[idx]).astype(BF16)
            state = s_ref[idx]
            o = (jnp.dot(p, v, preferred_element_type=F32)
                 + jnp.dot(q * qdec_ref[idx], state.astype(BF16), preferred_element_type=F32))
            s_ref[idx] = state * jnp.exp(log_gamma(idx) * float(c)) + lax.dot_general(
                k * kdec_ref[idx], v, (((0,), (0,)), ((), ())), preferred_element_type=F32)
            mu = jnp.mean(o, axis=-1, keepdims=True)
            ctr = o - mu
            var = jnp.mean(ctr * ctr, axis=-1, keepdims=True)
            o_ref[:, h * dv:(h + 1) * dv] = (ctr * lax.rsqrt(var + GN_EPS)).astype(BF16)


def _retention(qkv, decay_tile, batch, seq, n_heads):
    t, n = qkv.shape
    d = n // 4
    dk, dv = d // n_heads, 2 * d // n_heads
    c = RET_CHUNK
    n_lat = seq // c
    ctx_blk = batch * n_lat

    def fwd_blk(b, s):
        return jnp.where(s == 0, ctx_blk + b, b * n_lat + s - 1)

    def bwd_blk(b, s):
        return jnp.where(s == 0, ctx_blk + b, b * n_lat + n_lat - s)

    kern = functools.partial(_retention_kernel, n_heads=n_heads, dk=dk, dv=dv)
    out = jax.ShapeDtypeStruct((t, n_heads * dv), BF16)
    return pl.pallas_call(
        kern,
        grid=(batch, n_lat + 1),
        in_specs=[
            pl.BlockSpec((2 * n_heads, 8, LANES), lambda b, s: (0, 0, 0)),
            pl.BlockSpec((c, n), lambda b, s: (fwd_blk(b, s), 0)),
            pl.BlockSpec((c, n), lambda b, s: (bwd_blk(b, s), 0)),
        ],
        out_specs=[
            pl.BlockSpec((c, n_heads * dv), lambda b, s: (fwd_blk(b, s), 0)),
            pl.BlockSpec((c, n_heads * dv), lambda b, s: (bwd_blk(b, s), 0)),
        ],
        out_shape=[out, out],
        scratch_shapes=[
            pltpu.VMEM((2 * n_heads, dk, dv), F32),
            pltpu.VMEM((2 * n_heads, c, c), F32),
            pltpu.VMEM((2 * n_heads, c, dk), BF16),
            pltpu.VMEM((2 * n_heads, c, dk), BF16),
        ],
        compiler_params=_params("parallel", "arbitrary"),
        name="retention_scan",
    )(decay_tile, qkv, qkv)


def _ret_gate_kernel(x_ref, wf_ref, wb_ref, of_ref, ob_ref, wnext_ref, y_ref, wnext_bf_ref, wf_bf, wb_bf,
                     *, n_split):
    _cast_weights_once([(wf_ref, wf_bf), (wb_ref, wb_bf), (wnext_ref, wnext_bf_ref)])
    for rows in _row_groups(y_ref, n_split):
        x = x_ref[rows, :]
        gf = jnp.dot(x, wf_bf[...], preferred_element_type=F32)
        gb = jnp.dot(x, wb_bf[...], preferred_element_type=F32)
        y = _silu(gf) * of_ref[rows, :].astype(F32) + _silu(gb) * ob_ref[rows, :].astype(F32)
        y_ref[rows, :] = y.astype(BF16)


def _next_weight_specs(w_next, layer, nb):
    k_next, d_next = w_next.shape[1:]
    assert k_next % nb == 0
    slab = k_next // nb
    return (pl.BlockSpec((None, slab, d_next), lambda j, i: (layer, j, 0)),
            pl.BlockSpec((slab, d_next), lambda j, i: (j, 0)),
            jax.ShapeDtypeStruct((k_next, d_next), BF16))


def _ret_gate(u, w_g, layer, o_f, o_b, w_next, tm):
    t, d = u.shape
    n = w_g.shape[2] // 2
    tn = 512
    nb = n // tn
    next_in, next_out, next_shape = _next_weight_specs(w_next, layer, nb)
    return pl.pallas_call(
        functools.partial(_ret_gate_kernel, n_split=2),
        grid=(nb, t // tm),
        in_specs=[
            pl.BlockSpec((tm, d), lambda j, i: (i, 0)),
            pl.BlockSpec((None, d, tn), lambda j, i: (layer, 0, j)),
            pl.BlockSpec((None, d, tn), lambda j, i: (layer, 0, nb + j)),
            pl.BlockSpec((tm, tn), lambda j, i: (i, j)),
            pl.BlockSpec((tm, tn), lambda j, i: (i, j)),
            next_in,
        ],
        out_specs=[pl.BlockSpec((tm, tn), lambda j, i: (i, j)), next_out],
        out_shape=[jax.ShapeDtypeStruct((t, n), BF16), next_shape],
        scratch_shapes=[pltpu.VMEM((d, tn), BF16), pltpu.VMEM((d, tn), BF16)],
        compiler_params=_params("parallel", "arbitrary"),
        name="ret_gate",
    )(u, w_g, w_g, o_f, o_b, w_next)


def _out_ln_kernel(*refs, alpha, n_k, n_split, group, n_x, n_lat_tiles):
    x_refs, refs = refs[:n_x], refs[n_x:]
    w_ref, h_ref, gate_ref, lng_ref, lnb_ref, sc_ref, sh_ref, hout_ref, u_ref, *acc = refs
    k = pl.program_id(1)
    is_lat = group * pl.program_id(0) + pl.program_id(2) < n_lat_tiles

    def x_rows(rows):
        if n_x == 1:
            return x_refs[0][rows, :]
        return jnp.where(is_lat, x_refs[0][rows, :], x_refs[1][rows, :])

    def partial_product(rows):
        return jnp.dot(x_rows(rows), w_ref[...], preferred_element_type=F32)

    def finish(rows, y):
        z = alpha * h_ref[rows, :] + gate_ref[0] * y
        mu = jnp.mean(z, axis=-1, keepdims=True)
        ctr = z - mu
        var = jnp.mean(ctr * ctr, axis=-1, keepdims=True)
        hn = ctr * lax.rsqrt(var + LN_EPS) * lng_ref[0] + lnb_ref[0]
        hout_ref[rows, :] = hn
        u_ref[rows, :] = (hn * (1.0 + sc_ref[0]) + sh_ref[0]).astype(BF16)

    def last_step(prev):
        for rows in _row_groups(hout_ref, n_split):
            y = partial_product(rows)
            finish(rows, y if prev is None else prev[rows, :] + y)

    if n_k == 1:
        last_step(None)
        return
    acc_ref = acc[0].at[pl.program_id(2)]

    @pl.when(k == 0)
    def _():
        acc_ref[...] = partial_product(slice(None))

    @pl.when(jnp.logical_and(k > 0, k < n_k - 1))
    def _():
        acc_ref[...] += partial_product(slice(None))

    @pl.when(k == n_k - 1)
    def _():
        last_step(acc_ref)


def _out_ln(x, w, layer, h, mod, ln_g, ln_b, *, gate_idx, ln_idx, sc_idx, sh_idx, alpha, n_rows, tm, tk,
            group, n_split=2):
    x_parts = x if isinstance(x, tuple) else (x,)
    kdim = x_parts[0].shape[1]
    d = w.shape[2]
    n_k = kdim // tk
    if n_rows % (group * tm):
        group = 1
    assert n_rows % (group * tm) == 0 and kdim % tk == 0
    n_lat_tiles = x_parts[0].shape[0] // tm if len(x_parts) == 2 else 0

    def row_blk(i, k, m):
        return group * i + jnp.where(k == n_k - 1, m, 0)

    if len(x_parts) == 1:
        x_specs = [pl.BlockSpec((tm, tk), lambda i, k, m: (group * i + m, k))]
    else:
        assert x_parts[0].shape[0] == n_lat_tiles * tm
        x_specs = [
            pl.BlockSpec((tm, tk), lambda i, k, m: (jnp.minimum(group * i + m, n_lat_tiles - 1), k)),
            pl.BlockSpec((tm, tk), lambda i, k, m: (jnp.maximum(group * i + m - n_lat_tiles, 0), k)),
        ]

    vec = lambda arr_idx: pl.BlockSpec((1, 1, d), lambda i, k, m: (arr_idx(row_blk(i, k, m)), 0, 0))
    row = pl.BlockSpec((tm, d), lambda i, k, m: (row_blk(i, k, m), 0))
    kern = functools.partial(_out_ln_kernel, alpha=alpha, n_k=n_k, n_split=n_split, group=group,
                             n_x=len(x_parts), n_lat_tiles=n_lat_tiles)
    return pl.pallas_call(
        kern,
        grid=(n_rows // (group * tm), n_k, group),
        in_specs=[
            *x_specs,
            pl.BlockSpec((None, tk, d), lambda i, k, m: (layer, k, 0)),
            row,
            vec(gate_idx),
            vec(lambda r: ln_idx),
            vec(lambda r: ln_idx),
            vec(sc_idx),
            vec(sh_idx),
        ],
        out_specs=[row, row],
        out_shape=[
            jax.ShapeDtypeStruct((n_rows, d), F32),
            jax.ShapeDtypeStruct((n_rows, d), BF16),
        ],
        scratch_shapes=[pltpu.VMEM((group, tm, d), F32)] if n_k > 1 else [],
        compiler_params=_params("parallel", "arbitrary", "arbitrary"),
        name="out_ln",
    )(*x_parts, w, h, mod, ln_g, ln_b, mod, mod)


def _ffn_in_kernel(x_ref, wa_ref, wb_ref, wnext_ref, o_ref, wnext_bf_ref, wa_bf, wb_bf, *, n_split):
    _cast_weights_once([(wa_ref, wa_bf), (wb_ref, wb_bf), (wnext_ref, wnext_bf_ref)])
    for rows in _row_groups(o_ref, n_split):
        x = x_ref[rows, :]
        a = jnp.dot(x, wa_bf[...], preferred_element_type=F32)
        b = jnp.dot(x, wb_bf[...], preferred_element_type=F32)
        o_ref[rows, :] = (_silu(a) * b).astype(BF16)


def _ffn_in(u, w_in, layer, w_next, n_rows, tm):
    d = u.shape[1]
    f = w_in.shape[2] // 2
    tn = 512
    nb = f // tn
    next_in, next_out, next_shape = _next_weight_specs(w_next, layer, nb)
    return pl.pallas_call(
        functools.partial(_ffn_in_kernel, n_split=2),
        grid=(nb, n_rows // tm),
        in_specs=[
            pl.BlockSpec((tm, d), lambda j, i: (i, 0)),
            pl.BlockSpec((None, d, tn), lambda j, i: (layer, 0, j)),
            pl.BlockSpec((None, d, tn), lambda j, i: (layer, 0, nb + j)),
            next_in,
        ],
        out_specs=[pl.BlockSpec((tm, tn), lambda j, i: (i, j)), next_out],
        out_shape=[jax.ShapeDtypeStruct((n_rows, f), BF16), next_shape],
        scratch_shapes=[pltpu.VMEM((d, tn), BF16), pltpu.VMEM((d, tn), BF16)],
        compiler_params=_params("parallel", "arbitrary"),
        name="ffn_in",
    )(u, w_in, w_in, w_next)


def _rope_lanes(x, c_ref, s_up_ref, s_dn_ref):
    half = MLA_D_ROPE // 4
    up = pltpu.roll(x, LANES - half, 1)
    dn = pltpu.roll(x, half, 1)
    return x * c_ref[...] + up * s_up_ref[...] + dn * s_dn_ref[...]


def _rms(x, g):
    return x * lax.rsqrt(jnp.mean(x * x, axis=-1, keepdims=True) + RMS_EPS) * g


def _mla_down_kernel(x_ref, w_ref, gq_ref, gkv_ref, c_ref, su_ref, sd_ref,
                     cq_ref, ckv_ref, kr_ref, *, rq, rkv):
    for rows in _row_groups(x_ref, 2):
        acc = jnp.dot(x_ref[rows, :], w_ref[...], preferred_element_type=F32)
        cq_ref[rows, :] = _rms(acc[:, :rq], gq_ref[...]).astype(BF16)
        ckv_ref[rows, :] = _rms(acc[:, rq:rq + rkv], gkv_ref[...]).astype(BF16)
        kr_ref[rows, :] = _rope_lanes(acc[:, rq + rkv:], c_ref.at[rows, :], su_ref.at[rows, :],
                                      sd_ref.at[rows, :]).astype(BF16)


def _mla_down(u, w_down, g_q, g_kv, tabs, tm):
    t, d = u.shape
    rq, rkv = g_q.shape[1], g_kv.shape[1]
    n = w_down.shape[1]
    tab = pl.BlockSpec((tm, LANES), lambda i: (i, 0))
    kern = functools.partial(_mla_down_kernel, rq=rq, rkv=rkv)
    return pl.pallas_call(
        kern,
        grid=(t // tm,),
        in_specs=[
            pl.BlockSpec((tm, d), lambda i: (i, 0)),
            pl.BlockSpec((d, n), lambda i: (0, 0)),
            pl.BlockSpec((1, rq), lambda i: (0, 0)),
            pl.BlockSpec((1, rkv), lambda i: (0, 0)),
            tab, tab, tab,
        ],
        out_specs=[
            pl.BlockSpec((tm, rq), lambda i: (i, 0)),
            pl.BlockSpec((tm, rkv), lambda i: (i, 0)),
            pl.BlockSpec((tm, LANES), lambda i: (i, 0)),
        ],
        out_shape=[
            jax.ShapeDtypeStruct((t, rq), BF16),
            jax.ShapeDtypeStruct((t, rkv), BF16),
            jax.ShapeDtypeStruct((t, LANES), BF16),
        ],
        compiler_params=_params("parallel"),
        name="mla_down",
    )(u, w_down, g_q, g_kv, *tabs)


def _mla_up_kernel(cq_ref, ckv_ref, kr_ref, wq_ref, wkv_ref, c_ref, su_ref, sd_ref,
                   q_ref, k_ref, v_ref):
    cq = cq_ref[...]
    ckv = ckv_ref[...]
    kr = kr_ref[...]
    hp = MLA_HEAD_PAD
    for h in range(MLA_HEADS):
        qh = jnp.dot(cq, wq_ref[:, h * hp:(h + 1) * hp], preferred_element_type=F32) * MLA_Q_SCALE
        q_ref[:, h * hp:h * hp + LANES] = qh[:, :LANES].astype(BF16)
        q_ref[:, h * hp + LANES:(h + 1) * hp] = _rope_lanes(qh[:, LANES:], c_ref, su_ref, sd_ref).astype(BF16)
        kvh = jnp.dot(ckv, wkv_ref[:, h * hp:(h + 1) * hp], preferred_element_type=F32)
        k_ref[:, h * hp:h * hp + LANES] = kvh[:, :LANES].astype(BF16)
        k_ref[:, h * hp + LANES:(h + 1) * hp] = kr
        v_ref[:, h * hp:h * hp + MLA_D_V] = kvh[:, LANES:].astype(BF16)
        v_ref[:, h * hp + MLA_D_V:(h + 1) * hp] = jnp.ones((kvh.shape[0], hp - MLA_D_V), BF16)


def _mla_up(cq, ckv, kr, wq_pad, wkv, tabs, tm):
    t, rq = cq.shape
    rkv = ckv.shape[1]
    nq = wq_pad.shape[1]
    nkv = wkv.shape[1]
    tab = pl.BlockSpec((tm, LANES), lambda i: (i, 0))
    return pl.pallas_call(
        _mla_up_kernel,
        grid=(t // tm,),
        in_specs=[
            pl.BlockSpec((tm, rq), lambda i: (i, 0)),
            pl.BlockSpec((tm, rkv), lambda i: (i, 0)),
            pl.BlockSpec((tm, LANES), lambda i: (i, 0)),
            pl.BlockSpec((rq, nq), lambda i: (0, 0)),
            pl.BlockSpec((rkv, nkv), lambda i: (0, 0)),
            tab, tab, tab,
        ],
        out_specs=[
            pl.BlockSpec((tm, nq), lambda i: (i, 0)),
            pl.BlockSpec((tm, nq), lambda i: (i, 0)),
            pl.BlockSpec((tm, nq), lambda i: (i, 0)),
        ],
        out_shape=[
            jax.ShapeDtypeStruct((t, nq), BF16),
            jax.ShapeDtypeStruct((t, nq), BF16),
            jax.ShapeDtypeStruct((t, nq), BF16),
        ],
        compiler_params=_params("parallel"),
        name="mla_up",
    )(cq, ckv, kr, wq_pad, wkv, *tabs)


def _scores(q, k):
    return lax.dot_general(q, k, (((1,), (1,)), ((), ())), preferred_element_type=F32)


def _attn_lat_kernel(q_ref, kl_ref, kc_ref, vl_ref, vc_ref, o_ref, *, key_chunk):
    q = q_ref[...]
    n_lat = kl_ref.shape[0] // key_chunk
    chunks = [(kl_ref, vl_ref, c * key_chunk, key_chunk) for c in range(n_lat)]
    chunks.append((kc_ref, vc_ref, 0, kc_ref.shape[0]))

    def chunk_scores(c):
        k_ref, _, lo, size = chunks[c]
        return _scores(q, k_ref[lo:lo + size, :])

    s = chunk_scores(0)
    m = acc = None
    for c, (_, v_ref, lo, size) in enumerate(chunks):
        s_next = chunk_scores(c + 1) if c + 1 < len(chunks) else None
        m_c = jnp.max(s, axis=-1, keepdims=True)
        v = v_ref[lo:lo + size, :]
        if c == 0:
            m = m_c
            acc = jnp.dot(jnp.exp2(s - m).astype(BF16), v, preferred_element_type=F32)
        else:
            m_new = jnp.maximum(m, m_c)
            acc = (jnp.exp2(m - m_new) * acc
                   + jnp.dot(jnp.exp2(s - m_new).astype(BF16), v, preferred_element_type=F32))
            m = m_new
        s = s_next
    _store_normalised(o_ref, acc)


def _store_normalised(o_ref, acc):
    o_ref[...] = (acc[:, :MLA_D_V] / acc[:, MLA_D_V:MLA_D_V + 1]).astype(BF16)


def _attn_ctx_kernel(q_ref, kc_ref, vc_ref, o_ref):
    hp, dv = MLA_HEAD_PAD, MLA_D_V
    for h in range(MLA_HEADS):
        cols = slice(h * hp, (h + 1) * hp)
        s_c = _scores(q_ref[:, cols], kc_ref[:, cols])
        p_c = jnp.exp2(s_c - jnp.max(s_c, axis=-1, keepdims=True))
        _store_normalised(o_ref.at[:, h * dv:(h + 1) * dv],
                          jnp.dot(p_c.astype(BF16), vc_ref[:, cols], preferred_element_type=F32))


def _attention(q, k, v, batch, seq, ctx_len, with_ctx_queries):
    hp, dv = MLA_HEAD_PAD, MLA_D_V
    tq = min(seq, ATTN_Q_TILE)
    nq = seq // tq
    ctx0 = batch * seq // ctx_len
    o_lat = pl.pallas_call(
        functools.partial(_attn_lat_kernel, key_chunk=min(seq, ATTN_KEY_CHUNK)),
        grid=(batch, MLA_HEADS, nq),
        in_specs=[
            pl.BlockSpec((tq, hp), lambda b, h, i: (b * nq + i, h)),
            pl.BlockSpec((seq, hp), lambda b, h, i: (b, h)),
            pl.BlockSpec((ctx_len, hp), lambda b, h, i: (ctx0 + b, h)),
            pl.BlockSpec((seq, hp), lambda b, h, i: (b, h)),
            pl.BlockSpec((ctx_len, hp), lambda b, h, i: (ctx0 + b, h)),
        ],
        out_specs=pl.BlockSpec((tq, dv), lambda b, h, i: (b * nq + i, h)),
        out_shape=jax.ShapeDtypeStruct((batch * seq, MLA_HEADS * dv), BF16),
        compiler_params=_params("parallel", "parallel", "arbitrary"),
        name="attn_lat",
    )(q, k, k, v, v)
    if not with_ctx_queries:
        return o_lat
    o_ctx = pl.pallas_call(
        _attn_ctx_kernel,
        grid=(batch,),
        in_specs=[
            pl.BlockSpec((ctx_len, MLA_HEADS * hp), lambda b: (ctx0 + b, 0)),
            pl.BlockSpec((ctx_len, MLA_HEADS * hp), lambda b: (ctx0 + b, 0)),
            pl.BlockSpec((ctx_len, MLA_HEADS * hp), lambda b: (ctx0 + b, 0)),
        ],
        out_specs=pl.BlockSpec((ctx_len, MLA_HEADS * dv), lambda b: (b, 0)),
        out_shape=jax.ShapeDtypeStruct((batch * ctx_len, MLA_HEADS * dv), BF16),
        compiler_params=_params("parallel"),
        name="attn_ctx",
    )(q, k, v)
    return o_lat, o_ctx


def _retention_tables(seq, n_ctx, dk):
    inv = RET_ROPE_BASE ** (-jnp.linspace(0.0, 1.0, dk // 2, dtype=F32))
    ang = jnp.arange(seq, dtype=F32)[:, None] * inv[None, :]
    return jnp.cos(ang), jnp.sin(ang), jnp.ones((n_ctx, dk // 2), F32), jnp.zeros((n_ctx, dk // 2), F32)


def _axial_tables(seq):
    n_f = MLA_D_ROPE // 4
    inv = AXIAL_ROPE_BASE ** (-jnp.arange(n_f, dtype=F32) * 2.0 / (MLA_D_ROPE // 2))
    tpos = jnp.arange(seq)
    row = (tpos // GRID_W).astype(F32)[:, None] * inv[None, :]
    col = (tpos % GRID_W).astype(F32)[:, None] * inv[None, :]
    z = jnp.zeros((seq, n_f), F32)
    tail = LANES - MLA_D_ROPE
    cos = jnp.concatenate([jnp.cos(row), jnp.cos(row), jnp.cos(col), jnp.cos(col), jnp.ones((seq, tail), F32)], 1)
    s_up = jnp.concatenate([-jnp.sin(row), z, -jnp.sin(col), z, jnp.zeros((seq, tail), F32)], 1)
    s_dn = jnp.concatenate([z, jnp.sin(row), z, jnp.sin(col), jnp.zeros((seq, tail), F32)], 1)
    return cos, s_up, s_dn


def _token_table(lat, batch, ctx_rows, fill):
    return jnp.concatenate([jnp.tile(lat, (batch, 1)), jnp.full((ctx_rows, lat.shape[1]), fill, F32)], 0)


def kernel(x, c, ctx, c_ctx, ada_w, ada_b, ln_g, ln_b, ret_w_qkv, ret_w_g, ret_decay_logit, ret_w_o,
           mla_w_dq, mla_g_q, mla_w_uq, mla_w_dkv, mla_g_kv, mla_w_ukv, mla_w_o, ffn_w_in, ffn_w_out):
    batch, seq, d = x.shape
    ctx_len = ctx.shape[1]
    depth = ada_w.shape[0]
    n_lat, n_ctx = batch * seq, batch * ctx_len
    n_tok = n_lat + n_ctx
    ret_heads = ret_decay_logit.shape[-1]
    assert ctx_len == RET_CHUNK and seq % RET_CHUNK == 0 and batch + 1 <= 8
    tm = _row_tile(n_lat, n_ctx)
    alpha = (2 * depth) ** 0.25

    cond8 = jnp.concatenate([c, c_ctx[None, :], jnp.zeros((8 - batch - 1, d), F32)], 0)
    mod = _modulation(cond8, ada_w, ada_b).reshape(depth * 8 * 6, 1, d)

    def mod_idx(layer, chunk, tile):
        def f(i):
            r = jnp.where(i * tile >= n_lat, batch, (i * tile) // seq)
            return (layer * 8 + r) * 6 + chunk
        return f

    ln_g3 = ln_g.reshape(depth * 2, 1, d)
    ln_b3 = ln_b.reshape(depth * 2, 1, d)

    ret_cos, ret_sin, one_c, zero_c = _retention_tables(seq, n_ctx, d // ret_heads)
    ret_cos = jnp.concatenate([jnp.tile(ret_cos, (batch, 1)), one_c], 0)
    ret_sin = jnp.concatenate([jnp.tile(ret_sin, (batch, 1)), zero_c], 0)
    ax_cos, ax_up, ax_dn = _axial_tables(seq)
    ax_tabs = (_token_table(ax_cos, batch, n_ctx, 1.0), _token_table(ax_up, batch, n_ctx, 0.0),
               _token_table(ax_dn, batch, n_ctx, 0.0))

    h, u = _modulate(x.reshape(n_lat, d), ctx.reshape(n_ctx, d), mod,
                     mod_idx(0, 1, TM_SMALL), mod_idx(0, 0, TM_SMALL), TM_SMALL)

    mla_w_o_b = mla_w_o.astype(BF16)

    for i in range(depth):
        last = i == depth - 1
        n_rows = n_lat if last else n_tok
        j = i // 2
        if i % 2 == 0:
            qkv = _ret_qkv(u, ret_w_qkv, j, ret_cos, ret_sin, ret_heads, tm)
            decay_tile = jnp.broadcast_to(
                ret_decay_logit[j].astype(F32).reshape(2 * ret_heads, 1, 1), (2 * ret_heads, 8, LANES))
            o_f, o_b = _retention(qkv, decay_tile, batch, seq, ret_heads)
            y, w_o = _ret_gate(u, ret_w_g, j, o_f, o_b, ret_w_o, tm)
            w_o, w_o_layer = w_o[None], 0
        else:
            rq, rkv = mla_g_q.shape[1], mla_g_kv.shape[1]
            w_down = jnp.concatenate(
                [mla_w_dq[j], mla_w_dkv[j], jnp.zeros((d, LANES - MLA_D_ROPE), F32)], 1).astype(BF16)
            cq, ckv, kr = _mla_down(u, w_down, mla_g_q[j][None, :], mla_g_kv[j][None, :], ax_tabs, TM_SMALL)
            wq_pad = jnp.pad(
                mla_w_uq[j].reshape(rq, MLA_HEADS, MLA_D_NOPE + MLA_D_ROPE),
                ((0, 0), (0, 0), (0, MLA_HEAD_PAD - MLA_D_NOPE - MLA_D_ROPE)),
            ).reshape(rq, MLA_HEADS * MLA_HEAD_PAD).astype(BF16)
            q, k, v = _mla_up(cq, ckv, kr, wq_pad, mla_w_ukv[j].astype(BF16), ax_tabs, TM_SMALL)
            y = _attention(q, k, v, batch, seq, ctx_len, not last)
            w_o, w_o_layer = mla_w_o_b, j

        tm_o, grp_o, tk_o = OUT_TILES_MIXER
        h, u = _out_ln(y, w_o, w_o_layer, h, mod, ln_g3, ln_b3, gate_idx=mod_idx(i, 2, tm_o), ln_idx=2 * i,
                       sc_idx=mod_idx(i, 4, tm_o), sh_idx=mod_idx(i, 3, tm_o), alpha=alpha,
                       n_rows=n_rows, tm=tm_o, tk=min(tk_o, w_o.shape[1]), group=grp_o)
        hid, ffn_w_out_b = _ffn_in(u, ffn_w_in, i, ffn_w_out, n_rows, tm)
        nxt = min(i + 1, depth - 1)
        tm_o, grp_o, k_steps = OUT_TILES_FFN
        h, u = _out_ln(hid, ffn_w_out_b[None], 0, h, mod, ln_g3, ln_b3, gate_idx=mod_idx(i, 5, tm_o),
                       ln_idx=2 * i + 1, sc_idx=mod_idx(nxt, 1, tm_o), sh_idx=mod_idx(nxt, 0, tm_o),
                       alpha=alpha, n_rows=n_rows, tm=tm_o, tk=ffn_w_out.shape[1] // k_steps, group=grp_o)
    return h.reshape(batch, seq, d)
```

```python
import functools

import jax
import jax.numpy as jnp
from jax import lax
from jax.experimental import pallas as pl
from jax.experimental.pallas import tpu as pltpu

F32 = jnp.float32
BF16 = jnp.bfloat16

GRID_W = 64
RET_ROPE_BASE = 10000.0
GN_EPS = 1e-6
MLA_HEADS = 16
MLA_D_NOPE = 128
MLA_D_ROPE = 64
MLA_D_V = 128
MLA_SCALE = (MLA_D_NOPE + MLA_D_ROPE) ** -0.5
MLA_Q_SCALE = MLA_SCALE * 1.4426950408889634
AXIAL_ROPE_BASE = 10000.0
RMS_EPS = 1e-6
LN_EPS = 1e-5

LANES = 128
MXU_DIM = 256
VMEM_LIMIT_BYTES = 56 * 1024 * 1024

RET_CHUNK = 256

TM_SMALL = 512
OUT_TILES_MIXER = (512, 2, 2048)
OUT_TILES_FFN = (512, 2, 4)
ATTN_Q_TILE = 1024
ATTN_KEY_CHUNK = 2048
MLA_HEAD_PAD = 2 * LANES


def _params(*sem):
    return pltpu.CompilerParams(dimension_semantics=sem, vmem_limit_bytes=VMEM_LIMIT_BYTES)


def _silu(x):
    return 0.5 * x * (1.0 + jnp.tanh(0.5 * x))


def _row_tile(n_lat, n_ctx):
    for t in (1024, 512, 256):
        if n_lat % t == 0 and n_ctx % t == 0:
            return t
    raise ValueError("token counts must be multiples of 256")


def _mod_kernel(cond_ref, w_ref, b_ref, o_ref):
    s = _silu(cond_ref[...]).astype(BF16)
    acc = jnp.dot(s, w_ref[0].astype(BF16), preferred_element_type=F32)
    o_ref[0] = acc + b_ref[0]


def _modulation(cond8, ada_w, ada_b):
    depth, d, n = ada_w.shape
    tn = 1024
    return pl.pallas_call(
        _mod_kernel,
        grid=(depth, n // tn),
        in_specs=[
            pl.BlockSpec((8, d), lambda l, j: (0, 0)),
            pl.BlockSpec((1, d, tn), lambda l, j: (l, 0, j)),
            pl.BlockSpec((1, 1, tn), lambda l, j: (l, 0, j)),
        ],
        out_specs=pl.BlockSpec((1, 8, tn), lambda l, j: (l, 0, j)),
        out_shape=jax.ShapeDtypeStruct((depth, 8, n), F32),
        compiler_params=_params("parallel", "parallel"),
        name="adaln_mod",
    )(cond8, ada_w, ada_b.reshape(depth, 1, n))


def _modulate_kernel(x_ref, ctx_ref, sc_ref, sh_ref, h_ref, u_ref, *, n_lat_tiles):
    def emit(src_ref):
        hv = src_ref[...]
        h_ref[...] = hv
        u_ref[...] = (hv * (1.0 + sc_ref[0]) + sh_ref[0]).astype(BF16)

    is_lat = pl.program_id(0) < n_lat_tiles
    pl.when(is_lat)(lambda: emit(x_ref))
    pl.when(jnp.logical_not(is_lat))(lambda: emit(ctx_ref))


def _modulate(x2d, ctx2d, mod, sc_idx, sh_idx, tm):
    n_lat, d = x2d.shape
    n_ctx = ctx2d.shape[0]
    t = n_lat + n_ctx
    n_lat_tiles = n_lat // tm
    vec = lambda f: pl.BlockSpec((1, 1, d), lambda i: (f(i), 0, 0))
    row = pl.BlockSpec((tm, d), lambda i: (i, 0))
    return pl.pallas_call(
        functools.partial(_modulate_kernel, n_lat_tiles=n_lat_tiles),
        grid=(t // tm,),
        in_specs=[
            pl.BlockSpec((tm, d), lambda i: (jnp.minimum(i, n_lat_tiles - 1), 0)),
            pl.BlockSpec((tm, d), lambda i: (jnp.maximum(i - n_lat_tiles, 0), 0)),
            vec(sc_idx), vec(sh_idx),
        ],
        out_specs=[row, row],
        out_shape=[jax.ShapeDtypeStruct((t, d), F32), jax.ShapeDtypeStruct((t, d), BF16)],
        compiler_params=_params("arbitrary"),
        name="modulate_in",
    )(x2d, ctx2d, mod, mod)


def _cast_weights_once(pairs):
    @pl.when(pl.program_id(1) == 0)
    def _():
        for src_ref, dst_ref in pairs:
            dst_ref[...] = src_ref[...].astype(BF16)


def _row_groups(ref, n_split):
    rows_per = ref.shape[0] // n_split
    return [pl.ds(r * rows_per, rows_per) for r in range(n_split)]


def _ret_qkv_kernel(x_ref, w_ref, cos_ref, sin_ref, o_ref, wb_ref, *, n_q, n_k, k_scale, n_split):
    j = pl.program_id(0)
    tn = o_ref.shape[1]
    _cast_weights_once([(w_ref, wb_ref)])

    def product(rows):
        return jnp.dot(x_ref[rows, :], wb_ref[...], preferred_element_type=F32)

    @pl.when(j < n_q + n_k)
    def _():
        scale = jnp.where(j < n_q, 1.0, k_scale).astype(F32)
        for rows in _row_groups(o_ref, n_split):
            acc = product(rows)
            c = cos_ref[rows, :]
            s = sin_ref[rows, :]
            for g in range(tn // MXU_DIM):
                lo = g * MXU_DIM
                x1 = acc[:, lo:lo + LANES] * scale
                x2 = acc[:, lo + LANES:lo + MXU_DIM] * scale
                o_ref[rows, lo:lo + LANES] = (x1 * c - x2 * s).astype(BF16)
                o_ref[rows, lo + LANES:lo + MXU_DIM] = (x1 * s + x2 * c).astype(BF16)

    @pl.when(j >= n_q + n_k)
    def _():
        for rows in _row_groups(o_ref, n_split):
            o_ref[rows, :] = product(rows).astype(BF16)


def _ret_qkv(u, w, layer, cos, sin, n_heads, tm):
    t, d = u.shape
    n = w.shape[2]
    dk = d // n_heads
    tn = 1024
    n_q = n_heads * dk // tn
    kern = functools.partial(_ret_qkv_kernel, n_q=n_q, n_k=n_q, k_scale=dk ** -0.5, n_split=2)
    return pl.pallas_call(
        kern,
        grid=(n // tn, t // tm),
        in_specs=[
            pl.BlockSpec((tm, d), lambda j, i: (i, 0)),
            pl.BlockSpec((None, d, tn), lambda j, i: (layer, 0, j)),
            pl.BlockSpec((tm, LANES), lambda j, i: (i, 0)),
            pl.BlockSpec((tm, LANES), lambda j, i: (i, 0)),
        ],
        out_specs=pl.BlockSpec((tm, tn), lambda j, i: (i, j)),
        out_shape=jax.ShapeDtypeStruct((t, n), BF16),
        scratch_shapes=[pltpu.VMEM((d, tn), BF16)],
        compiler_params=_params("parallel", "arbitrary"),
        name="ret_qkv",
    )(u, w, cos, sin)


def _retention_kernel(dec_ref, qf_ref, qb_ref, of_ref, ob_ref, s_ref, intra_ref, qdec_ref, kdec_ref,
                      *, n_heads, dk, dv):
    c = RET_CHUNK
    step = pl.program_id(1)

    def log_gamma(idx):
        x = dec_ref[idx][0:1, 0:1]
        return jnp.minimum(x, 0.0) - jnp.log1p(jnp.exp(-jnp.abs(x)))

    @pl.when(step == 0)
    def _():
        s_ref[...] = jnp.zeros_like(s_ref)
        row = lax.broadcasted_iota(jnp.int32, (c, c), 0)
        col = lax.broadcasted_iota(jnp.int32, (c, c), 1)
        pos = lax.broadcasted_iota(jnp.int32, (c, dk), 0).astype(F32)
        for d in range(2):
            diff = (row - col) if d == 0 else (col - row)
            dist = jnp.maximum(diff, 0).astype(F32)
            ahead = pos + 1.0 if d == 0 else float(c) - pos
            behind = float(c - 1) - pos if d == 0 else pos
            for h in range(n_heads):
                idx = d * n_heads + h
                lg = log_gamma(idx)
                intra_ref[idx] = jnp.where(diff >= 0, jnp.exp(lg * dist), 0.0)
                qdec_ref[idx] = jnp.exp(lg * ahead).astype(BF16)
                kdec_ref[idx] = jnp.exp(lg * behind).astype(BF16)

    for d, (x_ref, o_ref) in enumerate(((qf_ref, of_ref), (qb_ref, ob_ref))):
        for h in range(n_heads):
            idx = d * n_heads + h
            q = x_ref[:, h * dk:(h + 1) * dk]
            k = x_ref[:, n_heads * dk + h * dk:n_heads * dk + (h + 1) * dk]
            v = x_ref[:, 2 * n_heads * dk + h * dv:2 * n_heads * dk + (h + 1) * dv]
            scores = lax.dot_general(q, k, (((1,), (1,)), ((), ())), preferred_element_type=F32)
            p = (scores * intra_ref[idx]).astype(BF16)
            state = s_ref[idx]
            o = (jnp.dot(p, v, preferred_element_type=F32)
                 + jnp.dot(q * qdec_ref[idx], state.astype(BF16), preferred_element_type=F32))
            s_ref[idx] = state * jnp.exp(log_gamma(idx) * float(c)) + lax.dot_general(
                k * kdec_ref[idx], v, (((0,), (0,)), ((), ())), preferred_element_type=F32)
            mu = jnp.mean(o, axis=-1, keepdims=True)
            ctr = o - mu
            var = jnp.mean(ctr * ctr, axis=-1, keepdims=True)
            o_ref[:, h * dv:(h + 1) * dv] = (ctr * lax.rsqrt(var + GN_EPS)).astype(BF16)


def _retention(qkv, decay_tile, batch, seq, n_heads):
    t, n = qkv.shape
    d = n // 4
    dk, dv = d // n_heads, 2 * d // n_heads
    c = RET_CHUNK
    n_lat = seq // c
    ctx_blk = batch * n_lat

    def fwd_blk(b, s):
        return jnp.where(s == 0, ctx_blk + b, b * n_lat + s - 1)

    def bwd_blk(b, s):
        return jnp.where(s == 0, ctx_blk + b, b * n_lat + n_lat - s)

    kern = functools.partial(_retention_kernel, n_heads=n_heads, dk=dk, dv=dv)
    out = jax.ShapeDtypeStruct((t, n_heads * dv), BF16)
    return pl.pallas_call(
        kern,
        grid=(batch, n_lat + 1),
        in_specs=[
            pl.BlockSpec((2 * n_heads, 8, LANES), lambda b, s: (0, 0, 0)),
            pl.BlockSpec((c, n), lambda b, s: (fwd_blk(b, s), 0)),
            pl.BlockSpec((c, n), lambda b, s: (bwd_blk(b, s), 0)),
        ],
        out_specs=[
            pl.BlockSpec((c, n_heads * dv), lambda b, s: (fwd_blk(b, s), 0)),
            pl.BlockSpec((c, n_heads * dv), lambda b, s: (bwd_blk(b, s), 0)),
        ],
        out_shape=[out, out],
        scratch_shapes=[
            pltpu.VMEM((2 * n_heads, dk, dv), F32),
            pltpu.VMEM((2 * n_heads, c, c), F32),
            pltpu.VMEM((2 * n_heads, c, dk), BF16),
            pltpu.VMEM((2 * n_heads, c, dk), BF16),
        ],
        compiler_params=_params("parallel", "arbitrary"),
        name="retention_scan",
    )(decay_tile, qkv, qkv)


def _ret_gate_kernel(x_ref, wf_ref, wb_ref, of_ref, ob_ref, y_ref, wf_bf, wb_bf, *, n_split):
    _cast_weights_once([(wf_ref, wf_bf), (wb_ref, wb_bf)])
    for rows in _row_groups(y_ref, n_split):
        x = x_ref[rows, :]
        gf = jnp.dot(x, wf_bf[...], preferred_element_type=F32)
        gb = jnp.dot(x, wb_bf[...], preferred_element_type=F32)
        y = _silu(gf) * of_ref[rows, :].astype(F32) + _silu(gb) * ob_ref[rows, :].astype(F32)
        y_ref[rows, :] = y.astype(BF16)


def _ret_gate(u, w_g, layer, o_f, o_b, tm):
    t, d = u.shape
    n = w_g.shape[2] // 2
    tn = 512
    nb = n // tn
    return pl.pallas_call(
        functools.partial(_ret_gate_kernel, n_split=2),
        grid=(nb, t // tm),
        in_specs=[
            pl.BlockSpec((tm, d), lambda j, i: (i, 0)),
            pl.BlockSpec((None, d, tn), lambda j, i: (layer, 0, j)),
            pl.BlockSpec((None, d, tn), lambda j, i: (layer, 0, nb + j)),
            pl.BlockSpec((tm, tn), lambda j, i: (i, j)),
            pl.BlockSpec((tm, tn), lambda j, i: (i, j)),
        ],
        out_specs=pl.BlockSpec((tm, tn), lambda j, i: (i, j)),
        out_shape=jax.ShapeDtypeStruct((t, n), BF16),
        scratch_shapes=[pltpu.VMEM((d, tn), BF16), pltpu.VMEM((d, tn), BF16)],
        compiler_params=_params("parallel", "arbitrary"),
        name="ret_gate",
    )(u, w_g, w_g, o_f, o_b)


def _out_ln_kernel(*refs, alpha, n_k, n_split, group, n_x, n_lat_tiles):
    x_refs, refs = refs[:n_x], refs[n_x:]
    w_ref, h_ref, gate_ref, lng_ref, lnb_ref, sc_ref, sh_ref, hout_ref, u_ref, *acc = refs
    k = pl.program_id(1)
    is_lat = group * pl.program_id(0) + pl.program_id(2) < n_lat_tiles

    def x_rows(rows):
        if n_x == 1:
            return x_refs[0][rows, :]
        return jnp.where(is_lat, x_refs[0][rows, :], x_refs[1][rows, :])

    def partial_product(rows):
        return jnp.dot(x_rows(rows), w_ref[...], preferred_element_type=F32)

    def finish(rows, y):
        z = alpha * h_ref[rows, :] + gate_ref[0] * y
        mu = jnp.mean(z, axis=-1, keepdims=True)
        ctr = z - mu
        var = jnp.mean(ctr * ctr, axis=-1, keepdims=True)
        hn = ctr * lax.rsqrt(var + LN_EPS) * lng_ref[0] + lnb_ref[0]
        hout_ref[rows, :] = hn
        u_ref[rows, :] = (hn * (1.0 + sc_ref[0]) + sh_ref[0]).astype(BF16)

    def last_step(prev):
        for rows in _row_groups(hout_ref, n_split):
            y = partial_product(rows)
            finish(rows, y if prev is None else prev[rows, :] + y)

    if n_k == 1:
        last_step(None)
        return
    acc_ref = acc[0].at[pl.program_id(2)]

    @pl.when(k == 0)
    def _():
        acc_ref[...] = partial_product(slice(None))

    @pl.when(jnp.logical_and(k > 0, k < n_k - 1))
    def _():
        acc_ref[...] += partial_product(slice(None))

    @pl.when(k == n_k - 1)
    def _():
        last_step(acc_ref)


def _out_ln(x, w, layer, h, mod, ln_g, ln_b, *, gate_idx, ln_idx, sc_idx, sh_idx, alpha, n_rows, tm, tk,
            group, n_split=2):
    x_parts = x if isinstance(x, tuple) else (x,)
    kdim = x_parts[0].shape[1]
    d = w.shape[2]
    n_k = kdim // tk
    if n_rows % (group * tm):
        group = 1
    assert n_rows % (group * tm) == 0 and kdim % tk == 0
    n_lat_tiles = x_parts[0].shape[0] // tm if len(x_parts) == 2 else 0

    def row_blk(i, k, m):
        return group * i + jnp.where(k == n_k - 1, m, 0)

    if len(x_parts) == 1:
        x_specs = [pl.BlockSpec((tm, tk), lambda i, k, m: (group * i + m, k))]
    else:
        assert x_parts[0].shape[0] == n_lat_tiles * tm
        x_specs = [
            pl.BlockSpec((tm, tk), lambda i, k, m: (jnp.minimum(group * i + m, n_lat_tiles - 1), k)),
            pl.BlockSpec((tm, tk), lambda i, k, m: (jnp.maximum(group * i + m - n_lat_tiles, 0), k)),
        ]

    vec = lambda arr_idx: pl.BlockSpec((1, 1, d), lambda i, k, m: (arr_idx(row_blk(i, k, m)), 0, 0))
    row = pl.BlockSpec((tm, d), lambda i, k, m: (row_blk(i, k, m), 0))
    kern = functools.partial(_out_ln_kernel, alpha=alpha, n_k=n_k, n_split=n_split, group=group,
                             n_x=len(x_parts), n_lat_tiles=n_lat_tiles)
    return pl.pallas_call(
        kern,
        grid=(n_rows // (group * tm), n_k, group),
        in_specs=[
            *x_specs,
            pl.BlockSpec((None, tk, d), lambda i, k, m: (layer, k, 0)),
            row,
            vec(gate_idx),
            vec(lambda r: ln_idx),
            vec(lambda r: ln_idx),
            vec(sc_idx),
            vec(sh_idx),
        ],
        out_specs=[row, row],
        out_shape=[
            jax.ShapeDtypeStruct((n_rows, d), F32),
            jax.ShapeDtypeStruct((n_rows, d), BF16),
        ],
        scratch_shapes=[pltpu.VMEM((group, tm, d), F32)] if n_k > 1 else [],
        compiler_params=_params("parallel", "arbitrary", "arbitrary"),
        name="out_ln",
    )(*x_parts, w, h, mod, ln_g, ln_b, mod, mod)


def _ffn_in_kernel(x_ref, wa_ref, wb_ref, o_ref, wa_bf, wb_bf, *, n_split):
    _cast_weights_once([(wa_ref, wa_bf), (wb_ref, wb_bf)])
    for rows in _row_groups(o_ref, n_split):
        x = x_ref[rows, :]
        a = jnp.dot(x, wa_bf[...], preferred_element_type=F32)
        b = jnp.dot(x, wb_bf[...], preferred_element_type=F32)
        o_ref[rows, :] = (_silu(a) * b).astype(BF16)


def _ffn_in(u, w_in, layer, n_rows, tm):
    d = u.shape[1]
    f = w_in.shape[2] // 2
    tn = 512
    nb = f // tn
    return pl.pallas_call(
        functools.partial(_ffn_in_kernel, n_split=2),
        grid=(nb, n_rows // tm),
        in_specs=[
            pl.BlockSpec((tm, d), lambda j, i: (i, 0)),
            pl.BlockSpec((None, d, tn), lambda j, i: (layer, 0, j)),
            pl.BlockSpec((None, d, tn), lambda j, i: (layer, 0, nb + j)),
        ],
        out_specs=pl.BlockSpec((tm, tn), lambda j, i: (i, j)),
        out_shape=jax.ShapeDtypeStruct((n_rows, f), BF16),
        scratch_shapes=[pltpu.VMEM((d, tn), BF16), pltpu.VMEM((d, tn), BF16)],
        compiler_params=_params("parallel", "arbitrary"),
        name="ffn_in",
    )(u, w_in, w_in)


def _rope_lanes(x, c_ref, s_up_ref, s_dn_ref):
    half = MLA_D_ROPE // 4
    up = pltpu.roll(x, LANES - half, 1)
    dn = pltpu.roll(x, half, 1)
    return x * c_ref[...] + up * s_up_ref[...] + dn * s_dn_ref[...]


def _rms(x, g):
    return x * lax.rsqrt(jnp.mean(x * x, axis=-1, keepdims=True) + RMS_EPS) * g


def _mla_down_kernel(x_ref, w_ref, gq_ref, gkv_ref, c_ref, su_ref, sd_ref,
                     cq_ref, ckv_ref, kr_ref, *, rq, rkv):
    for rows in _row_groups(x_ref, 2):
        acc = jnp.dot(x_ref[rows, :], w_ref[...], preferred_element_type=F32)
        cq_ref[rows, :] = _rms(acc[:, :rq], gq_ref[...]).astype(BF16)
        ckv_ref[rows, :] = _rms(acc[:, rq:rq + rkv], gkv_ref[...]).astype(BF16)
        kr_ref[rows, :] = _rope_lanes(acc[:, rq + rkv:], c_ref.at[rows, :], su_ref.at[rows, :],
                                      sd_ref.at[rows, :]).astype(BF16)


def _mla_down(u, w_down, g_q, g_kv, tabs, tm):
    t, d = u.shape
    rq, rkv = g_q.shape[1], g_kv.shape[1]
    n = w_down.shape[1]
    tab = pl.BlockSpec((tm, LANES), lambda i: (i, 0))
    kern = functools.partial(_mla_down_kernel, rq=rq, rkv=rkv)
    return pl.pallas_call(
        kern,
        grid=(t // tm,),
        in_specs=[
            pl.BlockSpec((tm, d), lambda i: (i, 0)),
            pl.BlockSpec((d, n), lambda i: (0, 0)),
            pl.BlockSpec((1, rq), lambda i: (0, 0)),
            pl.BlockSpec((1, rkv), lambda i: (0, 0)),
            tab, tab, tab,
        ],
        out_specs=[
            pl.BlockSpec((tm, rq), lambda i: (i, 0)),
            pl.BlockSpec((tm, rkv), lambda i: (i, 0)),
            pl.BlockSpec((tm, LANES), lambda i: (i, 0)),
        ],
        out_shape=[
            jax.ShapeDtypeStruct((t, rq), BF16),
            jax.ShapeDtypeStruct((t, rkv), BF16),
            jax.ShapeDtypeStruct((t, LANES), BF16),
        ],
        compiler_params=_params("parallel"),
        name="mla_down",
    )(u, w_down, g_q, g_kv, *tabs)


def _mla_up_kernel(cq_ref, ckv_ref, kr_ref, wq_ref, wkv_ref, c_ref, su_ref, sd_ref,
                   q_ref, k_ref, v_ref):
    cq = cq_ref[...]
    ckv = ckv_ref[...]
    kr = kr_ref[...]
    hp = MLA_HEAD_PAD
    for h in range(MLA_HEADS):
        qh = jnp.dot(cq, wq_ref[:, h * hp:(h + 1) * hp], preferred_element_type=F32) * MLA_Q_SCALE
        q_ref[:, h * hp:h * hp + LANES] = qh[:, :LANES].astype(BF16)
        q_ref[:, h * hp + LANES:(h + 1) * hp] = _rope_lanes(qh[:, LANES:], c_ref, su_ref, sd_ref).astype(BF16)
        kvh = jnp.dot(ckv, wkv_ref[:, h * hp:(h + 1) * hp], preferred_element_type=F32)
        k_ref[:, h * hp:h * hp + LANES] = kvh[:, :LANES].astype(BF16)
        k_ref[:, h * hp + LANES:(h + 1) * hp] = kr
        v_ref[:, h * hp:h * hp + MLA_D_V] = kvh[:, LANES:].astype(BF16)
        v_ref[:, h * hp + MLA_D_V:(h + 1) * hp] = jnp.ones((kvh.shape[0], hp - MLA_D_V), BF16)


def _mla_up(cq, ckv, kr, wq_pad, wkv, tabs, tm):
    t, rq = cq.shape
    rkv = ckv.shape[1]
    nq = wq_pad.shape[1]
    nkv = wkv.shape[1]
    tab = pl.BlockSpec((tm, LANES), lambda i: (i, 0))
    return pl.pallas_call(
        _mla_up_kernel,
        grid=(t // tm,),
        in_specs=[
            pl.BlockSpec((tm, rq), lambda i: (i, 0)),
            pl.BlockSpec((tm, rkv), lambda i: (i, 0)),
            pl.BlockSpec((tm, LANES), lambda i: (i, 0)),
            pl.BlockSpec((rq, nq), lambda i: (0, 0)),
            pl.BlockSpec((rkv, nkv), lambda i: (0, 0)),
            tab, tab, tab,
        ],
        out_specs=[
            pl.BlockSpec((tm, nq), lambda i: (i, 0)),
            pl.BlockSpec((tm, nq), lambda i: (i, 0)),
            pl.BlockSpec((tm, nq), lambda i: (i, 0)),
        ],
        out_shape=[
            jax.ShapeDtypeStruct((t, nq), BF16),
            jax.ShapeDtypeStruct((t, nq), BF16),
            jax.ShapeDtypeStruct((t, nq), BF16),
        ],
        compiler_params=_params("parallel"),
        name="mla_up",
    )(cq, ckv, kr, wq_pad, wkv, *tabs)


def _mla_proj_kernel(x_ref, wd_ref, gq_ref, gkv_ref, wq_ref, wkv_ref, c_ref, su_ref, sd_ref,
                     q_ref, k_ref, v_ref, cq_s, ckv_s, kr_s, *, rq, rkv):
    _mla_down_kernel(x_ref, wd_ref, gq_ref, gkv_ref, c_ref, su_ref, sd_ref, cq_s, ckv_s, kr_s, rq=rq, rkv=rkv)
    _mla_up_kernel(cq_s, ckv_s, kr_s, wq_ref, wkv_ref, c_ref, su_ref, sd_ref, q_ref, k_ref, v_ref)


def _mla_proj(u, w_down, g_q, g_kv, wq_pad, wkv, tabs, tm):
    t, d = u.shape
    rq, rkv = g_q.shape[1], g_kv.shape[1]
    nq = wq_pad.shape[1]
    tab = pl.BlockSpec((tm, LANES), lambda i: (i, 0))
    resident = lambda shape: pl.BlockSpec(shape, lambda i: (0, 0), pipeline_mode=pl.Buffered(1))
    out = pl.BlockSpec((tm, nq), lambda i: (i, 0))
    return pl.pallas_call(
        functools.partial(_mla_proj_kernel, rq=rq, rkv=rkv),
        grid=(t // tm,),
        in_specs=[
            pl.BlockSpec((tm, d), lambda i: (i, 0)),
            resident(w_down.shape),
            pl.BlockSpec((1, rq), lambda i: (0, 0)),
            pl.BlockSpec((1, rkv), lambda i: (0, 0)),
            resident(wq_pad.shape),
            resident(wkv.shape),
            tab, tab, tab,
        ],
        out_specs=[out, out, out],
        out_shape=[jax.ShapeDtypeStruct((t, nq), BF16)] * 3,
        scratch_shapes=[pltpu.VMEM((tm, rq), BF16), pltpu.VMEM((tm, rkv), BF16), pltpu.VMEM((tm, LANES), BF16)],
        compiler_params=_params("parallel"),
        name="mla_proj",
    )(u, w_down, g_q, g_kv, wq_pad, wkv, *tabs)


def _scores(q, k):
    return lax.dot_general(q, k, (((1,), (1,)), ((), ())), preferred_element_type=F32)


def _attn_lat_kernel(q_ref, kl_ref, kc_ref, vl_ref, vc_ref, o_ref, *, key_chunk):
    q = q_ref[...]
    n_lat = kl_ref.shape[0] // key_chunk
    chunks = [(kl_ref, vl_ref, c * key_chunk, key_chunk) for c in range(n_lat)]
    chunks.append((kc_ref, vc_ref, 0, kc_ref.shape[0]))

    def chunk_scores(c):
        k_ref, _, lo, size = chunks[c]
        return _scores(q, k_ref[lo:lo + size, :])

    s = chunk_scores(0)
    m = acc = None
    for c, (_, v_ref, lo, size) in enumerate(chunks):
        s_next = chunk_scores(c + 1) if c + 1 < len(chunks) else None
        m_c = jnp.max(s, axis=-1, keepdims=True)
        v = v_ref[lo:lo + size, :]
        if c == 0:
            m = m_c
            acc = jnp.dot(jnp.exp2(s - m).astype(BF16), v, preferred_element_type=F32)
        else:
            m_new = jnp.maximum(m, m_c)
            acc = (jnp.exp2(m - m_new) * acc
                   + jnp.dot(jnp.exp2(s - m_new).astype(BF16), v, preferred_element_type=F32))
            m = m_new
        s = s_next
    _store_normalised(o_ref, acc)


def _store_normalised(o_ref, acc):
    o_ref[...] = (acc[:, :MLA_D_V] / acc[:, MLA_D_V:MLA_D_V + 1]).astype(BF16)


def _attn_ctx_kernel(q_ref, kc_ref, vc_ref, o_ref):
    hp, dv = MLA_HEAD_PAD, MLA_D_V
    for h in range(MLA_HEADS):
        cols = slice(h * hp, (h + 1) * hp)
        s_c = _scores(q_ref[:, cols], kc_ref[:, cols])
        p_c = jnp.exp2(s_c - jnp.max(s_c, axis=-1, keepdims=True))
        _store_normalised(o_ref.at[:, h * dv:(h + 1) * dv],
                          jnp.dot(p_c.astype(BF16), vc_ref[:, cols], preferred_element_type=F32))


def _attention(q, k, v, batch, seq, ctx_len, with_ctx_queries):
    hp, dv = MLA_HEAD_PAD, MLA_D_V
    tq = min(seq, ATTN_Q_TILE)
    nq = seq // tq
    ctx0 = batch * seq // ctx_len
    o_lat = pl.pallas_call(
        functools.partial(_attn_lat_kernel, key_chunk=min(seq, ATTN_KEY_CHUNK)),
        grid=(batch, MLA_HEADS, nq),
        in_specs=[
            pl.BlockSpec((tq, hp), lambda b, h, i: (b * nq + i, h)),
            pl.BlockSpec((seq, hp), lambda b, h, i: (b, h)),
            pl.BlockSpec((ctx_len, hp), lambda b, h, i: (ctx0 + b, h)),
            pl.BlockSpec((seq, hp), lambda b, h, i: (b, h)),
            pl.BlockSpec((ctx_len, hp), lambda b, h, i: (ctx0 + b, h)),
        ],
        out_specs=pl.BlockSpec((tq, dv), lambda b, h, i: (b * nq + i, h)),
        out_shape=jax.ShapeDtypeStruct((batch * seq, MLA_HEADS * dv), BF16),
        compiler_params=_params("parallel", "parallel", "arbitrary"),
        name="attn_lat",
    )(q, k, k, v, v)
    if not with_ctx_queries:
        return o_lat
    o_ctx = pl.pallas_call(
        _attn_ctx_kernel,
        grid=(batch,),
        in_specs=[
            pl.BlockSpec((ctx_len, MLA_HEADS * hp), lambda b: (ctx0 + b, 0)),
            pl.BlockSpec((ctx_len, MLA_HEADS * hp), lambda b: (ctx0 + b, 0)),
            pl.BlockSpec((ctx_len, MLA_HEADS * hp), lambda b: (ctx0 + b, 0)),
        ],
        out_specs=pl.BlockSpec((ctx_len, MLA_HEADS * dv), lambda b: (b, 0)),
        out_shape=jax.ShapeDtypeStruct((batch * ctx_len, MLA_HEADS * dv), BF16),
        compiler_params=_params("parallel"),
        name="attn_ctx",
    )(q, k, v)
    return o_lat, o_ctx


def _retention_tables(seq, n_ctx, dk):
    inv = RET_ROPE_BASE ** (-jnp.linspace(0.0, 1.0, dk // 2, dtype=F32))
    ang = jnp.arange(seq, dtype=F32)[:, None] * inv[None, :]
    return jnp.cos(ang), jnp.sin(ang), jnp.ones((n_ctx, dk // 2), F32), jnp.zeros((n_ctx, dk // 2), F32)


def _axial_tables(seq):
    n_f = MLA_D_ROPE // 4
    inv = AXIAL_ROPE_BASE ** (-jnp.arange(n_f, dtype=F32) * 2.0 / (MLA_D_ROPE // 2))
    tpos = jnp.arange(seq)
    row = (tpos // GRID_W).astype(F32)[:, None] * inv[None, :]
    col = (tpos % GRID_W).astype(F32)[:, None] * inv[None, :]
    z = jnp.zeros((seq, n_f), F32)
    tail = LANES - MLA_D_ROPE
    cos = jnp.concatenate([jnp.cos(row), jnp.cos(row), jnp.cos(col), jnp.cos(col), jnp.ones((seq, tail), F32)], 1)
    s_up = jnp.concatenate([-jnp.sin(row), z, -jnp.sin(col), z, jnp.zeros((seq, tail), F32)], 1)
    s_dn = jnp.concatenate([z, jnp.sin(row), z, jnp.sin(col), jnp.zeros((seq, tail), F32)], 1)
    return cos, s_up, s_dn


def _token_table(lat, batch, ctx_rows, fill):
    return jnp.concatenate([jnp.tile(lat, (batch, 1)), jnp.full((ctx_rows, lat.shape[1]), fill, F32)], 0)


def kernel(x, c, ctx, c_ctx, ada_w, ada_b, ln_g, ln_b, ret_w_qkv, ret_w_g, ret_decay_logit, ret_w_o,
           mla_w_dq, mla_g_q, mla_w_uq, mla_w_dkv, mla_g_kv, mla_w_ukv, mla_w_o, ffn_w_in, ffn_w_out):
    batch, seq, d = x.shape
    ctx_len = ctx.shape[1]
    depth = ada_w.shape[0]
    n_lat, n_ctx = batch * seq, batch * ctx_len
    n_tok = n_lat + n_ctx
    ret_heads = ret_decay_logit.shape[-1]
    assert ctx_len == RET_CHUNK and seq % RET_CHUNK == 0 and batch + 1 <= 8
    tm = _row_tile(n_lat, n_ctx)
    alpha = (2 * depth) ** 0.25

    cond8 = jnp.concatenate([c, c_ctx[None, :], jnp.zeros((8 - batch - 1, d), F32)], 0)
    mod = _modulation(cond8, ada_w, ada_b).reshape(depth * 8 * 6, 1, d)

    def mod_idx(layer, chunk, tile):
        def f(i):
            r = jnp.where(i * tile >= n_lat, batch, (i * tile) // seq)
            return (layer * 8 + r) * 6 + chunk
        return f

    ln_g3 = ln_g.reshape(depth * 2, 1, d)
    ln_b3 = ln_b.reshape(depth * 2, 1, d)

    ret_cos, ret_sin, one_c, zero_c = _retention_tables(seq, n_ctx, d // ret_heads)
    ret_cos = jnp.concatenate([jnp.tile(ret_cos, (batch, 1)), one_c], 0)
    ret_sin = jnp.concatenate([jnp.tile(ret_sin, (batch, 1)), zero_c], 0)
    ax_cos, ax_up, ax_dn = _axial_tables(seq)
    ax_tabs = (_token_table(ax_cos, batch, n_ctx, 1.0), _token_table(ax_up, batch, n_ctx, 0.0),
               _token_table(ax_dn, batch, n_ctx, 0.0))

    h, u = _modulate(x.reshape(n_lat, d), ctx.reshape(n_ctx, d), mod,
                     mod_idx(0, 1, TM_SMALL), mod_idx(0, 0, TM_SMALL), TM_SMALL)

    ret_w_o_b, mla_w_o_b, ffn_w_out_b = (w.astype(BF16) for w in (ret_w_o, mla_w_o, ffn_w_out))

    for i in range(depth):
        last = i == depth - 1
        n_rows = n_lat if last else n_tok
        j = i // 2
        if i % 2 == 0:
            qkv = _ret_qkv(u, ret_w_qkv, j, ret_cos, ret_sin, ret_heads, tm)
            decay_tile = jnp.broadcast_to(
                ret_decay_logit[j].astype(F32).reshape(2 * ret_heads, 1, 1), (2 * ret_heads, 8, LANES))
            o_f, o_b = _retention(qkv, decay_tile, batch, seq, ret_heads)
            y = _ret_gate(u, ret_w_g, j, o_f, o_b, tm)
            w_o = ret_w_o_b
        else:
            rq, rkv = mla_g_q.shape[1], mla_g_kv.shape[1]
            w_down = jnp.concatenate(
                [mla_w_dq[j], mla_w_dkv[j], jnp.zeros((d, LANES - MLA_D_ROPE), F32)], 1).astype(BF16)
            wq_pad = jnp.pad(
                mla_w_uq[j].reshape(rq, MLA_HEADS, MLA_D_NOPE + MLA_D_ROPE),
                ((0, 0), (0, 0), (0, MLA_HEAD_PAD - MLA_D_NOPE - MLA_D_ROPE)),
            ).reshape(rq, MLA_HEADS * MLA_HEAD_PAD).astype(BF16)
            q, k, v = _mla_proj(u, w_down, mla_g_q[j][None, :], mla_g_kv[j][None, :], wq_pad,
                                mla_w_ukv[j].astype(BF16), ax_tabs, TM_SMALL)
            y = _attention(q, k, v, batch, seq, ctx_len, not last)
            w_o = mla_w_o_b

        tm_o, grp_o, tk_o = OUT_TILES_MIXER
        h, u = _out_ln(y, w_o, j, h, mod, ln_g3, ln_b3, gate_idx=mod_idx(i, 2, tm_o), ln_idx=2 * i,
                       sc_idx=mod_idx(i, 4, tm_o), sh_idx=mod_idx(i, 3, tm_o), alpha=alpha,
                       n_rows=n_rows, tm=tm_o, tk=min(tk_o, w_o.shape[1]), group=grp_o)
        hid = _ffn_in(u, ffn_w_in, i, n_rows, tm)
        nxt = min(i + 1, depth - 1)
        tm_o, grp_o, k_steps = OUT_TILES_FFN
        h, u = _out_ln(hid, ffn_w_out_b, i, h, mod, ln_g3, ln_b3, gate_idx=mod_idx(i, 5, tm_o),
                       ln_idx=2 * i + 1, sc_idx=mod_idx(nxt, 1, tm_o), sh_idx=mod_idx(nxt, 0, tm_o),
                       alpha=alpha, n_rows=n_rows, tm=tm_o, tk=ffn_w_out.shape[1] // k_steps, group=grp_o)
    return h.reshape(batch, seq, d)
```
